```python
import math
import jax, jax.numpy as jnp
from jax import lax
import numpy as np

D_MODEL = 1024
BATCH = 32
SEQ = 256
DEPTH = 4
DEC_BATCH = 2
DEC_SEQ = 4096
PAST_LEN = 256

GRID_W = 64
MIX_W = D_MODEL
S5_W = MIX_W // 2
S5_P = 16
S5_G = S5_W // S5_P
S5_N = 64
GDN_W = MIX_W - S5_W
GDN_DK = 128
GDN_DV = 128
GDN_H = GDN_W // GDN_DK
CONV_K = 3
CHUNK = 64
N_DIR = 2
D_FF = 4 * D_MODEL
N_MOD = 6
IN_W = 2 * S5_W + 4 * GDN_W + 2 * N_DIR * GDN_H
SPLITS = (S5_W, 2 * S5_W, 2 * S5_W + 3 * GDN_W, 2 * S5_W + 4 * GDN_W,
          2 * S5_W + 4 * GDN_W + N_DIR * GDN_H)
POS_BASE = 10000.0
EPS = 1e-6
F32 = jnp.float32

kernel_name = 'hymba_s5_gdn_prefix_dit_step'


def rmsnorm(x, g):
    xf = x.astype(F32)
    y = xf * lax.rsqrt(jnp.mean(xf * xf, axis=-1, keepdims=True) + EPS)
    return (y * g.astype(F32)).astype(x.dtype)


def l2norm(x):
    return x * lax.rsqrt(jnp.sum(x * x, axis=-1, keepdims=True) + EPS)


def rev(t):
    return jnp.flip(t, axis=1)


def grid_pos_embed(n_tok):
    rows = n_tok // GRID_W
    r, col = jnp.meshgrid(jnp.arange(rows, dtype=F32), jnp.arange(GRID_W, dtype=F32), indexing='ij')
    r = r.reshape(-1, 1)
    col = col.reshape(-1, 1)
    quarter = D_MODEL // 4
    omega = 1.0 / (POS_BASE ** (jnp.arange(quarter, dtype=F32) / quarter))
    ang_r = r * omega
    ang_c = col * omega
    return jnp.concatenate([jnp.sin(ang_r), jnp.cos(ang_r), jnp.sin(ang_c), jnp.cos(ang_c)], axis=-1)


def modulation(cond, w_ada, b_ada):
    m = jnp.dot(jax.nn.silu(cond), w_ada) + b_ada
    return jnp.split(m[..., None, :], N_MOD, axis=-1)


def short_conv(x, w):
    pad = (CONV_K - 1) // 2
    return lax.conv_general_dilated(x, w[:, None, :], window_strides=(1,),
                                    padding=[(pad, CONV_K - 1 - pad)],
                                    dimension_numbers=('NWC', 'WIO', 'NWC'),
                                    feature_group_count=x.shape[-1])


def s5_discretise(lam_re, lam_im, log_dt, b_re, b_im):
    lam_re = jnp.minimum(lam_re, -1e-4)
    dt = jnp.exp(log_dt)[:, None]
    mag = jnp.exp(lam_re * dt)
    abar_re = mag * jnp.cos(lam_im * dt)
    abar_im = mag * jnp.sin(lam_im * dt)
    num_re = abar_re - 1.0
    num_im = abar_im
    den = lam_re * lam_re + lam_im * lam_im
    f_re = (num_re * lam_re + num_im * lam_im) / den
    f_im = (num_im * lam_re - num_re * lam_im) / den
    bbar_re = f_re[..., None] * b_re - f_im[..., None] * b_im
    bbar_im = f_re[..., None] * b_im + f_im[..., None] * b_re
    return abar_re, abar_im, bbar_re, bbar_im


def complex_linear_scan(abar_re, abar_im, bu_re, bu_im, h0_re, h0_im):
    bu_re = bu_re.at[:, 0].add(abar_re * h0_re - abar_im * h0_im)
    bu_im = bu_im.at[:, 0].add(abar_re * h0_im + abar_im * h0_re)
    a_re = jnp.broadcast_to(abar_re, bu_re.shape)
    a_im = jnp.broadcast_to(abar_im, bu_im.shape)

    def combine(e1, e2):
        a1r, a1i, b1r, b1i = e1
        a2r, a2i, b2r, b2i = e2
        return (a2r * a1r - a2i * a1i, a2r * a1i + a2i * a1r,
                a2r * b1r - a2i * b1i + b2r, a2r * b1i + a2i * b1r + b2i)

    _, _, h_re, h_im = lax.associative_scan(combine, (a_re, a_im, bu_re, bu_im), axis=1)
    return h_re, h_im


def s5_mixer(u, z, lam_re, lam_im, log_dt, b_re, b_im, c_re, c_im, d_skip, h0_re, h0_im):
    bsz, length, _ = u.shape
    u = u.astype(F32).reshape(bsz, length, S5_G, S5_P)
    y = u * d_skip.astype(F32).reshape(S5_G, S5_P)
    fin_re, fin_im = [], []
    for d in range(N_DIR):
        abr, abi, bbr, bbi = s5_discretise(lam_re[d].astype(F32), lam_im[d].astype(F32),
                                           log_dt[d].astype(F32), b_re[d].astype(F32), b_im[d].astype(F32))
        ud = u if d == 0 else rev(u)
        bu_re = jnp.einsum('gnp,blgp->blgn', bbr, ud)
        bu_im = jnp.einsum('gnp,blgp->blgn', bbi, ud)
        hr, hi = complex_linear_scan(abr, abi, bu_re, bu_im,
                                     h0_re[:, d].astype(F32), h0_im[:, d].astype(F32))
        yd = (jnp.einsum('gpn,blgn->blgp', c_re[d].astype(F32), hr)
              - jnp.einsum('gpn,blgn->blgp', c_im[d].astype(F32), hi))
        y = y + (yd if d == 0 else rev(yd))
        fin_re.append(hr[:, -1])
        fin_im.append(hi[:, -1])
    out = jax.nn.gelu(y.reshape(bsz, length, S5_W)) * jax.nn.sigmoid(z.astype(F32))
    return out, jnp.stack(fin_re, axis=1), jnp.stack(fin_im, axis=1)


def gated_delta_chunked(q, k, v, g, beta, s0):
    bsz, length, nh, _ = q.shape
    nc = length // CHUNK

    def to_chunks(t):
        return t.reshape(bsz, nc, CHUNK, nh, -1).transpose(1, 0, 3, 2, 4)

    qc, kc, vc = to_chunks(q), to_chunks(k), to_chunks(v)
    gc = g.reshape(bsz, nc, CHUNK, nh).transpose(1, 0, 3, 2)
    bc = beta.reshape(bsz, nc, CHUNK, nh).transpose(1, 0, 3, 2)
    gcum = jnp.cumsum(gc, axis=-1)
    tril_incl = jnp.tril(jnp.ones((CHUNK, CHUNK), dtype=bool))
    tril_strict = jnp.tril(jnp.ones((CHUNK, CHUNK), dtype=bool), k=-1)
    diff = gcum[..., :, None] - gcum[..., None, :]
    decay = jnp.where(tril_incl, jnp.exp(jnp.where(tril_incl, diff, 0.0)), 0.0)
    kb = kc * bc[..., None]
    vb = vc * bc[..., None]
    m = jnp.where(tril_strict, jnp.einsum('nbhid,nbhjd->nbhij', kb, kc) * decay, 0.0)
    eye = jnp.eye(CHUNK, dtype=F32)
    t_inv = lax.linalg.triangular_solve(eye + m, jnp.broadcast_to(eye, m.shape),
                                        left_side=True, lower=True, unit_diagonal=True)
    u_c = jnp.matmul(t_inv, vb)
    w_c = jnp.matmul(t_inv, kb * jnp.exp(gcum)[..., None])
    a_intra = jnp.where(tril_incl, jnp.einsum('nbhid,nbhjd->nbhij', qc, kc) * decay, 0.0)

    def step(s, xs):
        q_i, k_i, u_i, w_i, a_i, g_i = xs
        v_new = u_i - jnp.matmul(w_i, s)
        o = jnp.matmul(q_i * jnp.exp(g_i)[..., None], s) + jnp.matmul(a_i, v_new)
        g_last = g_i[..., -1]
        s = (s * jnp.exp(g_last)[..., None, None]
             + jnp.einsum('bhcd,bhce->bhde', k_i * jnp.exp(g_last[..., None] - g_i)[..., None], v_new))
        return s, o

    s_fin, o = lax.scan(step, s0, (qc, kc, u_c, w_c, a_intra, gcum))
    o = o.transpose(1, 0, 3, 2, 4).reshape(bsz, length, nh, -1)
    return o, s_fin


def gdn_mixer(qkv, z, beta_raw, a_raw, conv_w, a_log, dt_bias, norm_w, s0):
    bsz, length, _ = qkv.shape
    qkv = jax.nn.silu(short_conv(qkv, conv_w)).astype(F32)
    q, k, v = jnp.split(qkv, 3, axis=-1)
    q = l2norm(q.reshape(bsz, length, GDN_H, GDN_DK)) * (GDN_DK ** -0.5)
    k = l2norm(k.reshape(bsz, length, GDN_H, GDN_DK))
    v = v.reshape(bsz, length, GDN_H, GDN_DV)
    beta = jax.nn.sigmoid(beta_raw.astype(F32).reshape(bsz, length, N_DIR, GDN_H))
    g = -jnp.exp(a_log.astype(F32)) * jax.nn.softplus(
        a_raw.astype(F32).reshape(bsz, length, N_DIR, GDN_H) + dt_bias.astype(F32))
    outs, fins = [], []
    for d in range(N_DIR):
        if d == 0:
            od, sd = gated_delta_chunked(q, k, v, g[:, :, d], beta[:, :, d], s0[:, d].astype(F32))
        else:
            od, sd = gated_delta_chunked(rev(q), rev(k), rev(v), rev(g[:, :, d]), rev(beta[:, :, d]),
                                         s0[:, d].astype(F32))
            od = rev(od)
        outs.append(od)
        fins.append(sd)
    o = outs[0] + outs[1]
    o = rmsnorm(o, norm_w) * jax.nn.silu(z.astype(F32).reshape(bsz, length, GDN_H, GDN_DV))
    return o.reshape(bsz, length, GDN_W), jnp.stack(fins, axis=1)


def mixer(h, lp, s5_h0_re, s5_h0_im, gdn_s0):
    proj = jnp.dot(h, lp['w_in'])
    u, z_s5, qkv, z_gdn, beta_raw, a_raw = jnp.split(proj, SPLITS, axis=-1)
    s5_out, s5_re, s5_im = s5_mixer(u, z_s5, lp['s5_lambda_re'], lp['s5_lambda_im'], lp['s5_log_dt'],
                                    lp['s5_b_re'], lp['s5_b_im'], lp['s5_c_re'], lp['s5_c_im'],
                                    lp['s5_d'], s5_h0_re, s5_h0_im)
    gdn_out, gdn_state = gdn_mixer(qkv, z_gdn, beta_raw, a_raw, lp['conv_qkv'], lp['gdn_a_log'],
                                   lp['gdn_dt_bias'], lp['gdn_norm'], gdn_s0)
    mixed = jnp.concatenate([s5_out, gdn_out], axis=-1).astype(h.dtype)
    return jnp.dot(mixed, lp['w_out']), s5_re, s5_im, gdn_state


def trunk_layer(x, mods, lp, s5_h0_re, s5_h0_im, gdn_s0):
    sh1, sc1, gt1, sh2, sc2, gt2 = mods
    h = rmsnorm(x, lp['norm_mix']) * (1 + sc1) + sh1
    mix, s5_re, s5_im, gdn_state = mixer(h, lp, s5_h0_re, s5_h0_im, gdn_s0)
    x = x + gt1 * mix
    h = rmsnorm(x, lp['norm_mlp']) * (1 + sc2) + sh2
    ff = jnp.square(jax.nn.relu(jnp.dot(h, lp['w_mlp_in'])))
    x = x + gt2 * jnp.dot(ff, lp['w_mlp_out'])
    return x, s5_re, s5_im, gdn_state


def setup_inputs(seed: int = 0) -> dict:
    key = jax.random.key(seed)
    ks = jax.random.split(key, 32)

    def nrm(k, shape, s):
        return jax.random.normal(k, shape, F32) * s

    s5_shape = (DEPTH, N_DIR, S5_G, S5_N)
    dt_gdn = jnp.exp(jax.random.uniform(ks[22], (DEPTH, N_DIR, GDN_H), F32, math.log(1e-3), math.log(1e-1)))
    n_idx = jnp.arange(S5_N, dtype=F32)
    return {
        'x_prompt': nrm(ks[0], (BATCH, SEQ, D_MODEL), 1.0),
        'x_sample': nrm(ks[1], (DEC_BATCH, DEC_SEQ, D_MODEL), 1.0),
        'state_s5_re': nrm(ks[2], (DEC_BATCH, DEPTH, N_DIR, S5_G, S5_N), 0.1),
        'state_s5_im': nrm(ks[3], (DEC_BATCH, DEPTH, N_DIR, S5_G, S5_N), 0.1),
        'state_gdn': nrm(ks[4], (DEC_BATCH, DEPTH, N_DIR, GDN_H, GDN_DK, GDN_DV), 0.05),
        'c': nrm(ks[5], (DEC_BATCH, D_MODEL), 1.0),
        'c_ctx': nrm(ks[6], (D_MODEL,), 1.0),
        'norm_mix': 1.0 + nrm(ks[7], (DEPTH, D_MODEL), 0.02),
        'norm_mlp': 1.0 + nrm(ks[8], (DEPTH, D_MODEL), 0.02),
        'w_ada': nrm(ks[9], (DEPTH, D_MODEL, N_MOD * D_MODEL), 0.5 * D_MODEL ** -0.5),
        'b_ada': nrm(ks[10], (DEPTH, N_MOD * D_MODEL), 0.02),
        'w_in': nrm(ks[11], (DEPTH, D_MODEL, IN_W), D_MODEL ** -0.5),
        'conv_qkv': nrm(ks[12], (DEPTH, CONV_K, 3 * GDN_W), CONV_K ** -0.5),
        's5_lambda_re': -0.5 + nrm(ks[13], s5_shape, 0.01),
        's5_lambda_im': jnp.pi * n_idx + nrm(ks[14], s5_shape, 0.01),
        's5_log_dt': jax.random.uniform(ks[15], (DEPTH, N_DIR, S5_G), F32, math.log(1e-3), math.log(1e-1)),
        's5_b_re': nrm(ks[16], (DEPTH, N_DIR, S5_G, S5_N, S5_P), (2 * S5_P) ** -0.5),
        's5_b_im': nrm(ks[17], (DEPTH, N_DIR, S5_G, S5_N, S5_P), (2 * S5_P) ** -0.5),
        's5_c_re': nrm(ks[18], (DEPTH, N_DIR, S5_G, S5_P, S5_N), (2 * S5_N) ** -0.5),
        's5_c_im': nrm(ks[19], (DEPTH, N_DIR, S5_G, S5_P, S5_N), (2 * S5_N) ** -0.5),
        's5_d': nrm(ks[20], (DEPTH, S5_W), 1.0),
        'gdn_a_log': jnp.log(jax.random.uniform(ks[21], (DEPTH, N_DIR, GDN_H), F32, 1.0, 16.0)),
        'gdn_dt_bias': dt_gdn + jnp.log(-jnp.expm1(-dt_gdn)),
        'gdn_norm': 1.0 + nrm(ks[23], (DEPTH, GDN_DV), 0.02),
        'w_out': nrm(ks[24], (DEPTH, MIX_W, D_MODEL), MIX_W ** -0.5),
        'w_mlp_in': nrm(ks[25], (DEPTH, D_MODEL, D_FF), D_MODEL ** -0.5),
        'w_mlp_out': nrm(ks[26], (DEPTH, D_FF, D_MODEL), D_FF ** -0.5),
        'norm_final': 1.0 + nrm(ks[27], (D_MODEL,), 0.02),
    }


def reference(x_prompt, x_sample, state_s5_re, state_s5_im, state_gdn, c, c_ctx,
              norm_mix, norm_mlp, w_ada, b_ada, w_in, conv_qkv,
              s5_lambda_re, s5_lambda_im, s5_log_dt, s5_b_re, s5_b_im, s5_c_re, s5_c_im, s5_d,
              gdn_a_log, gdn_dt_bias, gdn_norm, w_out, w_mlp_in, w_mlp_out, norm_final):
    n_req = x_prompt.shape[0]
    x_ctx = x_prompt
    x_lat = x_sample + grid_pos_embed(x_sample.shape[1]).astype(x_sample.dtype)
    zero_s5 = jnp.zeros((n_req, N_DIR, S5_G, S5_N), F32)
    zero_gdn = jnp.zeros((n_req, N_DIR, GDN_H, GDN_DK, GDN_DV), F32)
    ctx_re, ctx_im, ctx_gdn = [], [], []
    for l in range(DEPTH):
        lp = {
            'norm_mix': norm_mix[l], 'norm_mlp': norm_mlp[l], 'w_in': w_in[l], 'conv_qkv': conv_qkv[l],
            's5_lambda_re': s5_lambda_re[l], 's5_lambda_im': s5_lambda_im[l], 's5_log_dt': s5_log_dt[l],
            's5_b_re': s5_b_re[l], 's5_b_im': s5_b_im[l], 's5_c_re': s5_c_re[l], 's5_c_im': s5_c_im[l],
            's5_d': s5_d[l], 'gdn_a_log': gdn_a_log[l], 'gdn_dt_bias': gdn_dt_bias[l],
            'gdn_norm': gdn_norm[l], 'w_out': w_out[l], 'w_mlp_in': w_mlp_in[l], 'w_mlp_out': w_mlp_out[l],
        }
        x_ctx, s_re, s_im, s_gdn = trunk_layer(x_ctx, modulation(c_ctx, w_ada[l], b_ada[l]), lp,
                                               zero_s5, zero_s5, zero_gdn)
        ctx_re.append(s_re)
        ctx_im.append(s_im)
        ctx_gdn.append(s_gdn)
        x_lat, _, _, _ = trunk_layer(x_lat, modulation(c, w_ada[l], b_ada[l]), lp,
                                     state_s5_re[:, l], state_s5_im[:, l], state_gdn[:, l])
    y_prompt = rmsnorm(x_ctx, norm_final)
    y_sample = rmsnorm(x_lat, norm_final)
    new_state_s5_re = jnp.stack(ctx_re, axis=1)
    new_state_s5_im = jnp.stack(ctx_im, axis=1)
    new_state_gdn = jnp.stack(ctx_gdn, axis=1)
    return (y_prompt, y_sample, new_state_s5_re, new_state_s5_im, new_state_gdn)
```

```python
import functools
import math

import numpy as np
import jax
import jax.numpy as jnp
from jax import lax
from jax.experimental import pallas as pl
from jax.experimental.pallas import tpu as pltpu

F32 = jnp.float32
BF16 = jnp.bfloat16
HIGHEST = lax.Precision.HIGHEST

D_MODEL = 1024
S5_W = 512
S5_P = 16
S5_G = 32
S5_N = 64
GDN_W = 512
GDN_DK = 128
GDN_H = 4
CHUNK = 64
N_DIR = 2
N_MOD = 6
D_FF = 4 * D_MODEL
GRID_W = 64
POS_BASE = 10000.0
EPS = 1e-6
S5_T = 16
S5_TP = S5_T * S5_P
N_GATE = 2 * N_DIR * GDN_H
GATE_PAD = 128
TM = 256
VMEM_LIMIT = 56 * 1024 * 1024


def _dot(a, b):
    return jnp.dot(a, b, preferred_element_type=F32)


def _dot_nt(a, b):
    return lax.dot_general(a, b, (((1,), (1,)), ((), ())), preferred_element_type=F32)


def _dot_tn(a, b):
    return lax.dot_general(a, b, (((0,), (0,)), ((), ())), preferred_element_type=F32)


def _sigmoid(x):
    return 1.0 / (1.0 + jnp.exp(-x))


def _silu(x):
    return x * _sigmoid(x)


def _softplus(x):
    return jnp.maximum(x, 0.0) + jnp.log1p(jnp.exp(-jnp.abs(x)))


def _gelu_tanh(x):
    c = math.sqrt(2.0 / math.pi)
    return 0.5 * x * (1.0 + jnp.tanh(c * (x + 0.044715 * (x * x * x))))


def _params(*sem):
    return pltpu.CompilerParams(dimension_semantics=sem, vmem_limit_bytes=VMEM_LIMIT)


def _mod_kernel(c_ref, w_ref, b_ref, o_ref):
    sc = _silu(c_ref[...])
    o_ref[0] = jnp.dot(sc, w_ref[0], precision=HIGHEST, preferred_element_type=F32) + b_ref[0]


def _modulation(cond, w_ada, b_ada):
    depth = w_ada.shape[0]
    tn = 1536
    n_w = N_MOD * D_MODEL
    return pl.pallas_call(
        _mod_kernel,
        grid=(depth, n_w // tn),
        in_specs=[pl.BlockSpec((8, D_MODEL), lambda l, j: (0, 0)),
                  pl.BlockSpec((1, D_MODEL, tn), lambda l, j: (l, 0, j)),
                  pl.BlockSpec((1, 1, tn), lambda l, j: (l, 0, j))],
        out_specs=pl.BlockSpec((1, 8, tn), lambda l, j: (l, 0, j)),
        out_shape=jax.ShapeDtypeStruct((depth, 8, n_w), F32),
        compiler_params=_params("parallel", "parallel"),
        name="modulation",
    )(cond, w_ada, b_ada.reshape(depth, 1, n_w))


def _embed_kernel(x_ref, tr_ref, tc_ref, o_ref, *, rows_per_tile, tiles_per_seq):
    r0 = (pl.program_id(0) % tiles_per_seq) * rows_per_tile
    half = D_MODEL // 2
    tc = tc_ref[...]
    for k in range(rows_per_tile):
        lo, hi = k * GRID_W, (k + 1) * GRID_W
        o_ref[lo:hi, 0:half] = x_ref[lo:hi, 0:half] + tr_ref[pl.ds(r0 + k, 1), :]
        o_ref[lo:hi, half:D_MODEL] = x_ref[lo:hi, half:D_MODEL] + tc


def _embed(x_sample):
    n_seq, length, _ = x_sample.shape
    rows = length // GRID_W
    quarter = D_MODEL // 4
    omega = 1.0 / (POS_BASE ** (jnp.arange(quarter, dtype=F32) / quarter))
    ang_r = jnp.arange(rows, dtype=F32)[:, None] * omega
    ang_c = jnp.arange(GRID_W, dtype=F32)[:, None] * omega
    tab_r = jnp.concatenate([jnp.sin(ang_r), jnp.cos(ang_r)], axis=-1)
    tab_c = jnp.concatenate([jnp.sin(ang_c), jnp.cos(ang_c)], axis=-1)
    rows_per_tile = 8
    tile = rows_per_tile * GRID_W
    tiles_per_seq = length // tile
    x2 = x_sample.reshape(n_seq * length, D_MODEL)
    return pl.pallas_call(
        functools.partial(_embed_kernel, rows_per_tile=rows_per_tile, tiles_per_seq=tiles_per_seq),
        grid=(n_seq * tiles_per_seq,),
        in_specs=[pl.BlockSpec((tile, D_MODEL), lambda i: (i, 0)),
                  pl.BlockSpec((rows, D_MODEL // 2), lambda i: (0, 0)),
                  pl.BlockSpec((GRID_W, D_MODEL // 2), lambda i: (0, 0))],
        out_specs=pl.BlockSpec((tile, D_MODEL), lambda i: (i, 0)),
        out_shape=jax.ShapeDtypeStruct(x2.shape, F32),
        compiler_params=_params("parallel"),
        name="pos_embed",
    )(x2, tab_r, tab_c)


def _norm_mod(x, g, shift, scale):
    ms = jnp.mean(x * x, axis=-1, keepdims=True)
    h = x * lax.rsqrt(ms + EPS) * g
    return h * (1.0 + scale) + shift


def _in_kernel(x_ref, m_ref, g_ref, wu_ref, wz_ref, wqkv_ref, wzg_ref, wba_ref, wbat_ref,
               u_ref, z_ref, qkv_ref, zg_ref, ba_ref, bat_ref):
    m = m_ref[0]
    h = _norm_mod(x_ref[...], g_ref[...], m[:, 0:D_MODEL], m[:, D_MODEL:2 * D_MODEL]).astype(BF16)
    u_ref[...] = _dot(h, wu_ref[...])
    z_ref[...] = _dot(h, wz_ref[...])
    qkv_ref[...] = _dot(h, wqkv_ref[...])
    zg_ref[...] = _dot(h, wzg_ref[...])
    ba_ref[...] = _dot(h, wba_ref[...])
    bat_ref[...] = _dot_nt(wbat_ref[...], h)


def _mod_row_map(n_ctx_tiles, tiles_per_lat):
    def index(i):
        return (jnp.where(i < n_ctx_tiles, 0, 1 + (i - n_ctx_tiles) // tiles_per_lat), 0, 0)
    return index


def _in_proj(x, mods_l, g, w_in, mod_map):
    tokens = x.shape[0]
    o1, o2, o3, o4 = S5_W, 2 * S5_W, 2 * S5_W + 3 * GDN_W, 2 * S5_W + 4 * GDN_W
    wb = w_in.astype(BF16)
    w_ba = jnp.pad(wb[:, o4:], ((0, 0), (0, GATE_PAD - N_GATE)))
    w_bat = wb[:, o4:].T
    const = lambda i: (0, 0)
    tile = lambda n: pl.BlockSpec((TM, n), lambda i: (i, 0))
    full = lambda a: pl.BlockSpec(a.shape, const)
    ws = (wb[:, :o1], wb[:, o1:o2], wb[:, o2:o3], wb[:, o3:o4], w_ba, w_bat)
    return pl.pallas_call(
        _in_kernel,
        grid=(tokens // TM,),
        in_specs=[tile(D_MODEL), pl.BlockSpec((1, 1, N_MOD * D_MODEL), mod_map), full(g)]
                 + [full(w) for w in ws],
        out_specs=[tile(S5_W), tile(S5_W), tile(3 * GDN_W), tile(GDN_W), tile(GATE_PAD),
                   pl.BlockSpec((N_GATE, TM), lambda i: (0, i))],
        out_shape=[jax.ShapeDtypeStruct((tokens, S5_W), F32),
                   jax.ShapeDtypeStruct((tokens, S5_W), F32),
                   jax.ShapeDtypeStruct((tokens, 3 * GDN_W), F32),
                   jax.ShapeDtypeStruct((tokens, GDN_W), F32),
                   jax.ShapeDtypeStruct((tokens, GATE_PAD), F32),
                   jax.ShapeDtypeStruct((N_GATE, tokens), F32)],
        compiler_params=_params("parallel"),
        name="in_proj",
    )(x, mods_l, g, *ws)


def _s5_operators(lam_re, lam_im, log_dt, b_re, b_im, c_re, c_im):
    lam_re = jnp.minimum(lam_re, -1e-4)
    dt = jnp.exp(log_dt)[..., None]
    x, th = lam_re * dt, lam_im * dt
    mag = jnp.exp(x)
    abar_re, abar_im = mag * jnp.cos(th), mag * jnp.sin(th)
    num_re, num_im = abar_re - 1.0, abar_im
    den = lam_re * lam_re + lam_im * lam_im
    f_re = (num_re * lam_re + num_im * lam_im) / den
    f_im = (num_im * lam_re - num_re * lam_im) / den
    bbar_re = f_re[..., None] * b_re - f_im[..., None] * b_im
    bbar_im = f_re[..., None] * b_im + f_im[..., None] * b_re

    def power(m):
        mm = jnp.asarray(m, F32)
        e = jnp.exp(x[..., None] * mm.reshape(-1)).reshape(x.shape + mm.shape)
        ang = (th[..., None] * mm.reshape(-1)).reshape(x.shape + mm.shape)
        return e * jnp.cos(ang), e * jnp.sin(ang)

    t = np.arange(S5_T + 1)
    pr, pi = power(t)
    ca_re = c_re[:, :, None] * pr.transpose(0, 1, 3, 2)[:, :, :, None, :] \
        - c_im[:, :, None] * pi.transpose(0, 1, 3, 2)[:, :, :, None, :]
    ca_im = c_re[:, :, None] * pi.transpose(0, 1, 3, 2)[:, :, :, None, :] \
        + c_im[:, :, None] * pr.transpose(0, 1, 3, 2)[:, :, :, None, :]
    k_lag = (jnp.einsum('dgtpn,dgnq->dgtpq', ca_re[:, :, :S5_T], bbar_re, precision=HIGHEST)
             - jnp.einsum('dgtpn,dgnq->dgtpq', ca_im[:, :, :S5_T], bbar_im, precision=HIGHEST))
    jj, ii = np.meshgrid(np.arange(S5_T), np.arange(S5_T), indexing='ij')
    lag_f = np.clip(ii - jj, 0, S5_T - 1)
    lag_b = np.clip(jj - ii, 0, S5_T - 1)
    kf = jnp.where((ii >= jj)[:, :, None, None], k_lag[0][:, lag_f], 0.0)
    kb = jnp.where((jj >= ii)[:, :, None, None], k_lag[1][:, lag_b], 0.0)
    k_mat = (kf + kb).transpose(0, 1, 4, 2, 3).reshape(S5_G, S5_TP, S5_TP)

    def in_state(d, expo):
        er, ei = pr[d][..., expo], pi[d][..., expo]
        re = er[..., None] * bbar_re[d][:, :, None, :] - ei[..., None] * bbar_im[d][:, :, None, :]
        im = er[..., None] * bbar_im[d][:, :, None, :] + ei[..., None] * bbar_re[d][:, :, None, :]
        m = jnp.concatenate([re, im], axis=1)
        return m.transpose(0, 2, 3, 1).reshape(S5_G, S5_TP, 2 * S5_N)

    b_f = in_state(0, np.arange(S5_T - 1, -1, -1))
    b_b = in_state(1, np.arange(S5_T))

    def state_out(d, expo):
        re = ca_re[d][:, expo]
        im = ca_im[d][:, expo]
        m = jnp.concatenate([re, -im], axis=-1)
        return m.transpose(0, 3, 1, 2).reshape(S5_G, 2 * S5_N, S5_TP)

    c_f = state_out(0, np.arange(1, S5_T + 1))
    c_b = state_out(1, np.arange(S5_T, 0, -1))

    steps = S5_T * (2 ** np.arange(8))
    qr, qi = power(steps)
    a1 = jnp.concatenate([qr, qr], axis=2)
    a2 = jnp.concatenate([-qi, qi], axis=2)
    a_pow = jnp.stack([a1, a2], axis=-1)
    a_pow = a_pow.transpose(1, 0, 3, 4, 2).reshape(S5_G, N_DIR * 16, 2 * S5_N)
    return k_mat.astype(BF16), b_f.astype(BF16), b_b.astype(BF16), c_f.astype(BF16), c_b.astype(BF16), a_pow


def _lane_swap(x):
    return pltpu.roll(x, S5_N, 1)


def _cmul(a1, a2, h):
    return a1 * h + a2 * _lane_swap(h)


def _block_scan(s, seg, a_pow, d, reverse):
    rows = s.shape[0]
    ridx = lax.broadcasted_iota(jnp.int32, s.shape, 0) % seg
    h = s
    k, step = 0, 1
    while step < seg:
        a1 = a_pow[d * 16 + 2 * k:d * 16 + 2 * k + 1]
        a2 = a_pow[d * 16 + 2 * k + 1:d * 16 + 2 * k + 2]
        if reverse:
            sh = jnp.where(ridx < seg - step, pltpu.roll(h, rows - step, 0), 0.0)
        else:
            sh = jnp.where(ridx >= step, pltpu.roll(h, step, 0), 0.0)
        h = h + _cmul(a1, a2, sh)
        k += 1
        step *= 2
    return h


def _s5_kernel(u_ref, k_ref, bf_ref, bb_ref, cf_ref, cb_ref, ap_ref, h0_ref,
               y_ref, fin_ref, hf_ref, hb_ref, *, ctx_rows, ctx_seg, lat_seg, n_ctx, n_lat):
    u = u_ref[0]
    a_pow = ap_ref[0]
    s_f = _dot(u, bf_ref[0])
    s_b = _dot(u, bb_ref[0])
    lat_rows = u.shape[0] - ctx_rows

    lrow = lax.broadcasted_iota(jnp.int32, (lat_rows, 2 * S5_N), 0)
    lmod = lrow % lat_seg
    lseq = lrow // lat_seg
    h0 = h0_ref[0]
    h0_f = jnp.zeros((lat_rows, 2 * S5_N), F32)
    h0_b = jnp.zeros((lat_rows, 2 * S5_N), F32)
    for s in range(n_lat):
        h0_f = jnp.where(lseq == s, h0[2 * s:2 * s + 1], h0_f)
        h0_b = jnp.where(lseq == s, h0[2 * s + 1:2 * s + 2], h0_b)
    first, last = lmod == 0, lmod == lat_seg - 1
    sl_f = s_f[ctx_rows:] + jnp.where(first, _cmul(a_pow[0:1], a_pow[1:2], h0_f), 0.0)
    sl_b = s_b[ctx_rows:] + jnp.where(last, _cmul(a_pow[16:17], a_pow[17:18], h0_b), 0.0)

    hc_f = _block_scan(s_f[:ctx_rows], ctx_seg, a_pow, 0, False)
    hc_b = _block_scan(s_b[:ctx_rows], ctx_seg, a_pow, 1, True)
    hl_f = _block_scan(sl_f, lat_seg, a_pow, 0, False)
    hl_b = _block_scan(sl_b, lat_seg, a_pow, 1, True)

    cmod = lax.broadcasted_iota(jnp.int32, (ctx_rows, 2 * S5_N), 0) % ctx_seg
    pc_f = jnp.where(cmod == 0, 0.0, pltpu.roll(hc_f, 1, 0))
    pc_b = jnp.where(cmod == ctx_seg - 1, 0.0, pltpu.roll(hc_b, ctx_rows - 1, 0))
    pl_f = jnp.where(first, h0_f, pltpu.roll(hl_f, 1, 0))
    pl_b = jnp.where(last, h0_b, pltpu.roll(hl_b, lat_rows - 1, 0))
    prev_f = jnp.concatenate([pc_f, pl_f], axis=0).astype(BF16)
    prev_b = jnp.concatenate([pc_b, pl_b], axis=0).astype(BF16)
    y_ref[0] = _dot(u, k_ref[0]) + _dot(prev_f, cf_ref[0]) + _dot(prev_b, cb_ref[0])

    hf_ref[...] = hc_f
    hb_ref[...] = hc_b
    fin_ref[0, 0] = hf_ref[pl.ds(ctx_seg - 1, n_ctx, stride=ctx_seg), :]
    fin_ref[0, 1] = hb_ref[pl.ds(0, n_ctx, stride=ctx_seg), :]


def _s5_mixer(u_g, ops, h0, n_ctx, ctx_len, n_lat, lat_len):
    k_mat, b_f, b_b, c_f, c_b, a_pow = ops
    rows = u_g.shape[1]
    ctx_seg, lat_seg = ctx_len // S5_T, lat_len // S5_T
    ctx_rows = n_ctx * ctx_seg
    per_g = lambda a: pl.BlockSpec((1,) + a.shape[1:], lambda g: (g,) + (0,) * (a.ndim - 1))
    kern = functools.partial(_s5_kernel, ctx_rows=ctx_rows, ctx_seg=ctx_seg, lat_seg=lat_seg,
                             n_ctx=n_ctx, n_lat=n_lat)
    return pl.pallas_call(
        kern,
        grid=(S5_G,),
        in_specs=[per_g(a) for a in (u_g, k_mat, b_f, b_b, c_f, c_b, a_pow, h0)],
        out_specs=[pl.BlockSpec((1, rows, S5_TP), lambda g: (g, 0, 0)),
                   pl.BlockSpec((1, N_DIR, n_ctx, 2 * S5_N), lambda g: (g, 0, 0, 0))],
        out_shape=[jax.ShapeDtypeStruct((S5_G, rows, S5_TP), F32),
                   jax.ShapeDtypeStruct((S5_G, N_DIR, n_ctx, 2 * S5_N), F32)],
        scratch_shapes=[pltpu.VMEM((ctx_rows, 2 * S5_N), F32)] * N_DIR,
        compiler_params=_params("parallel"),
        name="s5_mixer",
    )(u_g, k_mat, b_f, b_b, c_f, c_b, a_pow, h0)


def _gdn_pre_kernel(x_ref, hp_ref, hn_ref, cw_ref, ba_ref, bat_ref, al_ref, dl_ref, ar_ref, dr_ref,
                    q_ref, k_ref, v_ref, col_ref, row_ref):
    x = x_ref[...]
    rid = lax.broadcasted_iota(jnp.int32, x.shape, 0)
    x_prev = jnp.where(rid == 0, hp_ref[0], pltpu.roll(x, 1, 0))
    x_next = jnp.where(rid == TM - 1, hn_ref[0], pltpu.roll(x, TM - 1, 0))
    cw = cw_ref[...]
    y = _silu(x_prev * cw[0:1] + x * cw[1:2] + x_next * cw[2:3])
    for h in range(GDN_H):
        lo, hi = h * GDN_DK, (h + 1) * GDN_DK
        qh = y[:, lo:hi]
        kh = y[:, GDN_W + lo:GDN_W + hi]
        q_ref[:, lo:hi] = qh * (lax.rsqrt(jnp.sum(qh * qh, axis=-1, keepdims=True) + EPS)
                                * (GDN_DK ** -0.5))
        k_ref[:, lo:hi] = kh * lax.rsqrt(jnp.sum(kh * kh, axis=-1, keepdims=True) + EPS)
    v_ref[...] = y[:, 2 * GDN_W:]

    half = N_GATE // 2
    ba = ba_ref[...]
    lane = lax.broadcasted_iota(jnp.int32, ba.shape, 1)
    gate = jnp.where(lane < half, _sigmoid(ba), -jnp.exp(al_ref[...]) * _softplus(ba + dl_ref[...]))
    ri = lax.broadcasted_iota(jnp.int32, (TM, TM), 0)
    ci = lax.broadcasted_iota(jnp.int32, (TM, TM), 1)
    same = (ri // CHUNK) == (ci // CHUNK)
    lower = jnp.where(same & (ci <= ri), 1.0, 0.0)
    upper = jnp.where(same & (ci >= ri), 1.0, 0.0)
    cum_f = jnp.dot(lower, gate, precision=HIGHEST, preferred_element_type=F32)
    cum_b = jnp.dot(upper, gate, precision=HIGHEST, preferred_element_type=F32)
    fwd_lane = (lane >= half) & (lane < half + GDN_H)
    bwd_lane = (lane >= half + GDN_H) & (lane < N_GATE)
    col = jnp.where(fwd_lane, cum_f, jnp.where(bwd_lane, cum_b, gate))
    for c in range(TM // CHUNK):
        col_ref[c] = col[c * CHUNK:(c + 1) * CHUNK]
    bat = bat_ref[...]
    g_row = -jnp.exp(ar_ref[...]) * _softplus(bat + dr_ref[...])
    row_f = jnp.dot(g_row, upper, precision=HIGHEST, preferred_element_type=F32)
    row_b = jnp.dot(g_row, lower, precision=HIGHEST, preferred_element_type=F32)
    srow = lax.broadcasted_iota(jnp.int32, bat.shape, 0)
    rowv = jnp.where(srow < half + GDN_H, row_f, row_b)
    for c in range(TM // CHUNK):
        row_ref[c] = rowv[:, c * CHUNK:(c + 1) * CHUNK]


def _gdn_pre(qkv, halo_prev, halo_next, conv_w, ba, bat, a_log, dt_bias):
    tokens = qkv.shape[0]
    n_tiles = tokens // TM
    cpt = TM // CHUNK
    half = N_GATE // 2
    al = a_log.reshape(-1)
    db = dt_bias.reshape(-1)
    al_l = jnp.zeros((1, GATE_PAD), F32).at[0, half:N_GATE].set(al)
    db_l = jnp.zeros((1, GATE_PAD), F32).at[0, half:N_GATE].set(db)
    al_r = jnp.zeros((N_GATE, 1), F32).at[half:, 0].set(al)
    db_r = jnp.zeros((N_GATE, 1), F32).at[half:, 0].set(db)
    const = lambda i: (0, 0)
    tile = lambda n: pl.BlockSpec((TM, n), lambda i: (i, 0))
    halo = pl.BlockSpec((1, 1, 3 * GDN_W), lambda i: (i, 0, 0))
    return pl.pallas_call(
        _gdn_pre_kernel,
        grid=(n_tiles,),
        in_specs=[tile(3 * GDN_W), halo, halo, pl.BlockSpec(conv_w.shape, const),
                  tile(GATE_PAD), pl.BlockSpec((N_GATE, TM), lambda i: (0, i)),
                  pl.BlockSpec((1, GATE_PAD), const), pl.BlockSpec((1, GATE_PAD), const),
                  pl.BlockSpec((N_GATE, 1), const), pl.BlockSpec((N_GATE, 1), const)],
        out_specs=[tile(GDN_W), tile(GDN_W), tile(GDN_W),
                   pl.BlockSpec((cpt, CHUNK, GATE_PAD), lambda i: (i, 0, 0)),
                   pl.BlockSpec((cpt, N_GATE, CHUNK), lambda i: (i, 0, 0))],
        out_shape=[jax.ShapeDtypeStruct((tokens, GDN_W), F32)] * 3
                  + [jax.ShapeDtypeStruct((tokens // CHUNK, CHUNK, GATE_PAD), F32),
                     jax.ShapeDtypeStruct((tokens // CHUNK, N_GATE, CHUNK), F32)],
        compiler_params=_params("parallel"),
        name="gdn_pre",
    )(qkv, halo_prev, halo_next, conv_w, ba, bat, al_l, db_l, al_r, db_r)


def _gdn_unit(qh, kh, vh, beta, gc, gr, s, upper):
    ii = lax.broadcasted_iota(jnp.int32, (CHUNK, CHUNK), 0)
    jj = lax.broadcasted_iota(jnp.int32, (CHUNK, CHUNK), 1)
    incl = (jj >= ii) if upper else (jj <= ii)
    strict = (jj > ii) if upper else (jj < ii)
    decay = jnp.where(incl, jnp.exp(jnp.where(incl, gc - gr, 0.0)), 0.0)
    kb = kh * beta
    vb = vh * beta
    k16 = kh.astype(BF16)
    m = jnp.where(strict, _dot_nt(kb.astype(BF16), k16) * decay, 0.0)
    a_intra = _dot_nt(qh.astype(BF16), k16) * decay
    same2 = (ii // 2) == (jj // 2)
    acc = jnp.where(same2, -m, 0.0)
    b = 2
    while b < CHUNK:
        pair = ((ii // (2 * b)) == (jj // (2 * b))) & ((ii // b) != (jj // b))
        c = jnp.where(pair, m, 0.0)
        a16 = acc.astype(BF16)
        y = c + _dot(a16, c.astype(BF16))
        acc = acc - y - _dot(y.astype(BF16), a16)
        b *= 2
    egc = jnp.exp(gc)
    rhs = jnp.concatenate([vb, kb * egc], axis=1)
    sol = rhs + _dot(acc.astype(BF16), rhs.astype(BF16))
    u_c, w_c = sol[:, :GDN_DK], sol[:, GDN_DK:]
    lhs = jnp.concatenate([w_c, qh * egc], axis=0).astype(BF16)
    ps = _dot(lhs, s.astype(BF16))
    v_new = u_c - ps[:CHUNK]
    v16 = v_new.astype(BF16)
    o = ps[CHUNK:] + _dot(a_intra.astype(BF16), v16)
    g_last = gc[0:1] if upper else gc[CHUNK - 1:CHUNK]
    kd = (kh * jnp.exp(g_last - gc)).astype(BF16)
    s_new = s * jnp.exp(g_last) + _dot_tn(kd, v16)
    return o, s_new


def _gdn_scan_kernel(*refs, has_s0, has_fin):
    qf, kf, vf, cf, rf, qb, kb, vb, cb, rb = refs[:10]
    pos = 10
    s0_ref = None
    if has_s0:
        s0_ref = refs[pos]
        pos += 1
    of_ref, ob_ref = refs[pos], refs[pos + 1]
    pos += 2
    fin_ref = None
    if has_fin:
        fin_ref = refs[pos]
        pos += 1
    s_ref = refs[pos]
    j = pl.program_id(1)

    @pl.when(j == 0)
    def _():
        if has_s0:
            s_ref[...] = s0_ref[0]
        else:
            s_ref[...] = jnp.zeros(s_ref.shape, F32)

    half = N_GATE // 2
    for d, (q_r, k_r, v_r, c_r, r_r, o_r) in enumerate(((qf, kf, vf, cf, rf, of_ref),
                                                         (qb, kb, vb, cb, rb, ob_ref))):
        col = c_r[0]
        row = r_r[0]
        for h in range(GDN_H):
            lo, hi = h * GDN_DK, (h + 1) * GDN_DK
            lane = d * GDN_H + h
            o, s_new = _gdn_unit(q_r[:, lo:hi], k_r[:, lo:hi], v_r[:, lo:hi],
                                 col[:, lane:lane + 1], col[:, half + lane:half + lane + 1],
                                 row[half + lane:half + lane + 1, :], s_ref[d, h], d == 1)
            o_r[:, lo:hi] = o
            s_ref[d, h] = s_new

    if has_fin:
        @pl.when(j == pl.num_programs(1) - 1)
        def _():
            fin_ref[0] = s_ref[...]


def _gdn_scan(q, k, v, col, row, s0, n_seq, length, base_tok, want_fin):
    n_c = length // CHUNK
    base = base_tok // CHUNK
    fwd2 = lambda s, j: (base + s * n_c + j, 0)
    bwd2 = lambda s, j: (base + s * n_c + (n_c - 1 - j), 0)
    fwd3 = lambda s, j: (base + s * n_c + j, 0, 0)
    bwd3 = lambda s, j: (base + s * n_c + (n_c - 1 - j), 0, 0)
    tok = lambda m: pl.BlockSpec((CHUNK, GDN_W), m)
    in_specs, args = [], []
    for m2, m3 in ((fwd2, fwd3), (bwd2, bwd3)):
        in_specs += [tok(m2), tok(m2), tok(m2),
                     pl.BlockSpec((1, CHUNK, GATE_PAD), m3), pl.BlockSpec((1, N_GATE, CHUNK), m3)]
        args += [q, k, v, col, row]
    state_spec = pl.BlockSpec((1, N_DIR, GDN_H, GDN_DK, GDN_DK), lambda s, j: (s, 0, 0, 0, 0))
    if s0 is not None:
        in_specs.append(state_spec)
        args.append(s0)
    out_specs = [tok(lambda s, j: (s * n_c + j, 0)), tok(lambda s, j: (s * n_c + (n_c - 1 - j), 0))]
    out_shape = [jax.ShapeDtypeStruct((n_seq * length, GDN_W), F32)] * 2
    if want_fin:
        out_specs.append(state_spec)
        out_shape.append(jax.ShapeDtypeStruct((n_seq, N_DIR, GDN_H, GDN_DK, GDN_DK), F32))
    return pl.pallas_call(
        functools.partial(_gdn_scan_kernel, has_s0=s0 is not None, has_fin=want_fin),
        grid=(n_seq, n_c),
        in_specs=in_specs,
        out_specs=out_specs,
        out_shape=out_shape,
        scratch_shapes=[pltpu.VMEM((N_DIR, GDN_H, GDN_DK, GDN_DK), F32)],
        compiler_params=_params("parallel", "arbitrary"),
        name="gdn_scan",
    )(*args)


def _out_kernel(x_ref, m_ref, y5_ref, u_ref, z_ref, dsk_ref, of_ref, ob_ref, zg_ref, gn_ref, w_ref,
                o_ref):
    m = m_ref[0]
    s5 = _gelu_tanh(y5_ref[...] + u_ref[...] * dsk_ref[...]) * _sigmoid(z_ref[...])
    acc = _dot(s5.astype(BF16), w_ref[0:S5_W, :])
    o = of_ref[...] + ob_ref[...]
    zg = zg_ref[...]
    gn = gn_ref[...]
    for h in range(GDN_H):
        lo, hi = h * GDN_DK, (h + 1) * GDN_DK
        oh = o[:, lo:hi]
        nh = oh * lax.rsqrt(jnp.mean(oh * oh, axis=-1, keepdims=True) + EPS) * gn
        gh = (nh * _silu(zg[:, lo:hi])).astype(BF16)
        acc = acc + _dot(gh, w_ref[S5_W + lo:S5_W + hi, :])
    o_ref[...] = x_ref[...] + m[:, 2 * D_MODEL:3 * D_MODEL] * acc


def _out_proj(x, mods_l, y5, u, z, d_skip, o_f, o_b, zg, gdn_norm, w_out, mod_map):
    tokens = x.shape[0]
    const = lambda i: (0, 0)
    tile = lambda n: pl.BlockSpec((TM, n), lambda i: (i, 0))
    return pl.pallas_call(
        _out_kernel,
        grid=(tokens // TM,),
        in_specs=[tile(D_MODEL), pl.BlockSpec((1, 1, N_MOD * D_MODEL), mod_map),
                  tile(S5_W), tile(S5_W), tile(S5_W), pl.BlockSpec((1, S5_W), const),
                  tile(GDN_W), tile(GDN_W), tile(GDN_W), pl.BlockSpec((1, GDN_DK), const),
                  pl.BlockSpec((D_MODEL, D_MODEL), const)],
        out_specs=tile(D_MODEL),
        out_shape=jax.ShapeDtypeStruct((tokens, D_MODEL), F32),
        compiler_params=_params("parallel"),
        name="out_proj",
    )(x, mods_l, y5, u, z, d_skip, o_f, o_b, zg, gdn_norm, w_out.astype(BF16))


def _mlp_kernel(x_ref, m_ref, g_ref, w1_ref, w2_ref, o_ref):
    m = m_ref[0]
    x = x_ref[...]
    h = _norm_mod(x, g_ref[...], m[:, 3 * D_MODEL:4 * D_MODEL], m[:, 4 * D_MODEL:5 * D_MODEL])
    ff = jnp.maximum(_dot(h.astype(BF16), w1_ref[...]), 0.0)
    ff = (ff * ff).astype(BF16)
    o_ref[...] = x + m[:, 5 * D_MODEL:6 * D_MODEL] * _dot(ff, w2_ref[...])


def _mlp(x, mods_l, g, w1, w2, mod_map):
    tokens = x.shape[0]
    const = lambda i: (0, 0)
    tile = pl.BlockSpec((TM, D_MODEL), lambda i: (i, 0))
    once = pl.Buffered(1)
    return pl.pallas_call(
        _mlp_kernel,
        grid=(tokens // TM,),
        in_specs=[tile, pl.BlockSpec((1, 1, N_MOD * D_MODEL), mod_map),
                  pl.BlockSpec((1, D_MODEL), const),
                  pl.BlockSpec((D_MODEL, D_FF), const, pipeline_mode=once),
                  pl.BlockSpec((D_FF, D_MODEL), const, pipeline_mode=once)],
        out_specs=tile,
        out_shape=jax.ShapeDtypeStruct((tokens, D_MODEL), F32),
        compiler_params=_params("parallel"),
        name="mlp",
    )(x, mods_l, g, w1.astype(BF16), w2.astype(BF16))


def _final_kernel(x_ref, g_ref, o_ref):
    x = x_ref[...]
    o_ref[...] = x * lax.rsqrt(jnp.mean(x * x, axis=-1, keepdims=True) + EPS) * g_ref[...]


def _final_norm(x, g, first_tile, n_tiles):
    return pl.pallas_call(
        _final_kernel,
        grid=(n_tiles,),
        in_specs=[pl.BlockSpec((TM, D_MODEL), lambda i: (first_tile + i, 0)),
                  pl.BlockSpec((1, D_MODEL), lambda i: (0, 0))],
        out_specs=pl.BlockSpec((TM, D_MODEL), lambda i: (i, 0)),
        out_shape=jax.ShapeDtypeStruct((n_tiles * TM, D_MODEL), F32),
        compiler_params=_params("parallel"),
        name="final_norm",
    )(x, g)


def kernel(x_prompt, x_sample, state_s5_re, state_s5_im, state_gdn, c, c_ctx, norm_mix, norm_mlp, w_ada, b_ada, w_in, conv_qkv, s5_lambda_re, s5_lambda_im, s5_log_dt, s5_b_re, s5_b_im, s5_c_re, s5_c_im, s5_d, gdn_a_log, gdn_dt_bias, gdn_norm, w_out, w_mlp_in, w_mlp_out, norm_final):
    n_ctx, ctx_len, _ = x_prompt.shape
    n_lat, lat_len, _ = x_sample.shape
    depth = w_in.shape[0]
    assert ctx_len % TM == 0 and lat_len % TM == 0 and lat_len % (8 * GRID_W) == 0
    assert n_lat + 1 <= 8
    ctx_tok, lat_tok = n_ctx * ctx_len, n_lat * lat_len
    tokens = ctx_tok + lat_tok
    n_ctx_tiles, tiles_per_lat = ctx_tok // TM, lat_len // TM
    mod_map = _mod_row_map(n_ctx_tiles, tiles_per_lat)

    cond = jnp.zeros((8, D_MODEL), F32).at[0].set(c_ctx).at[1:1 + n_lat].set(c)
    mods = _modulation(cond, w_ada, b_ada)
    x = jnp.concatenate([x_prompt.reshape(ctx_tok, D_MODEL), _embed(x_sample)], axis=0)

    tile_ids = np.arange(tokens // TM)
    lat_tile = np.maximum(tile_ids - n_ctx_tiles, 0) % tiles_per_lat
    is_lat = tile_ids >= n_ctx_tiles
    has_prev = (is_lat & (lat_tile != 0)) | (~is_lat & ((tile_ids % (ctx_len // TM)) != 0))
    has_next = (is_lat & (lat_tile != tiles_per_lat - 1)) \
        | (~is_lat & ((tile_ids % (ctx_len // TM)) != ctx_len // TM - 1))

    fin_re, fin_im, fin_gdn = [], [], []
    for l in range(depth):
        mods_l = mods[l].reshape(8, 1, N_MOD * D_MODEL)
        u, z, qkv, zg, ba, bat = _in_proj(x, mods_l, norm_mix[l].reshape(1, D_MODEL), w_in[l], mod_map)

        ops = _s5_operators(s5_lambda_re[l], s5_lambda_im[l], s5_log_dt[l], s5_b_re[l], s5_b_im[l],
                            s5_c_re[l], s5_c_im[l])
        u_g = u.astype(BF16).reshape(tokens // S5_T, S5_T, S5_G, S5_P).transpose(2, 0, 1, 3)
        u_g = u_g.reshape(S5_G, tokens // S5_T, S5_TP)
        h0 = jnp.concatenate([state_s5_re[:, l], state_s5_im[:, l]], axis=-1)
        h0 = h0.transpose(2, 0, 1, 3).reshape(S5_G, n_lat * N_DIR, 2 * S5_N)
        y_g, s5_fin = _s5_mixer(u_g, ops, h0, n_ctx, ctx_len, n_lat, lat_len)
        y5 = y_g.reshape(S5_G, tokens // S5_T, S5_T, S5_P).transpose(1, 2, 0, 3).reshape(tokens, S5_W)
        s5_fin = s5_fin.transpose(2, 1, 0, 3)
        fin_re.append(s5_fin[..., :S5_N])
        fin_im.append(s5_fin[..., S5_N:])

        qkv_t = qkv.reshape(tokens // TM, TM, 3 * GDN_W)
        zero_row = jnp.zeros((1, 3 * GDN_W), F32)
        prev_rows = jnp.concatenate([zero_row, qkv_t[:-1, TM - 1]], axis=0)
        next_rows = jnp.concatenate([qkv_t[1:, 0], zero_row], axis=0)
        halo_prev = jnp.where(has_prev[:, None], prev_rows, 0.0)[:, None, :]
        halo_next = jnp.where(has_next[:, None], next_rows, 0.0)[:, None, :]
        q, k, v, col, row = _gdn_pre(qkv, halo_prev, halo_next, conv_qkv[l], ba, bat,
                                     gdn_a_log[l], gdn_dt_bias[l])
        of_c, ob_c, gdn_fin = _gdn_scan(q, k, v, col, row, None, n_ctx, ctx_len, 0, True)
        of_l, ob_l = _gdn_scan(q, k, v, col, row, state_gdn[:, l], n_lat, lat_len, ctx_tok, False)
        fin_gdn.append(gdn_fin)
        o_f = jnp.concatenate([of_c, of_l], axis=0)
        o_b = jnp.concatenate([ob_c, ob_l], axis=0)

        x = _out_proj(x, mods_l, y5, u, z, s5_d[l].reshape(1, S5_W), o_f, o_b, zg,
                      gdn_norm[l].reshape(1, GDN_DK), w_out[l], mod_map)
        x = _mlp(x, mods_l, norm_mlp[l].reshape(1, D_MODEL), w_mlp_in[l], w_mlp_out[l], mod_map)

    g_fin = norm_final.reshape(1, D_MODEL)
    y_prompt = _final_norm(x, g_fin, 0, n_ctx_tiles).reshape(n_ctx, ctx_len, D_MODEL)
    y_sample = _final_norm(x, g_fin, n_ctx_tiles, lat_tok // TM).reshape(n_lat, lat_len, D_MODEL)
    return (y_prompt, y_sample, jnp.stack(fin_re, axis=1), jnp.stack(fin_im, axis=1),
            jnp.stack(fin_gdn, axis=1))
```

```python
import functools
import math

import numpy as np
import jax
import jax.numpy as jnp
from jax import lax
from jax.experimental import pallas as pl
from jax.experimental.pallas import tpu as pltpu

F32 = jnp.float32
BF16 = jnp.bfloat16
HIGHEST = lax.Precision.HIGHEST

D_MODEL = 1024
S5_W = 512
S5_P = 16
S5_G = 32
S5_N = 64
GDN_W = 512
GDN_DK = 128
GDN_H = 4
CHUNK = 64
N_DIR = 2
N_MOD = 6
D_FF = 4 * D_MODEL
GRID_W = 64
POS_BASE = 10000.0
EPS = 1e-6
S5_T = 16
S5_TP = S5_T * S5_P
N_GATE = 2 * N_DIR * GDN_H
GATE_PAD = 128
TM = 256
GT = 128
SB = 256
VMEM_LIMIT = 56 * 1024 * 1024


def _dot(a, b):
    return jnp.dot(a, b, preferred_element_type=F32)


def _dot_nt(a, b):
    return lax.dot_general(a, b, (((1,), (1,)), ((), ())), preferred_element_type=F32)


def _dot_tn(a, b):
    return lax.dot_general(a, b, (((0,), (0,)), ((), ())), preferred_element_type=F32)


def _sigmoid(x):
    return 1.0 / (1.0 + jnp.exp(-x))


def _silu(x):
    return x * _sigmoid(x)


def _softplus(x):
    return jnp.maximum(x, 0.0) + jnp.log1p(jnp.exp(-jnp.abs(x)))


def _gelu_tanh(x):
    c = math.sqrt(2.0 / math.pi)
    return 0.5 * x * (1.0 + jnp.tanh(c * (x + 0.044715 * (x * x * x))))


def _params(*sem):
    return pltpu.CompilerParams(dimension_semantics=sem, vmem_limit_bytes=VMEM_LIMIT)


def _mod_kernel(c_ref, w_ref, b_ref, o_ref):
    sc = _silu(c_ref[...])
    o_ref[0] = jnp.dot(sc, w_ref[0], precision=HIGHEST, preferred_element_type=F32) + b_ref[0]


def _modulation(cond, w_ada, b_ada):
    depth = w_ada.shape[0]
    tn = 1536
    n_w = N_MOD * D_MODEL
    return pl.pallas_call(
        _mod_kernel,
        grid=(depth, n_w // tn),
        in_specs=[pl.BlockSpec((8, D_MODEL), lambda l, j: (0, 0)),
                  pl.BlockSpec((1, D_MODEL, tn), lambda l, j: (l, 0, j)),
                  pl.BlockSpec((1, 1, tn), lambda l, j: (l, 0, j))],
        out_specs=pl.BlockSpec((1, 8, tn), lambda l, j: (l, 0, j)),
        out_shape=jax.ShapeDtypeStruct((depth, 8, n_w), F32),
        compiler_params=_params("parallel", "parallel"),
        name="modulation",
    )(cond, w_ada, b_ada.reshape(depth, 1, n_w))


def _embed_kernel(x_ref, tr_ref, tc_ref, o_ref, *, rows_per_tile, tiles_per_seq):
    r0 = (pl.program_id(0) % tiles_per_seq) * rows_per_tile
    half = D_MODEL // 2
    tc = tc_ref[...]
    for k in range(rows_per_tile):
        lo, hi = k * GRID_W, (k + 1) * GRID_W
        o_ref[lo:hi, 0:half] = x_ref[lo:hi, 0:half] + tr_ref[pl.ds(r0 + k, 1), :]
        o_ref[lo:hi, half:D_MODEL] = x_ref[lo:hi, half:D_MODEL] + tc


def _embed(x_sample):
    n_seq, length, _ = x_sample.shape
    rows = length // GRID_W
    quarter = D_MODEL // 4
    omega = 1.0 / (POS_BASE ** (jnp.arange(quarter, dtype=F32) / quarter))
    ang_r = jnp.arange(rows, dtype=F32)[:, None] * omega
    ang_c = jnp.arange(GRID_W, dtype=F32)[:, None] * omega
    tab_r = jnp.concatenate([jnp.sin(ang_r), jnp.cos(ang_r)], axis=-1)
    tab_c = jnp.concatenate([jnp.sin(ang_c), jnp.cos(ang_c)], axis=-1)
    rows_per_tile = 8
    tile = rows_per_tile * GRID_W
    tiles_per_seq = length // tile
    x2 = x_sample.reshape(n_seq * length, D_MODEL)
    return pl.pallas_call(
        functools.partial(_embed_kernel, rows_per_tile=rows_per_tile, tiles_per_seq=tiles_per_seq),
        grid=(n_seq * tiles_per_seq,),
        in_specs=[pl.BlockSpec((tile, D_MODEL), lambda i: (i, 0)),
                  pl.BlockSpec((rows, D_MODEL // 2), lambda i: (0, 0)),
                  pl.BlockSpec((GRID_W, D_MODEL // 2), lambda i: (0, 0))],
        out_specs=pl.BlockSpec((tile, D_MODEL), lambda i: (i, 0)),
        out_shape=jax.ShapeDtypeStruct(x2.shape, F32),
        compiler_params=_params("parallel"),
        name="pos_embed",
    )(x2, tab_r, tab_c)


def _norm_mod(x, g, shift, scale):
    ms = jnp.mean(x * x, axis=-1, keepdims=True)
    h = x * lax.rsqrt(ms + EPS) * g
    return h * (1.0 + scale) + shift


def _in_kernel(x_ref, m_ref, g_ref, wu_ref, wz_ref, wqkv_ref, wzg_ref, wba_ref, wbat_ref,
               u_ref, z_ref, qkv_ref, zg_ref, ba_ref, bat_ref):
    m = m_ref[0]
    h = _norm_mod(x_ref[...], g_ref[...], m[:, 0:D_MODEL], m[:, D_MODEL:2 * D_MODEL]).astype(BF16)
    u_ref[...] = _dot(h, wu_ref[...])
    z_ref[...] = _dot(h, wz_ref[...])
    qkv_ref[...] = _dot(h, wqkv_ref[...])
    zg_ref[...] = _dot(h, wzg_ref[...])
    ba_ref[...] = _dot(h, wba_ref[...])
    bat_ref[...] = _dot_nt(wbat_ref[...], h)


def _mod_row_map(n_ctx_tiles, tiles_per_lat):
    def index(i):
        return (jnp.where(i < n_ctx_tiles, 0, 1 + (i - n_ctx_tiles) // tiles_per_lat), 0, 0)
    return index


def _in_proj(x, mods_l, g, w_in, mod_map):
    tokens = x.shape[0]
    o1, o2, o3, o4 = S5_W, 2 * S5_W, 2 * S5_W + 3 * GDN_W, 2 * S5_W + 4 * GDN_W
    wb = w_in.astype(BF16)
    w_ba = jnp.pad(wb[:, o4:], ((0, 0), (0, GATE_PAD - N_GATE)))
    w_bat = wb[:, o4:].T
    const = lambda i: (0, 0)
    tile = lambda n: pl.BlockSpec((TM, n), lambda i: (i, 0))
    full = lambda a: pl.BlockSpec(a.shape, const)
    ws = (wb[:, :o1], wb[:, o1:o2], wb[:, o2:o3], wb[:, o3:o4], w_ba, w_bat)
    return pl.pallas_call(
        _in_kernel,
        grid=(tokens // TM,),
        in_specs=[tile(D_MODEL), pl.BlockSpec((1, 1, N_MOD * D_MODEL), mod_map), full(g)]
                 + [full(w) for w in ws],
        out_specs=[tile(S5_W), tile(S5_W), tile(3 * GDN_W), tile(GDN_W), tile(GATE_PAD),
                   pl.BlockSpec((N_GATE, TM), lambda i: (0, i))],
        out_shape=[jax.ShapeDtypeStruct((tokens, S5_W), F32),
                   jax.ShapeDtypeStruct((tokens, S5_W), F32),
                   jax.ShapeDtypeStruct((tokens, 3 * GDN_W), F32),
                   jax.ShapeDtypeStruct((tokens, GDN_W), F32),
                   jax.ShapeDtypeStruct((tokens, GATE_PAD), F32),
                   jax.ShapeDtypeStruct((N_GATE, tokens), F32)],
        compiler_params=_params("parallel"),
        name="in_proj",
    )(x, mods_l, g, *ws)


def _s5_operators(lam_re, lam_im, log_dt, b_re, b_im, c_re, c_im):
    lam_re = jnp.minimum(lam_re, -1e-4)
    dt = jnp.exp(log_dt)[..., None]
    x, th = lam_re * dt, lam_im * dt
    mag = jnp.exp(x)
    abar_re, abar_im = mag * jnp.cos(th), mag * jnp.sin(th)
    num_re, num_im = abar_re - 1.0, abar_im
    den = lam_re * lam_re + lam_im * lam_im
    f_re = (num_re * lam_re + num_im * lam_im) / den
    f_im = (num_im * lam_re - num_re * lam_im) / den
    bbar_re = f_re[..., None] * b_re - f_im[..., None] * b_im
    bbar_im = f_re[..., None] * b_im + f_im[..., None] * b_re

    def power(m):
        mm = jnp.asarray(m, F32)
        e = jnp.exp(x[..., None] * mm.reshape(-1)).reshape(x.shape + mm.shape)
        ang = (th[..., None] * mm.reshape(-1)).reshape(x.shape + mm.shape)
        return e * jnp.cos(ang), e * jnp.sin(ang)

    t = np.arange(S5_T + 1)
    pr, pi = power(t)
    ca_re = c_re[:, :, None] * pr.transpose(0, 1, 3, 2)[:, :, :, None, :] \
        - c_im[:, :, None] * pi.transpose(0, 1, 3, 2)[:, :, :, None, :]
    ca_im = c_re[:, :, None] * pi.transpose(0, 1, 3, 2)[:, :, :, None, :] \
        + c_im[:, :, None] * pr.transpose(0, 1, 3, 2)[:, :, :, None, :]
    k_lag = (jnp.einsum('dgtpn,dgnq->dgtpq', ca_re[:, :, :S5_T], bbar_re, precision=HIGHEST)
             - jnp.einsum('dgtpn,dgnq->dgtpq', ca_im[:, :, :S5_T], bbar_im, precision=HIGHEST))
    jj, ii = np.meshgrid(np.arange(S5_T), np.arange(S5_T), indexing='ij')
    lag_f = np.clip(ii - jj, 0, S5_T - 1)
    lag_b = np.clip(jj - ii, 0, S5_T - 1)
    kf = jnp.where((ii >= jj)[:, :, None, None], k_lag[0][:, lag_f], 0.0)
    kb = jnp.where((jj >= ii)[:, :, None, None], k_lag[1][:, lag_b], 0.0)
    k_mat = (kf + kb).transpose(0, 1, 4, 2, 3).reshape(S5_G, S5_TP, S5_TP)

    def in_state(d, expo):
        er, ei = pr[d][..., expo], pi[d][..., expo]
        re = er[..., None] * bbar_re[d][:, :, None, :] - ei[..., None] * bbar_im[d][:, :, None, :]
        im = er[..., None] * bbar_im[d][:, :, None, :] + ei[..., None] * bbar_re[d][:, :, None, :]
        m = jnp.concatenate([re, im], axis=1)
        return m.transpose(0, 2, 3, 1).reshape(S5_G, S5_TP, 2 * S5_N)

    b_f = in_state(0, np.arange(S5_T - 1, -1, -1))
    b_b = in_state(1, np.arange(S5_T))

    def state_out(d, expo):
        re = ca_re[d][:, expo]
        im = ca_im[d][:, expo]
        m = jnp.concatenate([re, -im], axis=-1)
        return m.transpose(0, 3, 1, 2).reshape(S5_G, 2 * S5_N, S5_TP)

    c_f = state_out(0, np.arange(1, S5_T + 1))
    c_b = state_out(1, np.arange(S5_T, 0, -1))

    steps = S5_T * (2 ** np.arange(8))
    qr, qi = power(steps)
    a1 = jnp.concatenate([qr, qr], axis=2)
    a2 = jnp.concatenate([-qi, qi], axis=2)
    a_pow = jnp.stack([a1, a2], axis=-1)
    a_pow = a_pow.transpose(1, 0, 3, 4, 2).reshape(S5_G, N_DIR * 16, 2 * S5_N)
    return k_mat.astype(BF16), b_f.astype(BF16), b_b.astype(BF16), c_f.astype(BF16), c_b.astype(BF16), a_pow


def _lane_swap(x):
    return pltpu.roll(x, S5_N, 1)


def _cmul(a1, a2, h):
    return a1 * h + a2 * _lane_swap(h)


def _block_scan(s, seg, a_pow, d, reverse):
    rows = s.shape[0]
    ridx = lax.broadcasted_iota(jnp.int32, s.shape, 0) % seg
    h = s
    k, step = 0, 1
    while step < seg:
        a1 = a_pow[d * 16 + 2 * k:d * 16 + 2 * k + 1]
        a2 = a_pow[d * 16 + 2 * k + 1:d * 16 + 2 * k + 2]
        if reverse:
            sh = jnp.where(ridx < seg - step, pltpu.roll(h, rows - step, 0), 0.0)
        else:
            sh = jnp.where(ridx >= step, pltpu.roll(h, step, 0), 0.0)
        h = h + _cmul(a1, a2, sh)
        k += 1
        step *= 2
    return h


def _s5_kernel(u_ref, k_ref, bf_ref, bb_ref, cf_ref, cb_ref, ap_ref, h0_ref,
               y_ref, fin_ref, hf_ref, hb_ref, *, ctx_rows, ctx_seg, lat_seg, n_ctx, n_lat):
    u = u_ref[0]
    a_pow = ap_ref[0]
    s_f = _dot(u, bf_ref[0])
    s_b = _dot(u, bb_ref[0])
    lat_rows = u.shape[0] - ctx_rows

    lrow = lax.broadcasted_iota(jnp.int32, (lat_rows, 2 * S5_N), 0)
    lmod = lrow % lat_seg
    lseq = lrow // lat_seg
    h0 = h0_ref[0]
    h0_f = jnp.zeros((lat_rows, 2 * S5_N), F32)
    h0_b = jnp.zeros((lat_rows, 2 * S5_N), F32)
    for s in range(n_lat):
        h0_f = jnp.where(lseq == s, h0[2 * s:2 * s + 1], h0_f)
        h0_b = jnp.where(lseq == s, h0[2 * s + 1:2 * s + 2], h0_b)
    first, last = lmod == 0, lmod == lat_seg - 1
    sl_f = s_f[ctx_rows:] + jnp.where(first, _cmul(a_pow[0:1], a_pow[1:2], h0_f), 0.0)
    sl_b = s_b[ctx_rows:] + jnp.where(last, _cmul(a_pow[16:17], a_pow[17:18], h0_b), 0.0)

    hc_f = _block_scan(s_f[:ctx_rows], ctx_seg, a_pow, 0, False)
    hc_b = _block_scan(s_b[:ctx_rows], ctx_seg, a_pow, 1, True)
    hl_f = _block_scan(sl_f, lat_seg, a_pow, 0, False)
    hl_b = _block_scan(sl_b, lat_seg, a_pow, 1, True)

    cmod = lax.broadcasted_iota(jnp.int32, (ctx_rows, 2 * S5_N), 0) % ctx_seg
    pc_f = jnp.where(cmod == 0, 0.0, pltpu.roll(hc_f, 1, 0))
    pc_b = jnp.where(cmod == ctx_seg - 1, 0.0, pltpu.roll(hc_b, ctx_rows - 1, 0))
    pl_f = jnp.where(first, h0_f, pltpu.roll(hl_f, 1, 0))
    pl_b = jnp.where(last, h0_b, pltpu.roll(hl_b, lat_rows - 1, 0))
    prev_f = jnp.concatenate([pc_f, pl_f], axis=0).astype(BF16)
    prev_b = jnp.concatenate([pc_b, pl_b], axis=0).astype(BF16)
    y_ref[0] = _dot(u, k_ref[0]) + _dot(prev_f, cf_ref[0]) + _dot(prev_b, cb_ref[0])

    hf_ref[...] = hc_f
    hb_ref[...] = hc_b
    fin_ref[0, 0] = hf_ref[pl.ds(ctx_seg - 1, n_ctx, stride=ctx_seg), :]
    fin_ref[0, 1] = hb_ref[pl.ds(0, n_ctx, stride=ctx_seg), :]


def _s5_mixer(u_g, ops, h0, n_ctx, ctx_len, n_lat, lat_len):
    k_mat, b_f, b_b, c_f, c_b, a_pow = ops
    rows = u_g.shape[1]
    ctx_seg, lat_seg = ctx_len // S5_T, lat_len // S5_T
    ctx_rows = n_ctx * ctx_seg
    per_g = lambda a: pl.BlockSpec((1,) + a.shape[1:], lambda g: (g,) + (0,) * (a.ndim - 1))
    kern = functools.partial(_s5_kernel, ctx_rows=ctx_rows, ctx_seg=ctx_seg, lat_seg=lat_seg,
                             n_ctx=n_ctx, n_lat=n_lat)
    return pl.pallas_call(
        kern,
        grid=(S5_G,),
        in_specs=[per_g(a) for a in (u_g, k_mat, b_f, b_b, c_f, c_b, a_pow, h0)],
        out_specs=[pl.BlockSpec((1, rows, S5_TP), lambda g: (g, 0, 0)),
                   pl.BlockSpec((1, N_DIR, n_ctx, 2 * S5_N), lambda g: (g, 0, 0, 0))],
        out_shape=[jax.ShapeDtypeStruct((S5_G, rows, S5_TP), F32),
                   jax.ShapeDtypeStruct((S5_G, N_DIR, n_ctx, 2 * S5_N), F32)],
        scratch_shapes=[pltpu.VMEM((ctx_rows, 2 * S5_N), F32)] * N_DIR,
        compiler_params=_params("parallel"),
        name="s5_mixer",
    )(u_g, k_mat, b_f, b_b, c_f, c_b, a_pow, h0)


def _gdn_pre_kernel(x_ref, hp_ref, hn_ref, cw_ref, ba_ref, bat_ref, al_ref, dl_ref, ar_ref, dr_ref,
                    u_ref, wq_ref, kd_ref, a_ref, eg_ref,
                    q_ref, k_ref, v_ref, col_ref, row_ref, m_ref, acc_ref):
    x = x_ref[...]
    rid = lax.broadcasted_iota(jnp.int32, x.shape, 0)
    x_prev = jnp.where(rid == 0, hp_ref[0], pltpu.roll(x, 1, 0))
    x_next = jnp.where(rid == GT - 1, hn_ref[0], pltpu.roll(x, GT - 1, 0))
    cw = cw_ref[...]
    y = _silu(x_prev * cw[0:1] + x * cw[1:2] + x_next * cw[2:3])
    for h in range(GDN_H):
        lo, hi = h * GDN_DK, (h + 1) * GDN_DK
        qh = y[:, lo:hi]
        kh = y[:, GDN_W + lo:GDN_W + hi]
        q_ref[:, lo:hi] = qh * (lax.rsqrt(jnp.sum(qh * qh, axis=-1, keepdims=True) + EPS)
                                * (GDN_DK ** -0.5))
        k_ref[:, lo:hi] = kh * lax.rsqrt(jnp.sum(kh * kh, axis=-1, keepdims=True) + EPS)
    v_ref[...] = y[:, 2 * GDN_W:]

    half = N_GATE // 2
    ba = ba_ref[...]
    lane = lax.broadcasted_iota(jnp.int32, ba.shape, 1)
    gate = jnp.where(lane < half, _sigmoid(ba), -jnp.exp(al_ref[...]) * _softplus(ba + dl_ref[...]))
    ri = lax.broadcasted_iota(jnp.int32, (GT, GT), 0)
    ci = lax.broadcasted_iota(jnp.int32, (GT, GT), 1)
    same = (ri // CHUNK) == (ci // CHUNK)
    lower = jnp.where(same & (ci <= ri), 1.0, 0.0)
    upper = jnp.where(same & (ci >= ri), 1.0, 0.0)
    cum_f = jnp.dot(lower, gate, precision=HIGHEST, preferred_element_type=F32)
    cum_b = jnp.dot(upper, gate, precision=HIGHEST, preferred_element_type=F32)
    fwd_lane = (lane >= half) & (lane < half + GDN_H)
    bwd_lane = (lane >= half + GDN_H) & (lane < N_GATE)
    col_ref[...] = jnp.where(fwd_lane, cum_f, jnp.where(bwd_lane, cum_b, gate))
    bat = bat_ref[...]
    g_row = -jnp.exp(ar_ref[...]) * _softplus(bat + dr_ref[...])
    row_f = jnp.dot(g_row, upper, precision=HIGHEST, preferred_element_type=F32)
    row_b = jnp.dot(g_row, lower, precision=HIGHEST, preferred_element_type=F32)
    srow = lax.broadcasted_iota(jnp.int32, bat.shape, 0)
    row_ref[...] = jnp.where(srow < half + GDN_H, row_f, row_b)

    ii = lax.broadcasted_iota(jnp.int32, (CHUNK, CHUNK), 0)
    jj = lax.broadcasted_iota(jnp.int32, (CHUNK, CHUNK), 1)
    units = [(c, d, h) for c in range(GT // CHUNK) for d in range(N_DIR) for h in range(GDN_H)]

    def gates(c, d, h):
        rows = slice(c * CHUNK, (c + 1) * CHUNK)
        lane_ = d * GDN_H + h
        beta = col_ref[rows, lane_:lane_ + 1]
        gc = col_ref[rows, half + lane_:half + lane_ + 1]
        return rows, beta, gc

    for n, (c, d, h) in enumerate(units):
        rows, beta, gc = gates(c, d, h)
        cols = slice(h * GDN_DK, (h + 1) * GDN_DK)
        gr = row_ref[half + d * GDN_H + h:half + d * GDN_H + h + 1, rows]
        incl = (jj >= ii) if d == 1 else (jj <= ii)
        strict = (jj > ii) if d == 1 else (jj < ii)
        decay = jnp.where(incl, jnp.exp(jnp.where(incl, gc - gr, 0.0)), 0.0)
        k16 = k_ref[rows, cols].astype(BF16)
        kb16 = (k_ref[rows, cols] * beta).astype(BF16)
        m = jnp.where(strict, _dot_nt(kb16, k16) * decay, 0.0)
        a_ref[d, rows, h * CHUNK:(h + 1) * CHUNK] = (
            _dot_nt(q_ref[rows, cols].astype(BF16), k16) * decay).astype(BF16)
        m_ref[n] = m
        acc_ref[n] = jnp.where((ii // 2) == (jj // 2), -m, 0.0)

    b = 2
    while b < CHUNK:
        pair = ((ii // (2 * b)) == (jj // (2 * b))) & ((ii // b) != (jj // b))
        for n in range(len(units)):
            cpart = jnp.where(pair, m_ref[n], 0.0)
            acc = acc_ref[n]
            a16 = acc.astype(BF16)
            y = cpart + _dot(a16, cpart.astype(BF16))
            acc_ref[n] = acc - y - _dot(y.astype(BF16), a16)
        b *= 2

    for n, (c, d, h) in enumerate(units):
        rows, beta, gc = gates(c, d, h)
        cols = slice(h * GDN_DK, (h + 1) * GDN_DK)
        kh = k_ref[rows, cols]
        egc = jnp.exp(gc)
        kb = kh * beta
        rhs = jnp.concatenate([v_ref[rows, cols] * beta, kb * egc], axis=1)
        sol = rhs + _dot(acc_ref[n].astype(BF16), rhs.astype(BF16))
        u_ref[d, rows, cols] = sol[:, :GDN_DK]
        wq_ref[d, c, 0:CHUNK, cols] = sol[:, GDN_DK:].astype(BF16)
        wq_ref[d, c, CHUNK:2 * CHUNK, cols] = (q_ref[rows, cols] * egc).astype(BF16)
        g_last = gc[0:1] if d == 1 else gc[CHUNK - 1:CHUNK]
        kd_ref[d, rows, cols] = (kh * jnp.exp(g_last - gc)).astype(BF16)
        eg_ref[c, d * GDN_H + h:d * GDN_H + h + 1, :] = jnp.broadcast_to(jnp.exp(g_last), (1, GDN_DK))


def _gdn_pre(qkv, halo_prev, halo_next, conv_w, ba, bat, a_log, dt_bias):
    tokens = qkv.shape[0]
    n_tiles = tokens // GT
    cpt = GT // CHUNK
    half = N_GATE // 2
    al = a_log.reshape(-1)
    db = dt_bias.reshape(-1)
    al_l = jnp.zeros((1, GATE_PAD), F32).at[0, half:N_GATE].set(al)
    db_l = jnp.zeros((1, GATE_PAD), F32).at[0, half:N_GATE].set(db)
    al_r = jnp.zeros((N_GATE, 1), F32).at[half:, 0].set(al)
    db_r = jnp.zeros((N_GATE, 1), F32).at[half:, 0].set(db)
    const = lambda i: (0, 0)
    tile = lambda n: pl.BlockSpec((GT, n), lambda i: (i, 0))
    halo = pl.BlockSpec((1, 1, 3 * GDN_W), lambda i: (i, 0, 0))
    n_chunks = tokens // CHUNK
    n_units = cpt * N_DIR * GDN_H
    return pl.pallas_call(
        _gdn_pre_kernel,
        grid=(n_tiles,),
        in_specs=[tile(3 * GDN_W), halo, halo, pl.BlockSpec(conv_w.shape, const),
                  tile(GATE_PAD), pl.BlockSpec((N_GATE, GT), lambda i: (0, i)),
                  pl.BlockSpec((1, GATE_PAD), const), pl.BlockSpec((1, GATE_PAD), const),
                  pl.BlockSpec((N_GATE, 1), const), pl.BlockSpec((N_GATE, 1), const)],
        out_specs=[pl.BlockSpec((N_DIR, GT, GDN_W), lambda i: (0, i, 0)),
                   pl.BlockSpec((N_DIR, cpt, 2 * CHUNK, GDN_W), lambda i: (0, i, 0, 0)),
                   pl.BlockSpec((N_DIR, GT, GDN_W), lambda i: (0, i, 0)),
                   pl.BlockSpec((N_DIR, GT, GDN_H * CHUNK), lambda i: (0, i, 0)),
                   pl.BlockSpec((cpt, N_DIR * GDN_H, GDN_DK), lambda i: (i, 0, 0))],
        out_shape=[jax.ShapeDtypeStruct((N_DIR, tokens, GDN_W), F32),
                   jax.ShapeDtypeStruct((N_DIR, n_chunks, 2 * CHUNK, GDN_W), BF16),
                   jax.ShapeDtypeStruct((N_DIR, tokens, GDN_W), BF16),
                   jax.ShapeDtypeStruct((N_DIR, tokens, GDN_H * CHUNK), BF16),
                   jax.ShapeDtypeStruct((n_chunks, N_DIR * GDN_H, GDN_DK), F32)],
        scratch_shapes=[pltpu.VMEM((GT, GDN_W), F32)] * 3
                       + [pltpu.VMEM((GT, GATE_PAD), F32), pltpu.VMEM((N_GATE, GT), F32),
                          pltpu.VMEM((n_units, CHUNK, CHUNK), F32),
                          pltpu.VMEM((n_units, CHUNK, CHUNK), F32)],
        compiler_params=_params("parallel"),
        name="gdn_pre",
    )(qkv, halo_prev, halo_next, conv_w, ba, bat, al_l, db_l, al_r, db_r)


def _gdn_scan_kernel(*refs, has_s0, has_fin):
    uf, wqf, kdf, af, egf, ub, wqb, kdb, ab, egb = refs[:10]
    pos = 10
    s0_ref = None
    if has_s0:
        s0_ref = refs[pos]
        pos += 1
    of_ref, ob_ref = refs[pos], refs[pos + 1]
    pos += 2
    fin_ref = None
    if has_fin:
        fin_ref = refs[pos]
        pos += 1
    s_ref = refs[pos]
    j = pl.program_id(1)

    @pl.when(j == 0)
    def _():
        if has_s0:
            s_ref[...] = s0_ref[0]
        else:
            s_ref[...] = jnp.zeros(s_ref.shape, F32)

    cpb = SB // CHUNK
    dirs = ((uf, wqf, kdf, af, egf, of_ref), (ub, wqb, kdb, ab, egb, ob_ref))
    for step in range(cpb):
        work = []
        for d, (u_r, wq_r, kd_r, a_r, eg_r, o_r) in enumerate(dirs):
            c = step if d == 0 else cpb - 1 - step
            rows = slice(c * CHUNK, (c + 1) * CHUNK)
            for h in range(GDN_H):
                work.append((d, h, c, rows, slice(h * GDN_DK, (h + 1) * GDN_DK), u_r, wq_r, kd_r, a_r,
                             eg_r, o_r))
        ps = [_dot(wq_r[0, c, :, cols], s_ref[d, h].astype(BF16))
              for (d, h, c, rows, cols, u_r, wq_r, kd_r, a_r, eg_r, o_r) in work]
        v16 = [(u_r[0, rows, cols] - p[:CHUNK]).astype(BF16)
               for p, (d, h, c, rows, cols, u_r, wq_r, kd_r, a_r, eg_r, o_r) in zip(ps, work)]
        for p, v, (d, h, c, rows, cols, u_r, wq_r, kd_r, a_r, eg_r, o_r) in zip(ps, v16, work):
            o_r[rows, cols] = p[CHUNK:] + _dot(a_r[0, rows, h * CHUNK:(h + 1) * CHUNK], v)
        for v, (d, h, c, rows, cols, u_r, wq_r, kd_r, a_r, eg_r, o_r) in zip(v16, work):
            eg = eg_r[c, d * GDN_H + h:d * GDN_H + h + 1, :]
            s_ref[d, h] = s_ref[d, h] * eg + _dot_tn(kd_r[0, rows, cols], v)

    if has_fin:
        @pl.when(j == pl.num_programs(1) - 1)
        def _():
            fin_ref[0] = s_ref[...]


def _gdn_scan(u, wq, kd, a, eg, s0, n_seq, length, base_tok, want_fin):
    n_b = length // SB
    cpb = SB // CHUNK
    base = base_tok // SB
    in_specs, args = [], []
    for d in range(N_DIR):
        blk = (lambda s, j: base + s * n_b + j) if d == 0 else (lambda s, j: base + s * n_b + (n_b - 1 - j))
        in_specs += [pl.BlockSpec((1, SB, GDN_W), lambda s, j, d=d, blk=blk: (d, blk(s, j), 0)),
                     pl.BlockSpec((1, cpb, 2 * CHUNK, GDN_W), lambda s, j, d=d, blk=blk: (d, blk(s, j), 0, 0)),
                     pl.BlockSpec((1, SB, GDN_W), lambda s, j, d=d, blk=blk: (d, blk(s, j), 0)),
                     pl.BlockSpec((1, SB, GDN_H * CHUNK), lambda s, j, d=d, blk=blk: (d, blk(s, j), 0)),
                     pl.BlockSpec((cpb, N_DIR * GDN_H, GDN_DK), lambda s, j, blk=blk: (blk(s, j), 0, 0))]
        args += [u, wq, kd, a, eg]
    state_spec = pl.BlockSpec((1, N_DIR, GDN_H, GDN_DK, GDN_DK), lambda s, j: (s, 0, 0, 0, 0))
    if s0 is not None:
        in_specs.append(state_spec)
        args.append(s0)
    tok = lambda m: pl.BlockSpec((SB, GDN_W), m)
    out_specs = [tok(lambda s, j: (s * n_b + j, 0)), tok(lambda s, j: (s * n_b + (n_b - 1 - j), 0))]
    out_shape = [jax.ShapeDtypeStruct((n_seq * length, GDN_W), F32)] * 2
    if want_fin:
        out_specs.append(state_spec)
        out_shape.append(jax.ShapeDtypeStruct((n_seq, N_DIR, GDN_H, GDN_DK, GDN_DK), F32))
    return pl.pallas_call(
        functools.partial(_gdn_scan_kernel, has_s0=s0 is not None, has_fin=want_fin),
        grid=(n_seq, n_b),
        in_specs=in_specs,
        out_specs=out_specs,
        out_shape=out_shape,
        scratch_shapes=[pltpu.VMEM((N_DIR, GDN_H, GDN_DK, GDN_DK), F32)],
        compiler_params=_params("parallel", "arbitrary"),
        name="gdn_scan",
    )(*args)


def _out_kernel(x_ref, m_ref, y5_ref, u_ref, z_ref, dsk_ref, of_ref, ob_ref, zg_ref, gn_ref, w_ref,
                o_ref):
    m = m_ref[0]
    s5 = _gelu_tanh(y5_ref[...] + u_ref[...] * dsk_ref[...]) * _sigmoid(z_ref[...])
    acc = _dot(s5.astype(BF16), w_ref[0:S5_W, :])
    o = of_ref[...] + ob_ref[...]
    zg = zg_ref[...]
    gn = gn_ref[...]
    for h in range(GDN_H):
        lo, hi = h * GDN_DK, (h + 1) * GDN_DK
        oh = o[:, lo:hi]
        nh = oh * lax.rsqrt(jnp.mean(oh * oh, axis=-1, keepdims=True) + EPS) * gn
        gh = (nh * _silu(zg[:, lo:hi])).astype(BF16)
        acc = acc + _dot(gh, w_ref[S5_W + lo:S5_W + hi, :])
    o_ref[...] = x_ref[...] + m[:, 2 * D_MODEL:3 * D_MODEL] * acc


def _out_proj(x, mods_l, y5, u, z, d_skip, o_f, o_b, zg, gdn_norm, w_out, mod_map):
    tokens = x.shape[0]
    const = lambda i: (0, 0)
    tile = lambda n: pl.BlockSpec((TM, n), lambda i: (i, 0))
    return pl.pallas_call(
        _out_kernel,
        grid=(tokens // TM,),
        in_specs=[tile(D_MODEL), pl.BlockSpec((1, 1, N_MOD * D_MODEL), mod_map),
                  tile(S5_W), tile(S5_W), tile(S5_W), pl.BlockSpec((1, S5_W), const),
                  tile(GDN_W), tile(GDN_W), tile(GDN_W), pl.BlockSpec((1, GDN_DK), const),
                  pl.BlockSpec((D_MODEL, D_MODEL), const)],
        out_specs=tile(D_MODEL),
        out_shape=jax.ShapeDtypeStruct((tokens, D_MODEL), F32),
        compiler_params=_params("parallel"),
        name="out_proj",
    )(x, mods_l, y5, u, z, d_skip, o_f, o_b, zg, gdn_norm, w_out.astype(BF16))


def _mlp_kernel(x_ref, m_ref, g_ref, w1_ref, w2_ref, o_ref):
    m = m_ref[0]
    x = x_ref[...]
    h = _norm_mod(x, g_ref[...], m[:, 3 * D_MODEL:4 * D_MODEL], m[:, 4 * D_MODEL:5 * D_MODEL])
    ff = jnp.maximum(_dot(h.astype(BF16), w1_ref[...]), 0.0)
    ff = (ff * ff).astype(BF16)
    o_ref[...] = x + m[:, 5 * D_MODEL:6 * D_MODEL] * _dot(ff, w2_ref[...])


def _mlp(x, mods_l, g, w1, w2, mod_map):
    tokens = x.shape[0]
    const = lambda i: (0, 0)
    tile = pl.BlockSpec((TM, D_MODEL), lambda i: (i, 0))
    once = pl.Buffered(1)
    return pl.pallas_call(
        _mlp_kernel,
        grid=(tokens // TM,),
        in_specs=[tile, pl.BlockSpec((1, 1, N_MOD * D_MODEL), mod_map),
                  pl.BlockSpec((1, D_MODEL), const),
                  pl.BlockSpec((D_MODEL, D_FF), const, pipeline_mode=once),
                  pl.BlockSpec((D_FF, D_MODEL), const, pipeline_mode=once)],
        out_specs=tile,
        out_shape=jax.ShapeDtypeStruct((tokens, D_MODEL), F32),
        compiler_params=_params("parallel"),
        name="mlp",
    )(x, mods_l, g, w1.astype(BF16), w2.astype(BF16))


def _final_kernel(x_ref, g_ref, o_ref):
    x = x_ref[...]
    o_ref[...] = x * lax.rsqrt(jnp.mean(x * x, axis=-1, keepdims=True) + EPS) * g_ref[...]


def _final_norm(x, g, first_tile, n_tiles):
    return pl.pallas_call(
        _final_kernel,
        grid=(n_tiles,),
        in_specs=[pl.BlockSpec((TM, D_MODEL), lambda i: (first_tile + i, 0)),
                  pl.BlockSpec((1, D_MODEL), lambda i: (0, 0))],
        out_specs=pl.BlockSpec((TM, D_MODEL), lambda i: (i, 0)),
        out_shape=jax.ShapeDtypeStruct((n_tiles * TM, D_MODEL), F32),
        compiler_params=_params("parallel"),
        name="final_norm",
    )(x, g)


def kernel(x_prompt, x_sample, state_s5_re, state_s5_im, state_gdn, c, c_ctx, norm_mix, norm_mlp, w_ada, b_ada, w_in, conv_qkv, s5_lambda_re, s5_lambda_im, s5_log_dt, s5_b_re, s5_b_im, s5_c_re, s5_c_im, s5_d, gdn_a_log, gdn_dt_bias, gdn_norm, w_out, w_mlp_in, w_mlp_out, norm_final):
    n_ctx, ctx_len, _ = x_prompt.shape
    n_lat, lat_len, _ = x_sample.shape
    depth = w_in.shape[0]
    assert ctx_len % TM == 0 and lat_len % TM == 0 and lat_len % (8 * GRID_W) == 0
    assert n_lat + 1 <= 8
    ctx_tok, lat_tok = n_ctx * ctx_len, n_lat * lat_len
    tokens = ctx_tok + lat_tok
    n_ctx_tiles, tiles_per_lat = ctx_tok // TM, lat_len // TM
    mod_map = _mod_row_map(n_ctx_tiles, tiles_per_lat)

    cond = jnp.zeros((8, D_MODEL), F32).at[0].set(c_ctx).at[1:1 + n_lat].set(c)
    mods = _modulation(cond, w_ada, b_ada)
    x = jnp.concatenate([x_prompt.reshape(ctx_tok, D_MODEL), _embed(x_sample)], axis=0)

    starts = np.arange(tokens // GT) * GT
    in_ctx = starts < ctx_tok
    pos = np.where(in_ctx, starts % ctx_len, (starts - ctx_tok) % lat_len)
    has_prev = pos != 0
    has_next = pos + GT != np.where(in_ctx, ctx_len, lat_len)

    fin_re, fin_im, fin_gdn = [], [], []
    for l in range(depth):
        mods_l = mods[l].reshape(8, 1, N_MOD * D_MODEL)
        u, z, qkv, zg, ba, bat = _in_proj(x, mods_l, norm_mix[l].reshape(1, D_MODEL), w_in[l], mod_map)

        ops = _s5_operators(s5_lambda_re[l], s5_lambda_im[l], s5_log_dt[l], s5_b_re[l], s5_b_im[l],
                            s5_c_re[l], s5_c_im[l])
        u_g = u.astype(BF16).reshape(tokens // S5_T, S5_T, S5_G, S5_P).transpose(2, 0, 1, 3)
        u_g = u_g.reshape(S5_G, tokens // S5_T, S5_TP)
        h0 = jnp.concatenate([state_s5_re[:, l], state_s5_im[:, l]], axis=-1)
        h0 = h0.transpose(2, 0, 1, 3).reshape(S5_G, n_lat * N_DIR, 2 * S5_N)
        y_g, s5_fin = _s5_mixer(u_g, ops, h0, n_ctx, ctx_len, n_lat, lat_len)
        y5 = y_g.reshape(S5_G, tokens // S5_T, S5_T, S5_P).transpose(1, 2, 0, 3).reshape(tokens, S5_W)
        s5_fin = s5_fin.transpose(2, 1, 0, 3)
        fin_re.append(s5_fin[..., :S5_N])
        fin_im.append(s5_fin[..., S5_N:])

        qkv_t = qkv.reshape(tokens // GT, GT, 3 * GDN_W)
        zero_row = jnp.zeros((1, 3 * GDN_W), F32)
        prev_rows = jnp.concatenate([zero_row, qkv_t[:-1, GT - 1]], axis=0)
        next_rows = jnp.concatenate([qkv_t[1:, 0], zero_row], axis=0)
        halo_prev = jnp.where(has_prev[:, None], prev_rows, 0.0)[:, None, :]
        halo_next = jnp.where(has_next[:, None], next_rows, 0.0)[:, None, :]
        pre = _gdn_pre(qkv, halo_prev, halo_next, conv_qkv[l], ba, bat, gdn_a_log[l], gdn_dt_bias[l])
        of_c, ob_c, gdn_fin = _gdn_scan(*pre, None, n_ctx, ctx_len, 0, True)
        of_l, ob_l = _gdn_scan(*pre, state_gdn[:, l], n_lat, lat_len, ctx_tok, False)
        fin_gdn.append(gdn_fin)
        o_f = jnp.concatenate([of_c, of_l], axis=0)
        o_b = jnp.concatenate([ob_c, ob_l], axis=0)

        x = _out_proj(x, mods_l, y5, u, z, s5_d[l].reshape(1, S5_W), o_f, o_b, zg,
                      gdn_norm[l].reshape(1, GDN_DK), w_out[l], mod_map)
        x = _mlp(x, mods_l, norm_mlp[l].reshape(1, D_MODEL), w_mlp_in[l], w_mlp_out[l], mod_map)

    g_fin = norm_final.reshape(1, D_MODEL)
    y_prompt = _final_norm(x, g_fin, 0, n_ctx_tiles).reshape(n_ctx, ctx_len, D_MODEL)
    y_sample = _final_norm(x, g_fin, n_ctx_tiles, lat_tok // TM).reshape(n_lat, lat_len, D_MODEL)
    return (y_prompt, y_sample, jnp.stack(fin_re, axis=1), jnp.stack(fin_im, axis=1),
            jnp.stack(fin_gdn, axis=1))
```

```python
import functools
import math

import numpy as np
import jax
import jax.numpy as jnp
from jax import lax
from jax.experimental import pallas as pl
from jax.experimental.pallas import tpu as pltpu

F32 = jnp.float32
BF16 = jnp.bfloat16
HIGHEST = lax.Precision.HIGHEST

D_MODEL = 1024
S5_W = 512
S5_P = 16
S5_G = 32
S5_N = 64
GDN_W = 512
GDN_DK = 128
GDN_H = 4
CHUNK = 64
N_DIR = 2
N_MOD = 6
D_FF = 4 * D_MODEL
GRID_W = 64
POS_BASE = 10000.0
EPS = 1e-6
S5_T = 16
S5_TP = S5_T * S5_P
N_GATE = 2 * N_DIR * GDN_H
GATE_PAD = 128
TM = 256
GT = 128
SB = 256
VMEM_LIMIT = 56 * 1024 * 1024


def _dot(a, b):
    return jnp.dot(a, b, preferred_element_type=F32)


def _dot_nt(a, b):
    return lax.dot_general(a, b, (((1,), (1,)), ((), ())), preferred_element_type=F32)


def _dot_tn(a, b):
    return lax.dot_general(a, b, (((0,), (0,)), ((), ())), preferred_element_type=F32)


def _sigmoid(x):
    return 1.0 / (1.0 + jnp.exp(-x))


def _silu(x):
    return x * _sigmoid(x)


def _softplus(x):
    return jnp.maximum(x, 0.0) + jnp.log1p(jnp.exp(-jnp.abs(x)))


def _gelu_tanh(x):
    c = math.sqrt(2.0 / math.pi)
    return 0.5 * x * (1.0 + jnp.tanh(c * (x + 0.044715 * (x * x * x))))


def _params(*sem):
    return pltpu.CompilerParams(dimension_semantics=sem, vmem_limit_bytes=VMEM_LIMIT)


def _mod_kernel(c_ref, w_ref, b_ref, o_ref):
    sc = _silu(c_ref[...])
    o_ref[0] = jnp.dot(sc, w_ref[0], precision=HIGHEST, preferred_element_type=F32) + b_ref[0]


def _modulation(cond, w_ada, b_ada):
    depth = w_ada.shape[0]
    tn = 1536
    n_w = N_MOD * D_MODEL
    return pl.pallas_call(
        _mod_kernel,
        grid=(depth, n_w // tn),
        in_specs=[pl.BlockSpec((8, D_MODEL), lambda l, j: (0, 0)),
                  pl.BlockSpec((1, D_MODEL, tn), lambda l, j: (l, 0, j)),
                  pl.BlockSpec((1, 1, tn), lambda l, j: (l, 0, j))],
        out_specs=pl.BlockSpec((1, 8, tn), lambda l, j: (l, 0, j)),
        out_shape=jax.ShapeDtypeStruct((depth, 8, n_w), F32),
        compiler_params=_params("parallel", "parallel"),
        name="modulation",
    )(cond, w_ada, b_ada.reshape(depth, 1, n_w))


def _embed_kernel(x_ref, tr_ref, tc_ref, o_ref, *, rows_per_tile, tiles_per_seq):
    r0 = (pl.program_id(0) % tiles_per_seq) * rows_per_tile
    half = D_MODEL // 2
    tc = tc_ref[...]
    for k in range(rows_per_tile):
        lo, hi = k * GRID_W, (k + 1) * GRID_W
        o_ref[lo:hi, 0:half] = x_ref[lo:hi, 0:half] + tr_ref[pl.ds(r0 + k, 1), :]
        o_ref[lo:hi, half:D_MODEL] = x_ref[lo:hi, half:D_MODEL] + tc


def _embed(x_sample):
    n_seq, length, _ = x_sample.shape
    rows = length // GRID_W
    quarter = D_MODEL // 4
    omega = 1.0 / (POS_BASE ** (jnp.arange(quarter, dtype=F32) / quarter))
    ang_r = jnp.arange(rows, dtype=F32)[:, None] * omega
    ang_c = jnp.arange(GRID_W, dtype=F32)[:, None] * omega
    tab_r = jnp.concatenate([jnp.sin(ang_r), jnp.cos(ang_r)], axis=-1)
    tab_c = jnp.concatenate([jnp.sin(ang_c), jnp.cos(ang_c)], axis=-1)
    rows_per_tile = 8
    tile = rows_per_tile * GRID_W
    tiles_per_seq = length // tile
    x2 = x_sample.reshape(n_seq * length, D_MODEL)
    return pl.pallas_call(
        functools.partial(_embed_kernel, rows_per_tile=rows_per_tile, tiles_per_seq=tiles_per_seq),
        grid=(n_seq * tiles_per_seq,),
        in_specs=[pl.BlockSpec((tile, D_MODEL), lambda i: (i, 0)),
                  pl.BlockSpec((rows, D_MODEL // 2), lambda i: (0, 0)),
                  pl.BlockSpec((GRID_W, D_MODEL // 2), lambda i: (0, 0))],
        out_specs=pl.BlockSpec((tile, D_MODEL), lambda i: (i, 0)),
        out_shape=jax.ShapeDtypeStruct(x2.shape, F32),
        compiler_params=_params("parallel"),
        name="pos_embed",
    )(x2, tab_r, tab_c)


def _norm_mod(x, g, shift, scale):
    ms = jnp.mean(x * x, axis=-1, keepdims=True)
    h = x * lax.rsqrt(ms + EPS) * g
    return h * (1.0 + scale) + shift


LANES = 128
SLOTS = LANES // S5_P


def _regroup_in(u_refs, ug_ref):
    blocks = TM // S5_T
    slot = lax.broadcasted_iota(jnp.int32, (blocks, LANES), 1) // S5_P
    rot = {}
    for j in range(S5_T):
        for a in range(S5_W // LANES):
            t = u_refs[a][pl.ds(j, blocks, stride=S5_T), :]
            rot[j, a] = t if j % SLOTS == 0 else pltpu.roll(t, S5_P * (j % SLOTS), 1)
    for g in range(S5_G):
        a, s = g // SLOTS, g % SLOTS
        for jt in range(S5_T // SLOTS):
            acc = None
            for sig in range(SLOTS):
                piece = rot[SLOTS * jt + (sig - s) % SLOTS, a]
                acc = piece if acc is None else jnp.where(slot == sig, piece, acc)
            ug_ref[g, :, jt * LANES:(jt + 1) * LANES] = acc.astype(ug_ref.dtype)


def _regroup_out(yg_ref, y_refs):
    blocks = TM // S5_T
    slot = lax.broadcasted_iota(jnp.int32, (blocks, LANES), 1) // S5_P
    for j in range(S5_T):
        jt = j // SLOTS
        for a in range(S5_W // LANES):
            acc = None
            for s in range(SLOTS):
                piece = yg_ref[SLOTS * a + s, :, jt * LANES:(jt + 1) * LANES]
                acc = piece if acc is None else jnp.where(slot == (s + j) % SLOTS, piece, acc)
            if j % SLOTS:
                acc = pltpu.roll(acc, LANES - S5_P * (j % SLOTS), 1)
            y_refs[a][pl.ds(j, blocks, stride=S5_T), :] = acc


def _in_kernel(x_ref, m_ref, g_ref, wu_ref, wz_ref, wqkv_ref, wzg_ref, wba_ref, wbat_ref,
               u_ref, ug_ref, z_ref, qkv_ref, zg_ref, ba_ref, bat_ref, *ut_refs):
    m = m_ref[0]
    h = _norm_mod(x_ref[...], g_ref[...], m[:, 0:D_MODEL], m[:, D_MODEL:2 * D_MODEL]).astype(BF16)
    u = _dot(h, wu_ref[...])
    u_ref[...] = u
    for a, ut_ref in enumerate(ut_refs):
        ut_ref[...] = u[:, a * LANES:(a + 1) * LANES]
    _regroup_in(ut_refs, ug_ref)
    z_ref[...] = _dot(h, wz_ref[...])
    qkv_ref[...] = _dot(h, wqkv_ref[...])
    zg_ref[...] = _dot(h, wzg_ref[...])
    ba_ref[...] = _dot(h, wba_ref[...])
    bat_ref[...] = _dot_nt(wbat_ref[...], h)


def _mod_row_map(n_ctx_tiles, tiles_per_lat):
    def index(i):
        return (jnp.where(i < n_ctx_tiles, 0, 1 + (i - n_ctx_tiles) // tiles_per_lat), 0, 0)
    return index


def _in_proj(x, mods_l, g, w_in, mod_map):
    tokens = x.shape[0]
    o1, o2, o3, o4 = S5_W, 2 * S5_W, 2 * S5_W + 3 * GDN_W, 2 * S5_W + 4 * GDN_W
    wb = w_in.astype(BF16)
    w_ba = jnp.pad(wb[:, o4:], ((0, 0), (0, GATE_PAD - N_GATE)))
    w_bat = wb[:, o4:].T
    const = lambda i: (0, 0)
    tile = lambda n: pl.BlockSpec((TM, n), lambda i: (i, 0))
    full = lambda a: pl.BlockSpec(a.shape, const)
    ws = (wb[:, :o1], wb[:, o1:o2], wb[:, o2:o3], wb[:, o3:o4], w_ba, w_bat)
    return pl.pallas_call(
        _in_kernel,
        grid=(tokens // TM,),
        in_specs=[tile(D_MODEL), pl.BlockSpec((1, 1, N_MOD * D_MODEL), mod_map), full(g)]
                 + [full(w) for w in ws],
        out_specs=[tile(S5_W), pl.BlockSpec((S5_G, TM // S5_T, S5_TP), lambda i: (0, i, 0)),
                   tile(S5_W), tile(3 * GDN_W), tile(GDN_W), tile(GATE_PAD),
                   pl.BlockSpec((N_GATE, TM), lambda i: (0, i))],
        out_shape=[jax.ShapeDtypeStruct((tokens, S5_W), F32),
                   jax.ShapeDtypeStruct((S5_G, tokens // S5_T, S5_TP), BF16),
                   jax.ShapeDtypeStruct((tokens, S5_W), F32),
                   jax.ShapeDtypeStruct((tokens, 3 * GDN_W), F32),
                   jax.ShapeDtypeStruct((tokens, GDN_W), F32),
                   jax.ShapeDtypeStruct((tokens, GATE_PAD), F32),
                   jax.ShapeDtypeStruct((N_GATE, tokens), F32)],
        scratch_shapes=[pltpu.VMEM((TM, LANES), F32)] * (S5_W // LANES),
        compiler_params=_params("parallel"),
        name="in_proj",
    )(x, mods_l, g, *ws)


def _s5_operators(lam_re, lam_im, log_dt, b_re, b_im, c_re, c_im):
    lam_re = jnp.minimum(lam_re, -1e-4)
    dt = jnp.exp(log_dt)[..., None]
    x, th = lam_re * dt, lam_im * dt
    mag = jnp.exp(x)
    abar_re, abar_im = mag * jnp.cos(th), mag * jnp.sin(th)
    num_re, num_im = abar_re - 1.0, abar_im
    den = lam_re * lam_re + lam_im * lam_im
    f_re = (num_re * lam_re + num_im * lam_im) / den
    f_im = (num_im * lam_re - num_re * lam_im) / den
    bbar_re = f_re[..., None] * b_re - f_im[..., None] * b_im
    bbar_im = f_re[..., None] * b_im + f_im[..., None] * b_re

    def power(m):
        mm = jnp.asarray(m, F32)
        e = jnp.exp(x[..., None] * mm.reshape(-1)).reshape(x.shape + mm.shape)
        ang = (th[..., None] * mm.reshape(-1)).reshape(x.shape + mm.shape)
        return e * jnp.cos(ang), e * jnp.sin(ang)

    t = np.arange(S5_T + 1)
    pr, pi = power(t)
    ca_re = c_re[:, :, None] * pr.transpose(0, 1, 3, 2)[:, :, :, None, :] \
        - c_im[:, :, None] * pi.transpose(0, 1, 3, 2)[:, :, :, None, :]
    ca_im = c_re[:, :, None] * pi.transpose(0, 1, 3, 2)[:, :, :, None, :] \
        + c_im[:, :, None] * pr.transpose(0, 1, 3, 2)[:, :, :, None, :]
    k_lag = (jnp.einsum('dgtpn,dgnq->dgtpq', ca_re[:, :, :S5_T], bbar_re, precision=HIGHEST)
             - jnp.einsum('dgtpn,dgnq->dgtpq', ca_im[:, :, :S5_T], bbar_im, precision=HIGHEST))
    jj, ii = np.meshgrid(np.arange(S5_T), np.arange(S5_T), indexing='ij')
    lag_f = np.clip(ii - jj, 0, S5_T - 1)
    lag_b = np.clip(jj - ii, 0, S5_T - 1)
    kf = jnp.where((ii >= jj)[:, :, None, None], k_lag[0][:, lag_f], 0.0)
    kb = jnp.where((jj >= ii)[:, :, None, None], k_lag[1][:, lag_b], 0.0)
    k_mat = (kf + kb).transpose(0, 1, 4, 2, 3).reshape(S5_G, S5_TP, S5_TP)

    def in_state(d, expo):
        er, ei = pr[d][..., expo], pi[d][..., expo]
        re = er[..., None] * bbar_re[d][:, :, None, :] - ei[..., None] * bbar_im[d][:, :, None, :]
        im = er[..., None] * bbar_im[d][:, :, None, :] + ei[..., None] * bbar_re[d][:, :, None, :]
        m = jnp.concatenate([re, im], axis=1)
        return m.transpose(0, 2, 3, 1).reshape(S5_G, S5_TP, 2 * S5_N)

    b_f = in_state(0, np.arange(S5_T - 1, -1, -1))
    b_b = in_state(1, np.arange(S5_T))

    def state_out(d, expo):
        re = ca_re[d][:, expo]
        im = ca_im[d][:, expo]
        m = jnp.concatenate([re, -im], axis=-1)
        return m.transpose(0, 3, 1, 2).reshape(S5_G, 2 * S5_N, S5_TP)

    c_f = state_out(0, np.arange(1, S5_T + 1))
    c_b = state_out(1, np.arange(S5_T, 0, -1))

    steps = S5_T * (2 ** np.arange(8))
    qr, qi = power(steps)
    a1 = jnp.concatenate([qr, qr], axis=2)
    a2 = jnp.concatenate([-qi, qi], axis=2)
    a_pow = jnp.stack([a1, a2], axis=-1)
    a_pow = a_pow.transpose(1, 0, 3, 4, 2).reshape(S5_G, N_DIR * 16, 2 * S5_N)
    k_mat = _slot_order(_slot_order(k_mat, 1), 2)
    b_f, b_b = _slot_order(b_f, 1), _slot_order(b_b, 1)
    c_f, c_b = _slot_order(c_f, 2), _slot_order(c_b, 2)
    return k_mat.astype(BF16), b_f.astype(BF16), b_b.astype(BF16), c_f.astype(BF16), c_b.astype(BF16), a_pow


def _slot_order(a, axis):
    j, q = np.meshgrid(np.arange(S5_T), np.arange(S5_P), indexing='ij')
    perm = np.zeros((SLOTS, S5_TP, S5_TP), np.float32)
    for s in range(SLOTS):
        new = LANES * (j // SLOTS) + S5_P * ((s + j) % SLOTS) + q
        perm[s, new.reshape(-1), (j * S5_P + q).reshape(-1)] = 1.0
    a4 = a.reshape((S5_G // SLOTS, SLOTS) + a.shape[1:])
    if axis == 1:
        out = jnp.einsum('smn,asnk->asmk', perm, a4, precision=HIGHEST)
    else:
        out = jnp.einsum('askn,smn->askm', a4, perm, precision=HIGHEST)
    return out.reshape(a.shape)


def _lane_swap(x):
    return pltpu.roll(x, S5_N, 1)


def _cmul(a1, a2, h):
    return a1 * h + a2 * _lane_swap(h)


def _block_scan(s, seg, a_pow, d, reverse):
    rows = s.shape[0]
    ridx = lax.broadcasted_iota(jnp.int32, s.shape, 0) % seg
    h = s
    k, step = 0, 1
    while step < seg:
        a1 = a_pow[d * 16 + 2 * k:d * 16 + 2 * k + 1]
        a2 = a_pow[d * 16 + 2 * k + 1:d * 16 + 2 * k + 2]
        if reverse:
            sh = jnp.where(ridx < seg - step, pltpu.roll(h, rows - step, 0), 0.0)
        else:
            sh = jnp.where(ridx >= step, pltpu.roll(h, step, 0), 0.0)
        h = h + _cmul(a1, a2, sh)
        k += 1
        step *= 2
    return h


def _s5_kernel(u_ref, k_ref, bf_ref, bb_ref, cf_ref, cb_ref, ap_ref, h0_ref,
               y_ref, fin_ref, hf_ref, hb_ref, *, ctx_rows, ctx_seg, lat_seg, n_ctx, n_lat):
    u = u_ref[0]
    a_pow = ap_ref[0]
    s_f = _dot(u, bf_ref[0])
    s_b = _dot(u, bb_ref[0])
    lat_rows = u.shape[0] - ctx_rows

    lrow = lax.broadcasted_iota(jnp.int32, (lat_rows, 2 * S5_N), 0)
    lmod = lrow % lat_seg
    lseq = lrow // lat_seg
    h0 = h0_ref[0]
    h0_f = jnp.zeros((lat_rows, 2 * S5_N), F32)
    h0_b = jnp.zeros((lat_rows, 2 * S5_N), F32)
    for s in range(n_lat):
        h0_f = jnp.where(lseq == s, h0[2 * s:2 * s + 1], h0_f)
        h0_b = jnp.where(lseq == s, h0[2 * s + 1:2 * s + 2], h0_b)
    first, last = lmod == 0, lmod == lat_seg - 1
    sl_f = s_f[ctx_rows:] + jnp.where(first, _cmul(a_pow[0:1], a_pow[1:2], h0_f), 0.0)
    sl_b = s_b[ctx_rows:] + jnp.where(last, _cmul(a_pow[16:17], a_pow[17:18], h0_b), 0.0)

    hc_f = _block_scan(s_f[:ctx_rows], ctx_seg, a_pow, 0, False)
    hc_b = _block_scan(s_b[:ctx_rows], ctx_seg, a_pow, 1, True)
    hl_f = _block_scan(sl_f, lat_seg, a_pow, 0, False)
    hl_b = _block_scan(sl_b, lat_seg, a_pow, 1, True)

    cmod = lax.broadcasted_iota(jnp.int32, (ctx_rows, 2 * S5_N), 0) % ctx_seg
    pc_f = jnp.where(cmod == 0, 0.0, pltpu.roll(hc_f, 1, 0))
    pc_b = jnp.where(cmod == ctx_seg - 1, 0.0, pltpu.roll(hc_b, ctx_rows - 1, 0))
    pl_f = jnp.where(first, h0_f, pltpu.roll(hl_f, 1, 0))
    pl_b = jnp.where(last, h0_b, pltpu.roll(hl_b, lat_rows - 1, 0))
    prev_f = jnp.concatenate([pc_f, pl_f], axis=0).astype(BF16)
    prev_b = jnp.concatenate([pc_b, pl_b], axis=0).astype(BF16)
    y_ref[0] = _dot(u, k_ref[0]) + _dot(prev_f, cf_ref[0]) + _dot(prev_b, cb_ref[0])

    hf_ref[...] = hc_f
    hb_ref[...] = hc_b
    fin_ref[0, 0] = hf_ref[pl.ds(ctx_seg - 1, n_ctx, stride=ctx_seg), :]
    fin_ref[0, 1] = hb_ref[pl.ds(0, n_ctx, stride=ctx_seg), :]


def _s5_mixer(u_g, ops, h0, n_ctx, ctx_len, n_lat, lat_len):
    k_mat, b_f, b_b, c_f, c_b, a_pow = ops
    rows = u_g.shape[1]
    ctx_seg, lat_seg = ctx_len // S5_T, lat_len // S5_T
    ctx_rows = n_ctx * ctx_seg
    per_g = lambda a: pl.BlockSpec((1,) + a.shape[1:], lambda g: (g,) + (0,) * (a.ndim - 1))
    kern = functools.partial(_s5_kernel, ctx_rows=ctx_rows, ctx_seg=ctx_seg, lat_seg=lat_seg,
                             n_ctx=n_ctx, n_lat=n_lat)
    return pl.pallas_call(
        kern,
        grid=(S5_G,),
        in_specs=[per_g(a) for a in (u_g, k_mat, b_f, b_b, c_f, c_b, a_pow, h0)],
        out_specs=[pl.BlockSpec((1, rows, S5_TP), lambda g: (g, 0, 0)),
                   pl.BlockSpec((1, N_DIR, n_ctx, 2 * S5_N), lambda g: (g, 0, 0, 0))],
        out_shape=[jax.ShapeDtypeStruct((S5_G, rows, S5_TP), F32),
                   jax.ShapeDtypeStruct((S5_G, N_DIR, n_ctx, 2 * S5_N), F32)],
        scratch_shapes=[pltpu.VMEM((ctx_rows, 2 * S5_N), F32)] * N_DIR,
        compiler_params=_params("parallel"),
        name="s5_mixer",
    )(u_g, k_mat, b_f, b_b, c_f, c_b, a_pow, h0)


def _gdn_pre_kernel(x_ref, hp_ref, hn_ref, cw_ref, ba_ref, bat_ref, al_ref, dl_ref, ar_ref, dr_ref,
                    u_ref, wq_ref, kd_ref, a_ref, eg_ref,
                    q_ref, k_ref, v_ref, col_ref, row_ref, m_ref, acc_ref):
    x = x_ref[...]
    rid = lax.broadcasted_iota(jnp.int32, x.shape, 0)
    x_prev = jnp.where(rid == 0, hp_ref[0], pltpu.roll(x, 1, 0))
    x_next = jnp.where(rid == GT - 1, hn_ref[0], pltpu.roll(x, GT - 1, 0))
    cw = cw_ref[...]
    y = _silu(x_prev * cw[0:1] + x * cw[1:2] + x_next * cw[2:3])
    for h in range(GDN_H):
        lo, hi = h * GDN_DK, (h + 1) * GDN_DK
        qh = y[:, lo:hi]
        kh = y[:, GDN_W + lo:GDN_W + hi]
        q_ref[:, lo:hi] = qh * (lax.rsqrt(jnp.sum(qh * qh, axis=-1, keepdims=True) + EPS)
                                * (GDN_DK ** -0.5))
        k_ref[:, lo:hi] = kh * lax.rsqrt(jnp.sum(kh * kh, axis=-1, keepdims=True) + EPS)
    v_ref[...] = y[:, 2 * GDN_W:]

    half = N_GATE // 2
    ba = ba_ref[...]
    lane = lax.broadcasted_iota(jnp.int32, ba.shape, 1)
    gate = jnp.where(lane < half, _sigmoid(ba), -jnp.exp(al_ref[...]) * _softplus(ba + dl_ref[...]))
    ri = lax.broadcasted_iota(jnp.int32, (GT, GT), 0)
    ci = lax.broadcasted_iota(jnp.int32, (GT, GT), 1)
    same = (ri // CHUNK) == (ci // CHUNK)
    lower = jnp.where(same & (ci <= ri), 1.0, 0.0)
    upper = jnp.where(same & (ci >= ri), 1.0, 0.0)
    cum_f = jnp.dot(lower, gate, precision=HIGHEST, preferred_element_type=F32)
    cum_b = jnp.dot(upper, gate, precision=HIGHEST, preferred_element_type=F32)
    fwd_lane = (lane >= half) & (lane < half + GDN_H)
    bwd_lane = (lane >= half + GDN_H) & (lane < N_GATE)
    col_ref[...] = jnp.where(fwd_lane, cum_f, jnp.where(bwd_lane, cum_b, gate))
    bat = bat_ref[...]
    g_row = -jnp.exp(ar_ref[...]) * _softplus(bat + dr_ref[...])
    row_f = jnp.dot(g_row, upper, precision=HIGHEST, preferred_element_type=F32)
    row_b = jnp.dot(g_row, lower, precision=HIGHEST, preferred_element_type=F32)
    srow = lax.broadcasted_iota(jnp.int32, bat.shape, 0)
    row_ref[...] = jnp.where(srow < half + GDN_H, row_f, row_b)

    ii = lax.broadcasted_iota(jnp.int32, (CHUNK, CHUNK), 0)
    jj = lax.broadcasted_iota(jnp.int32, (CHUNK, CHUNK), 1)
    units = [(c, d, h) for c in range(GT // CHUNK) for d in range(N_DIR) for h in range(GDN_H)]

    def gates(c, d, h):
        rows = slice(c * CHUNK, (c + 1) * CHUNK)
        lane_ = d * GDN_H + h
        beta = col_ref[rows, lane_:lane_ + 1]
        gc = col_ref[rows, half + lane_:half + lane_ + 1]
        return rows, beta, gc

    for n, (c, d, h) in enumerate(units):
        rows, beta, gc = gates(c, d, h)
        cols = slice(h * GDN_DK, (h + 1) * GDN_DK)
        gr = row_ref[half + d * GDN_H + h:half + d * GDN_H + h + 1, rows]
        incl = (jj >= ii) if d == 1 else (jj <= ii)
        strict = (jj > ii) if d == 1 else (jj < ii)
        decay = jnp.where(incl, jnp.exp(jnp.where(incl, gc - gr, 0.0)), 0.0)
        k16 = k_ref[rows, cols].astype(BF16)
        kb16 = (k_ref[rows, cols] * beta).astype(BF16)
        m = jnp.where(strict, _dot_nt(kb16, k16) * decay, 0.0)
        a_ref[d, rows, h * CHUNK:(h + 1) * CHUNK] = (
            _dot_nt(q_ref[rows, cols].astype(BF16), k16) * decay).astype(BF16)
        m_ref[n] = m
        acc_ref[n] = jnp.where((ii // 2) == (jj // 2), -m, 0.0)

    b = 2
    while b < CHUNK:
        pair = ((ii // (2 * b)) == (jj // (2 * b))) & ((ii // b) != (jj // b))
        for n in range(len(units)):
            cpart = jnp.where(pair, m_ref[n], 0.0)
            acc = acc_ref[n]
            a16 = acc.astype(BF16)
            y = cpart + _dot(a16, cpart.astype(BF16))
            acc_ref[n] = acc - y - _dot(y.astype(BF16), a16)
        b *= 2

    for n, (c, d, h) in enumerate(units):
        rows, beta, gc = gates(c, d, h)
        cols = slice(h * GDN_DK, (h + 1) * GDN_DK)
        kh = k_ref[rows, cols]
        egc = jnp.exp(gc)
        kb = kh * beta
        rhs = jnp.concatenate([v_ref[rows, cols] * beta, kb * egc], axis=1)
        sol = rhs + _dot(acc_ref[n].astype(BF16), rhs.astype(BF16))
        u_ref[d, rows, cols] = sol[:, :GDN_DK]
        wq_ref[d, c, 0:CHUNK, cols] = sol[:, GDN_DK:].astype(BF16)
        wq_ref[d, c, CHUNK:2 * CHUNK, cols] = (q_ref[rows, cols] * egc).astype(BF16)
        g_last = gc[0:1] if d == 1 else gc[CHUNK - 1:CHUNK]
        kd_ref[d, rows, cols] = (kh * jnp.exp(g_last - gc)).astype(BF16)
        eg_ref[c, d * GDN_H + h:d * GDN_H + h + 1, :] = jnp.broadcast_to(jnp.exp(g_last), (1, GDN_DK))


def _gdn_pre(qkv, halo_prev, halo_next, conv_w, ba, bat, a_log, dt_bias):
    tokens = qkv.shape[0]
    n_tiles = tokens // GT
    cpt = GT // CHUNK
    half = N_GATE // 2
    al = a_log.reshape(-1)
    db = dt_bias.reshape(-1)
    al_l = jnp.zeros((1, GATE_PAD), F32).at[0, half:N_GATE].set(al)
    db_l = jnp.zeros((1, GATE_PAD), F32).at[0, half:N_GATE].set(db)
    al_r = jnp.zeros((N_GATE, 1), F32).at[half:, 0].set(al)
    db_r = jnp.zeros((N_GATE, 1), F32).at[half:, 0].set(db)
    const = lambda i: (0, 0)
    tile = lambda n: pl.BlockSpec((GT, n), lambda i: (i, 0))
    halo = pl.BlockSpec((1, 1, 3 * GDN_W), lambda i: (i, 0, 0))
    n_chunks = tokens // CHUNK
    n_units = cpt * N_DIR * GDN_H
    return pl.pallas_call(
        _gdn_pre_kernel,
        grid=(n_tiles,),
        in_specs=[tile(3 * GDN_W), halo, halo, pl.BlockSpec(conv_w.shape, const),
                  tile(GATE_PAD), pl.BlockSpec((N_GATE, GT), lambda i: (0, i)),
                  pl.BlockSpec((1, GATE_PAD), const), pl.BlockSpec((1, GATE_PAD), const),
                  pl.BlockSpec((N_GATE, 1), const), pl.BlockSpec((N_GATE, 1), const)],
        out_specs=[pl.BlockSpec((N_DIR, GT, GDN_W), lambda i: (0, i, 0)),
                   pl.BlockSpec((N_DIR, cpt, 2 * CHUNK, GDN_W), lambda i: (0, i, 0, 0)),
                   pl.BlockSpec((N_DIR, GT, GDN_W), lambda i: (0, i, 0)),
                   pl.BlockSpec((N_DIR, GT, GDN_H * CHUNK), lambda i: (0, i, 0)),
                   pl.BlockSpec((cpt, N_DIR * GDN_H, GDN_DK), lambda i: (i, 0, 0))],
        out_shape=[jax.ShapeDtypeStruct((N_DIR, tokens, GDN_W), F32),
                   jax.ShapeDtypeStruct((N_DIR, n_chunks, 2 * CHUNK, GDN_W), BF16),
                   jax.ShapeDtypeStruct((N_DIR, tokens, GDN_W), BF16),
                   jax.ShapeDtypeStruct((N_DIR, tokens, GDN_H * CHUNK), BF16),
                   jax.ShapeDtypeStruct((n_chunks, N_DIR * GDN_H, GDN_DK), F32)],
        scratch_shapes=[pltpu.VMEM((GT, GDN_W), F32)] * 3
                       + [pltpu.VMEM((GT, GATE_PAD), F32), pltpu.VMEM((N_GATE, GT), F32),
                          pltpu.VMEM((n_units, CHUNK, CHUNK), F32),
                          pltpu.VMEM((n_units, CHUNK, CHUNK), F32)],
        compiler_params=_params("parallel"),
        name="gdn_pre",
    )(qkv, halo_prev, halo_next, conv_w, ba, bat, al_l, db_l, al_r, db_r)


def _gdn_scan_kernel(*refs, has_s0, has_fin):
    uf, wqf, kdf, af, egf, ub, wqb, kdb, ab, egb = refs[:10]
    pos = 10
    s0_ref = None
    if has_s0:
        s0_ref = refs[pos]
        pos += 1
    of_ref, ob_ref = refs[pos], refs[pos + 1]
    pos += 2
    fin_ref = None
    if has_fin:
        fin_ref = refs[pos]
        pos += 1
    s_ref = refs[pos]
    j = pl.program_id(1)

    @pl.when(j == 0)
    def _():
        if has_s0:
            s_ref[...] = s0_ref[0]
        else:
            s_ref[...] = jnp.zeros(s_ref.shape, F32)

    cpb = SB // CHUNK
    dirs = ((uf, wqf, kdf, af, egf, of_ref), (ub, wqb, kdb, ab, egb, ob_ref))
    for step in range(cpb):
        work = []
        for d, (u_r, wq_r, kd_r, a_r, eg_r, o_r) in enumerate(dirs):
            c = step if d == 0 else cpb - 1 - step
            rows = slice(c * CHUNK, (c + 1) * CHUNK)
            for h in range(GDN_H):
                work.append((d, h, c, rows, slice(h * GDN_DK, (h + 1) * GDN_DK), u_r, wq_r, kd_r, a_r,
                             eg_r, o_r))
        ps = [_dot(wq_r[0, c, :, cols], s_ref[d, h].astype(BF16))
              for (d, h, c, rows, cols, u_r, wq_r, kd_r, a_r, eg_r, o_r) in work]
        v16 = [(u_r[0, rows, cols] - p[:CHUNK]).astype(BF16)
               for p, (d, h, c, rows, cols, u_r, wq_r, kd_r, a_r, eg_r, o_r) in zip(ps, work)]
        for p, v, (d, h, c, rows, cols, u_r, wq_r, kd_r, a_r, eg_r, o_r) in zip(ps, v16, work):
            o_r[rows, cols] = p[CHUNK:] + _dot(a_r[0, rows, h * CHUNK:(h + 1) * CHUNK], v)
        for v, (d, h, c, rows, cols, u_r, wq_r, kd_r, a_r, eg_r, o_r) in zip(v16, work):
            eg = eg_r[c, d * GDN_H + h:d * GDN_H + h + 1, :]
            s_ref[d, h] = s_ref[d, h] * eg + _dot_tn(kd_r[0, rows, cols], v)

    if has_fin:
        @pl.when(j == pl.num_programs(1) - 1)
        def _():
            fin_ref[0] = s_ref[...]


def _gdn_scan(u, wq, kd, a, eg, s0, n_seq, length, base_tok, want_fin):
    n_b = length // SB
    cpb = SB // CHUNK
    base = base_tok // SB
    in_specs, args = [], []
    for d in range(N_DIR):
        blk = (lambda s, j: base + s * n_b + j) if d == 0 else (lambda s, j: base + s * n_b + (n_b - 1 - j))
        in_specs += [pl.BlockSpec((1, SB, GDN_W), lambda s, j, d=d, blk=blk: (d, blk(s, j), 0)),
                     pl.BlockSpec((1, cpb, 2 * CHUNK, GDN_W), lambda s, j, d=d, blk=blk: (d, blk(s, j), 0, 0)),
                     pl.BlockSpec((1, SB, GDN_W), lambda s, j, d=d, blk=blk: (d, blk(s, j), 0)),
                     pl.BlockSpec((1, SB, GDN_H * CHUNK), lambda s, j, d=d, blk=blk: (d, blk(s, j), 0)),
                     pl.BlockSpec((cpb, N_DIR * GDN_H, GDN_DK), lambda s, j, blk=blk: (blk(s, j), 0, 0))]
        args += [u, wq, kd, a, eg]
    state_spec = pl.BlockSpec((1, N_DIR, GDN_H, GDN_DK, GDN_DK), lambda s, j: (s, 0, 0, 0, 0))
    if s0 is not None:
        in_specs.append(state_spec)
        args.append(s0)
    tok = lambda m: pl.BlockSpec((SB, GDN_W), m)
    out_specs = [tok(lambda s, j: (s * n_b + j, 0)), tok(lambda s, j: (s * n_b + (n_b - 1 - j), 0))]
    out_shape = [jax.ShapeDtypeStruct((n_seq * length, GDN_W), F32)] * 2
    if want_fin:
        out_specs.append(state_spec)
        out_shape.append(jax.ShapeDtypeStruct((n_seq, N_DIR, GDN_H, GDN_DK, GDN_DK), F32))
    return pl.pallas_call(
        functools.partial(_gdn_scan_kernel, has_s0=s0 is not None, has_fin=want_fin),
        grid=(n_seq, n_b),
        in_specs=in_specs,
        out_specs=out_specs,
        out_shape=out_shape,
        scratch_shapes=[pltpu.VMEM((N_DIR, GDN_H, GDN_DK, GDN_DK), F32)],
        compiler_params=_params("parallel", "arbitrary"),
        name="gdn_scan",
    )(*args)


def _out_kernel(x_ref, m_ref, yg_ref, u_ref, z_ref, dsk_ref, ofc_ref, obc_ref, ofl_ref, obl_ref, zg_ref,
                gn_ref, w_ref, o_ref, *y5_refs, n_ctx_tiles):
    m = m_ref[0]
    _regroup_out(yg_ref, y5_refs)
    y5 = jnp.concatenate([r[...] for r in y5_refs], axis=1)
    s5 = _gelu_tanh(y5 + u_ref[...] * dsk_ref[...]) * _sigmoid(z_ref[...])
    acc = _dot(s5.astype(BF16), w_ref[0:S5_W, :])
    in_ctx = pl.program_id(0) < n_ctx_tiles
    o = jnp.where(in_ctx, ofc_ref[...] + obc_ref[...], ofl_ref[...] + obl_ref[...])
    zg = zg_ref[...]
    gn = gn_ref[...]
    for h in range(GDN_H):
        lo, hi = h * GDN_DK, (h + 1) * GDN_DK
        oh = o[:, lo:hi]
        nh = oh * lax.rsqrt(jnp.mean(oh * oh, axis=-1, keepdims=True) + EPS) * gn
        gh = (nh * _silu(zg[:, lo:hi])).astype(BF16)
        acc = acc + _dot(gh, w_ref[S5_W + lo:S5_W + hi, :])
    o_ref[...] = x_ref[...] + m[:, 2 * D_MODEL:3 * D_MODEL] * acc


def _out_proj(x, mods_l, y_g, u, z, d_skip, o_ctx, o_lat, zg, gdn_norm, w_out, mod_map, n_ctx_tiles):
    tokens = x.shape[0]
    n_lat_tiles = tokens // TM - n_ctx_tiles
    const = lambda i: (0, 0)
    tile = lambda n: pl.BlockSpec((TM, n), lambda i: (i, 0))
    ctx_tile = pl.BlockSpec((TM, GDN_W), lambda i: (jnp.minimum(i, n_ctx_tiles - 1), 0))
    lat_tile = pl.BlockSpec((TM, GDN_W), lambda i: (jnp.maximum(i - n_ctx_tiles, 0), 0))
    assert n_lat_tiles > 0
    return pl.pallas_call(
        functools.partial(_out_kernel, n_ctx_tiles=n_ctx_tiles),
        grid=(tokens // TM,),
        in_specs=[tile(D_MODEL), pl.BlockSpec((1, 1, N_MOD * D_MODEL), mod_map),
                  pl.BlockSpec((S5_G, TM // S5_T, S5_TP), lambda i: (0, i, 0)),
                  tile(S5_W), tile(S5_W), pl.BlockSpec((1, S5_W), const),
                  ctx_tile, ctx_tile, lat_tile, lat_tile,
                  tile(GDN_W), pl.BlockSpec((1, GDN_DK), const),
                  pl.BlockSpec((D_MODEL, D_MODEL), const)],
        out_specs=tile(D_MODEL),
        out_shape=jax.ShapeDtypeStruct((tokens, D_MODEL), F32),
        scratch_shapes=[pltpu.VMEM((TM, LANES), F32)] * (S5_W // LANES),
        compiler_params=_params("parallel"),
        name="out_proj",
    )(x, mods_l, y_g, u, z, d_skip, *o_ctx, *o_lat, zg, gdn_norm, w_out.astype(BF16))


def _mlp_kernel(x_ref, m_ref, g_ref, w1_ref, w2_ref, o_ref):
    m = m_ref[0]
    x = x_ref[...]
    h = _norm_mod(x, g_ref[...], m[:, 3 * D_MODEL:4 * D_MODEL], m[:, 4 * D_MODEL:5 * D_MODEL])
    ff = jnp.maximum(_dot(h.astype(BF16), w1_ref[...]), 0.0)
    ff = (ff * ff).astype(BF16)
    o_ref[...] = x + m[:, 5 * D_MODEL:6 * D_MODEL] * _dot(ff, w2_ref[...])


def _mlp(x, mods_l, g, w1, w2, mod_map):
    tokens = x.shape[0]
    const = lambda i: (0, 0)
    tile = pl.BlockSpec((TM, D_MODEL), lambda i: (i, 0))
    once = pl.Buffered(1)
    return pl.pallas_call(
        _mlp_kernel,
        grid=(tokens // TM,),
        in_specs=[tile, pl.BlockSpec((1, 1, N_MOD * D_MODEL), mod_map),
                  pl.BlockSpec((1, D_MODEL), const),
                  pl.BlockSpec((D_MODEL, D_FF), const, pipeline_mode=once),
                  pl.BlockSpec((D_FF, D_MODEL), const, pipeline_mode=once)],
        out_specs=tile,
        out_shape=jax.ShapeDtypeStruct((tokens, D_MODEL), F32),
        compiler_params=_params("parallel"),
        name="mlp",
    )(x, mods_l, g, w1.astype(BF16), w2.astype(BF16))


def _final_kernel(x_ref, g_ref, o_ref):
    x = x_ref[...]
    o_ref[...] = x * lax.rsqrt(jnp.mean(x * x, axis=-1, keepdims=True) + EPS) * g_ref[...]


def _final_norm(x, g, first_tile, n_tiles):
    return pl.pallas_call(
        _final_kernel,
        grid=(n_tiles,),
        in_specs=[pl.BlockSpec((TM, D_MODEL), lambda i: (first_tile + i, 0)),
                  pl.BlockSpec((1, D_MODEL), lambda i: (0, 0))],
        out_specs=pl.BlockSpec((TM, D_MODEL), lambda i: (i, 0)),
        out_shape=jax.ShapeDtypeStruct((n_tiles * TM, D_MODEL), F32),
        compiler_params=_params("parallel"),
        name="final_norm",
    )(x, g)


def kernel(x_prompt, x_sample, state_s5_re, state_s5_im, state_gdn, c, c_ctx, norm_mix, norm_mlp, w_ada, b_ada, w_in, conv_qkv, s5_lambda_re, s5_lambda_im, s5_log_dt, s5_b_re, s5_b_im, s5_c_re, s5_c_im, s5_d, gdn_a_log, gdn_dt_bias, gdn_norm, w_out, w_mlp_in, w_mlp_out, norm_final):
    n_ctx, ctx_len, _ = x_prompt.shape
    n_lat, lat_len, _ = x_sample.shape
    depth = w_in.shape[0]
    assert ctx_len % TM == 0 and lat_len % TM == 0 and lat_len % (8 * GRID_W) == 0
    assert n_lat + 1 <= 8
    ctx_tok, lat_tok = n_ctx * ctx_len, n_lat * lat_len
    tokens = ctx_tok + lat_tok
    n_ctx_tiles, tiles_per_lat = ctx_tok // TM, lat_len // TM
    mod_map = _mod_row_map(n_ctx_tiles, tiles_per_lat)

    cond = jnp.zeros((8, D_MODEL), F32).at[0].set(c_ctx).at[1:1 + n_lat].set(c)
    mods = _modulation(cond, w_ada, b_ada)
    x = jnp.concatenate([x_prompt.reshape(ctx_tok, D_MODEL), _embed(x_sample)], axis=0)

    starts = np.arange(tokens // GT) * GT
    in_ctx = starts < ctx_tok
    pos = np.where(in_ctx, starts % ctx_len, (starts - ctx_tok) % lat_len)
    has_prev = pos != 0
    has_next = pos + GT != np.where(in_ctx, ctx_len, lat_len)

    s5_ops = jax.vmap(_s5_operators)(s5_lambda_re, s5_lambda_im, s5_log_dt, s5_b_re, s5_b_im,
                                     s5_c_re, s5_c_im)
    fin_re, fin_im, fin_gdn = [], [], []
    for l in range(depth):
        mods_l = mods[l].reshape(8, 1, N_MOD * D_MODEL)
        u, u_g, z, qkv, zg, ba, bat = _in_proj(x, mods_l, norm_mix[l].reshape(1, D_MODEL), w_in[l],
                                               mod_map)

        h0 = jnp.concatenate([state_s5_re[:, l], state_s5_im[:, l]], axis=-1)
        h0 = h0.transpose(2, 0, 1, 3).reshape(S5_G, n_lat * N_DIR, 2 * S5_N)
        y_g, s5_fin = _s5_mixer(u_g, tuple(o[l] for o in s5_ops), h0, n_ctx, ctx_len, n_lat, lat_len)
        s5_fin = s5_fin.transpose(2, 1, 0, 3)
        fin_re.append(s5_fin[..., :S5_N])
        fin_im.append(s5_fin[..., S5_N:])

        qkv_t = qkv.reshape(tokens // GT, GT, 3 * GDN_W)
        zero_row = jnp.zeros((1, 3 * GDN_W), F32)
        prev_rows = jnp.concatenate([zero_row, qkv_t[:-1, GT - 1]], axis=0)
        next_rows = jnp.concatenate([qkv_t[1:, 0], zero_row], axis=0)
        halo_prev = jnp.where(has_prev[:, None], prev_rows, 0.0)[:, None, :]
        halo_next = jnp.where(has_next[:, None], next_rows, 0.0)[:, None, :]
        pre = _gdn_pre(qkv, halo_prev, halo_next, conv_qkv[l], ba, bat, gdn_a_log[l], gdn_dt_bias[l])
        of_c, ob_c, gdn_fin = _gdn_scan(*pre, None, n_ctx, ctx_len, 0, True)
        of_l, ob_l = _gdn_scan(*pre, state_gdn[:, l], n_lat, lat_len, ctx_tok, False)
        fin_gdn.append(gdn_fin)

        x = _out_proj(x, mods_l, y_g, u, z, s5_d[l].reshape(1, S5_W), (of_c, ob_c), (of_l, ob_l), zg,
                      gdn_norm[l].reshape(1, GDN_DK), w_out[l], mod_map, n_ctx_tiles)
        x = _mlp(x, mods_l, norm_mlp[l].reshape(1, D_MODEL), w_mlp_in[l], w_mlp_out[l], mod_map)

    g_fin = norm_final.reshape(1, D_MODEL)
    y_prompt = _final_norm(x, g_fin, 0, n_ctx_tiles).reshape(n_ctx, ctx_len, D_MODEL)
    y_sample = _final_norm(x, g_fin, n_ctx_tiles, lat_tok // TM).reshape(n_lat, lat_len, D_MODEL)
    return (y_prompt, y_sample, jnp.stack(fin_re, axis=1), jnp.stack(fin_im, axis=1),
            jnp.stack(fin_gdn, axis=1))
```

```python
import functools
import math

import numpy as np
import jax
import jax.numpy as jnp
from jax import lax
from jax.experimental import pallas as pl
from jax.experimental.pallas import tpu as pltpu

F32 = jnp.float32
BF16 = jnp.bfloat16
HIGHEST = lax.Precision.HIGHEST

D_MODEL = 1024
S5_W = 512
S5_P = 16
S5_G = 32
S5_N = 64
GDN_W = 512
GDN_DK = 128
GDN_H = 4
CHUNK = 64
N_DIR = 2
N_MOD = 6
D_FF = 4 * D_MODEL
GRID_W = 64
POS_BASE = 10000.0
EPS = 1e-6
S5_T = 16
S5_TP = S5_T * S5_P
N_GATE = 2 * N_DIR * GDN_H
GATE_PAD = 128
TM = 256
GT = 128
SB = 256
VMEM_LIMIT = 56 * 1024 * 1024


def _dot(a, b):
    return jnp.dot(a, b, preferred_element_type=F32)


def _dot_nt(a, b):
    return lax.dot_general(a, b, (((1,), (1,)), ((), ())), preferred_element_type=F32)


def _dot_tn(a, b):
    return lax.dot_general(a, b, (((0,), (0,)), ((), ())), preferred_element_type=F32)


def _sigmoid(x):
    return 1.0 / (1.0 + jnp.exp(-x))


def _silu(x):
    return x * _sigmoid(x)


def _softplus(x):
    return jnp.maximum(x, 0.0) + jnp.log1p(jnp.exp(-jnp.abs(x)))


def _gelu_tanh(x):
    c = math.sqrt(2.0 / math.pi)
    return 0.5 * x * (1.0 + jnp.tanh(c * (x + 0.044715 * (x * x * x))))


def _params(*sem):
    return pltpu.CompilerParams(dimension_semantics=sem, vmem_limit_bytes=VMEM_LIMIT)


def _mod_kernel(c_ref, w_ref, b_ref, o_ref):
    sc = _silu(c_ref[...])
    o_ref[0] = jnp.dot(sc, w_ref[0], precision=HIGHEST, preferred_element_type=F32) + b_ref[0]


def _modulation(cond, w_ada, b_ada):
    depth = w_ada.shape[0]
    tn = 1536
    n_w = N_MOD * D_MODEL
    return pl.pallas_call(
        _mod_kernel,
        grid=(depth, n_w // tn),
        in_specs=[pl.BlockSpec((8, D_MODEL), lambda l, j: (0, 0)),
                  pl.BlockSpec((1, D_MODEL, tn), lambda l, j: (l, 0, j)),
                  pl.BlockSpec((1, 1, tn), lambda l, j: (l, 0, j))],
        out_specs=pl.BlockSpec((1, 8, tn), lambda l, j: (l, 0, j)),
        out_shape=jax.ShapeDtypeStruct((depth, 8, n_w), F32),
        compiler_params=_params("parallel", "parallel"),
        name="modulation",
    )(cond, w_ada, b_ada.reshape(depth, 1, n_w))


def _embed_kernel(x_ref, tr_ref, tc_ref, o_ref, *, rows_per_tile, tiles_per_seq):
    r0 = (pl.program_id(0) % tiles_per_seq) * rows_per_tile
    half = D_MODEL // 2
    tc = tc_ref[...]
    for k in range(rows_per_tile):
        lo, hi = k * GRID_W, (k + 1) * GRID_W
        o_ref[lo:hi, 0:half] = x_ref[lo:hi, 0:half] + tr_ref[pl.ds(r0 + k, 1), :]
        o_ref[lo:hi, half:D_MODEL] = x_ref[lo:hi, half:D_MODEL] + tc


def _embed(x_sample):
    n_seq, length, _ = x_sample.shape
    rows = length // GRID_W
    quarter = D_MODEL // 4
    omega = 1.0 / (POS_BASE ** (jnp.arange(quarter, dtype=F32) / quarter))
    ang_r = jnp.arange(rows, dtype=F32)[:, None] * omega
    ang_c = jnp.arange(GRID_W, dtype=F32)[:, None] * omega
    tab_r = jnp.concatenate([jnp.sin(ang_r), jnp.cos(ang_r)], axis=-1)
    tab_c = jnp.concatenate([jnp.sin(ang_c), jnp.cos(ang_c)], axis=-1)
    rows_per_tile = 8
    tile = rows_per_tile * GRID_W
    tiles_per_seq = length // tile
    x2 = x_sample.reshape(n_seq * length, D_MODEL)
    return pl.pallas_call(
        functools.partial(_embed_kernel, rows_per_tile=rows_per_tile, tiles_per_seq=tiles_per_seq),
        grid=(n_seq * tiles_per_seq,),
        in_specs=[pl.BlockSpec((tile, D_MODEL), lambda i: (i, 0)),
                  pl.BlockSpec((rows, D_MODEL // 2), lambda i: (0, 0)),
                  pl.BlockSpec((GRID_W, D_MODEL // 2), lambda i: (0, 0))],
        out_specs=pl.BlockSpec((tile, D_MODEL), lambda i: (i, 0)),
        out_shape=jax.ShapeDtypeStruct(x2.shape, F32),
        compiler_params=_params("parallel"),
        name="pos_embed",
    )(x2, tab_r, tab_c)


def _norm_mod(x, g, shift, scale):
    ms = jnp.mean(x * x, axis=-1, keepdims=True)
    h = x * lax.rsqrt(ms + EPS) * g
    return h * (1.0 + scale) + shift


LANES = 128
SLOTS = LANES // S5_P


def _regroup_in(u_refs, ug_ref):
    blocks = TM // S5_T
    slot = lax.broadcasted_iota(jnp.int32, (blocks, LANES), 1) // S5_P
    rot = {}
    for j in range(S5_T):
        for a in range(S5_W // LANES):
            t = u_refs[a][pl.ds(j, blocks, stride=S5_T), :]
            rot[j, a] = t if j % SLOTS == 0 else pltpu.roll(t, S5_P * (j % SLOTS), 1)
    for g in range(S5_G):
        a, s = g // SLOTS, g % SLOTS
        for jt in range(S5_T // SLOTS):
            acc = None
            for sig in range(SLOTS):
                piece = rot[SLOTS * jt + (sig - s) % SLOTS, a]
                acc = piece if acc is None else jnp.where(slot == sig, piece, acc)
            ug_ref[g, :, jt * LANES:(jt + 1) * LANES] = acc.astype(ug_ref.dtype)


def _regroup_out(yg_ref, y_refs):
    blocks = TM // S5_T
    slot = lax.broadcasted_iota(jnp.int32, (blocks, LANES), 1) // S5_P
    for j in range(S5_T):
        jt = j // SLOTS
        for a in range(S5_W // LANES):
            acc = None
            for s in range(SLOTS):
                piece = yg_ref[SLOTS * a + s, :, jt * LANES:(jt + 1) * LANES]
                acc = piece if acc is None else jnp.where(slot == (s + j) % SLOTS, piece, acc)
            if j % SLOTS:
                acc = pltpu.roll(acc, LANES - S5_P * (j % SLOTS), 1)
            y_refs[a][pl.ds(j, blocks, stride=S5_T), :] = acc


def _in_kernel(x_ref, m_ref, g_ref, wu_ref, wz_ref, wqkv_ref, wzg_ref, wba_ref, wbat_ref,
               u_ref, ug_ref, z_ref, qkv_ref, zg_ref, ba_ref, bat_ref, *ut_refs):
    m = m_ref[0]
    h = _norm_mod(x_ref[...], g_ref[...], m[:, 0:D_MODEL], m[:, D_MODEL:2 * D_MODEL]).astype(BF16)
    u = _dot(h, wu_ref[...])
    u_ref[...] = u
    for a, ut_ref in enumerate(ut_refs):
        ut_ref[...] = u[:, a * LANES:(a + 1) * LANES]
    _regroup_in(ut_refs, ug_ref)
    z_ref[...] = _dot(h, wz_ref[...])
    qkv_ref[...] = _dot(h, wqkv_ref[...])
    zg_ref[...] = _dot(h, wzg_ref[...])
    ba_ref[...] = _dot(h, wba_ref[...])
    bat_ref[...] = _dot_nt(wbat_ref[...], h)


def _mod_row_map(n_ctx_tiles, tiles_per_lat):
    def index(i):
        return (jnp.where(i < n_ctx_tiles, 0, 1 + (i - n_ctx_tiles) // tiles_per_lat), 0, 0)
    return index


def _in_proj(x, mods_l, g, w_in, mod_map):
    tokens = x.shape[0]
    o1, o2, o3, o4 = S5_W, 2 * S5_W, 2 * S5_W + 3 * GDN_W, 2 * S5_W + 4 * GDN_W
    wb = w_in.astype(BF16)
    w_ba = jnp.pad(wb[:, o4:], ((0, 0), (0, GATE_PAD - N_GATE)))
    w_bat = wb[:, o4:].T
    const = lambda i: (0, 0)
    tile = lambda n: pl.BlockSpec((TM, n), lambda i: (i, 0))
    full = lambda a: pl.BlockSpec(a.shape, const)
    ws = (wb[:, :o1], wb[:, o1:o2], wb[:, o2:o3], wb[:, o3:o4], w_ba, w_bat)
    return pl.pallas_call(
        _in_kernel,
        grid=(tokens // TM,),
        in_specs=[tile(D_MODEL), pl.BlockSpec((1, 1, N_MOD * D_MODEL), mod_map), full(g)]
                 + [full(w) for w in ws],
        out_specs=[tile(S5_W), pl.BlockSpec((S5_G, TM // S5_T, S5_TP), lambda i: (0, i, 0)),
                   tile(S5_W), tile(3 * GDN_W), tile(GDN_W), tile(GATE_PAD),
                   pl.BlockSpec((N_GATE, TM), lambda i: (0, i))],
        out_shape=[jax.ShapeDtypeStruct((tokens, S5_W), F32),
                   jax.ShapeDtypeStruct((S5_G, tokens // S5_T, S5_TP), BF16),
                   jax.ShapeDtypeStruct((tokens, S5_W), F32),
                   jax.ShapeDtypeStruct((tokens, 3 * GDN_W), F32),
                   jax.ShapeDtypeStruct((tokens, GDN_W), F32),
                   jax.ShapeDtypeStruct((tokens, GATE_PAD), F32),
                   jax.ShapeDtypeStruct((N_GATE, tokens), F32)],
        scratch_shapes=[pltpu.VMEM((TM, LANES), F32)] * (S5_W // LANES),
        compiler_params=_params("parallel"),
        name="in_proj",
    )(x, mods_l, g, *ws)


N_POW = 8


def _zoh(lam_re, lam_im, log_dt):
    lam_re = jnp.minimum(lam_re, -1e-4)
    dt = jnp.exp(log_dt)
    x, th = lam_re * dt, lam_im * dt
    mag = jnp.exp(x)
    num_re, num_im = mag * jnp.cos(th) - 1.0, mag * jnp.sin(th)
    den = lam_re * lam_re + lam_im * lam_im
    return (x, th, (num_re * lam_re + num_im * lam_im) / den, (num_im * lam_re - num_re * lam_im) / den)


def _power(x, th, e):
    mag = jnp.exp(x * e)
    return mag * jnp.cos(th * e), mag * jnp.sin(th * e)


def _shift_lanes(lo, hi, t, left):
    if t == 0:
        return lo, hi
    lane = lax.broadcasted_iota(jnp.int32, lo.shape, 1)
    zero = jnp.zeros_like(lo)
    r = t % LANES
    if not left:
        if t >= LANES:
            return zero, (lo if r == 0 else jnp.where(lane >= r, pltpu.roll(lo, r, 1), 0.0))
        rl, rh = pltpu.roll(lo, r, 1), pltpu.roll(hi, r, 1)
        return jnp.where(lane >= r, rl, 0.0), jnp.where(lane >= r, rh, rl)
    if t >= LANES:
        return (hi if r == 0 else jnp.where(lane < LANES - r, pltpu.roll(hi, LANES - r, 1), 0.0)), zero
    rl, rh = pltpu.roll(lo, LANES - r, 1), pltpu.roll(hi, LANES - r, 1)
    return jnp.where(lane < LANES - r, rl, rh), jnp.where(lane < LANES - r, rh, 0.0)


def _s5_ops_kernel(prow_ref, pcol_ref, bb_ref, ct_ref, k_ref, bf_ref, bbk_ref, cf_ref, cb_ref, ap_ref,
                   knat_ref):
    s16 = (pl.program_id(1) % SLOTS) * S5_P
    prow = prow_ref[0, 0]
    pcol = pcol_ref[0, 0]
    lane = lax.broadcasted_iota(jnp.int32, (1, 2 * S5_N), 1)
    sign_k = jnp.where(lane < S5_N, 1.0, -1.0)
    rep = jnp.where(lax.broadcasted_iota(jnp.int32, (S5_P, S5_TP), 1) % S5_P
                    == lax.broadcasted_iota(jnp.int32, (S5_P, S5_TP), 0), 1.0, 0.0)
    tok_l = (lax.broadcasted_iota(jnp.int32, (1, S5_TP), 1) // S5_P).astype(F32)
    tok_r = (lax.broadcasted_iota(jnp.int32, (S5_TP, 1), 0) // S5_P).astype(F32)
    kall = []
    for d in range(N_DIR):
        x_r, th_r, fr_r, fi_r = _zoh(prow[d:d + 1], prow[2 + d:3 + d], prow[4 + d:5 + d])
        x_c, th_c, _, _ = _zoh(pcol[:, d:d + 1], pcol[:, 2 + d:3 + d], pcol[:, 4 + d:5 + d])
        bpack, bswap = bb_ref[0, 0, d, 0], bb_ref[0, 0, d, 1]
        bbar = fr_r * bpack + fi_r * bswap
        bbar_sw = fr_r * bswap - fi_r * bpack
        pr, pi = _power(x_r, th_r, (S5_T - 1) - tok_r if d == 0 else tok_r)
        b_nat = pr * jnp.concatenate([bbar] * S5_T, axis=0) + pi * jnp.concatenate([bbar_sw] * S5_T, axis=0)
        b_out = bf_ref if d == 0 else bbk_ref
        for rt in range(S5_TP // LANES):
            b_out[0, 0, rt * LANES:(rt + 1) * LANES, :] = pltpu.roll(
                b_nat[rt * LANES:(rt + 1) * LANES], s16, 0).astype(b_out.dtype)
        c_re = jnp.dot(ct_ref[0, 0, d, 0], rep, precision=HIGHEST, preferred_element_type=F32)
        c_im = jnp.dot(ct_ref[0, 0, d, 1], rep, precision=HIGHEST, preferred_element_type=F32)
        pr, pi = _power(x_c, th_c, tok_l + 1.0 if d == 0 else S5_T - tok_l)
        c_nat = jnp.concatenate([c_re * pr - c_im * pi, -(c_re * pi + c_im * pr)], axis=0)
        c_out = cf_ref if d == 0 else cb_ref
        for ct in range(S5_TP // LANES):
            c_out[0, 0, :, ct * LANES:(ct + 1) * LANES] = pltpu.roll(
                c_nat[:, ct * LANES:(ct + 1) * LANES], s16, 1).astype(c_out.dtype)
        pr, pi = _power(x_c, th_c, tok_l if d == 0 else (S5_T - 1) - tok_l)
        ca = jnp.concatenate([c_re * pr - c_im * pi, c_re * pi + c_im * pr], axis=0)
        kall.append(jnp.dot(bbar * sign_k, ca, precision=HIGHEST, preferred_element_type=F32))
        row = lax.broadcasted_iota(jnp.int32, (2 * N_POW, 1), 0)
        expo = (S5_T * jnp.left_shift(1, row // 2)).astype(F32)
        pr, pi = _power(x_r, th_r, expo)
        ap_ref[0, 0, d * 2 * N_POW:(d + 1) * 2 * N_POW, :] = jnp.where(row % 2 == 0, pr, -pi * sign_k)

    kf_lo, kf_hi = kall[0][:, :LANES], kall[0][:, LANES:]
    kb_lo, kb_hi = kall[1][:, :LANES], kall[1][:, LANES:]
    for j in range(S5_T):
        f_lo, f_hi = _shift_lanes(kf_lo, kf_hi, S5_P * j, False)
        b_lo, b_hi = _shift_lanes(kb_lo, kb_hi, S5_P * (S5_T - 1 - j), True)
        knat_ref[j * S5_P:(j + 1) * S5_P, 0:LANES] = f_lo + b_lo
        knat_ref[j * S5_P:(j + 1) * S5_P, LANES:2 * LANES] = f_hi + b_hi
    for rt in range(S5_TP // LANES):
        for ct in range(S5_TP // LANES):
            tile = knat_ref[rt * LANES:(rt + 1) * LANES, ct * LANES:(ct + 1) * LANES]
            k_ref[0, 0, rt * LANES:(rt + 1) * LANES, ct * LANES:(ct + 1) * LANES] = pltpu.roll(
                pltpu.roll(tile, s16, 0), s16, 1).astype(k_ref.dtype)


def _s5_operators(lam_re, lam_im, log_dt, b_re, b_im, c_re, c_im):
    depth = lam_re.shape[0]
    dup = lambda a: jnp.concatenate([a, a], axis=-1)
    ldt = jnp.broadcast_to(log_dt[..., None], lam_re.shape)
    g_major = lambda a: jnp.moveaxis(a, 1, 2)
    prow = jnp.concatenate([dup(g_major(a)) for a in (lam_re, lam_im, ldt)], axis=2)
    pcol = jnp.concatenate([g_major(a) for a in (lam_re, lam_im, ldt)], axis=2)
    pcol = jnp.pad(jnp.swapaxes(pcol, 2, 3), ((0, 0), (0, 0), (0, 0), (0, 2)))
    bt_re, bt_im = (jnp.swapaxes(g_major(a), 3, 4) for a in (b_re, b_im))
    bb = jnp.stack([jnp.concatenate([bt_re, bt_im], axis=-1),
                    jnp.concatenate([-bt_im, bt_re], axis=-1)], axis=3)
    ct = jnp.stack([jnp.swapaxes(g_major(a), 3, 4) for a in (c_re, c_im)], axis=3)
    per = lambda a: pl.BlockSpec((1, 1) + a.shape[2:], lambda l, g: (l, g) + (0,) * (a.ndim - 2))
    mat = lambda r, c: pl.BlockSpec((1, 1, r, c), lambda l, g: (l, g, 0, 0))
    shape = lambda r, c, dt: jax.ShapeDtypeStruct((depth, S5_G, r, c), dt)
    return pl.pallas_call(
        _s5_ops_kernel,
        grid=(depth, S5_G),
        in_specs=[per(prow), per(pcol), per(bb), per(ct)],
        out_specs=[mat(S5_TP, S5_TP), mat(S5_TP, 2 * S5_N), mat(S5_TP, 2 * S5_N),
                   mat(2 * S5_N, S5_TP), mat(2 * S5_N, S5_TP), mat(N_DIR * 2 * N_POW, 2 * S5_N)],
        out_shape=[shape(S5_TP, S5_TP, BF16), shape(S5_TP, 2 * S5_N, BF16), shape(S5_TP, 2 * S5_N, BF16),
                   shape(2 * S5_N, S5_TP, BF16), shape(2 * S5_N, S5_TP, BF16),
                   shape(N_DIR * 2 * N_POW, 2 * S5_N, F32)],
        scratch_shapes=[pltpu.VMEM((S5_TP, S5_TP), F32)],
        compiler_params=_params("parallel", "parallel"),
        name="s5_operators",
    )(prow, pcol, bb, ct)


def _lane_swap(x):
    return pltpu.roll(x, S5_N, 1)


def _cmul(a1, a2, h):
    return a1 * h + a2 * _lane_swap(h)


def _block_scan(s, seg, a_pow, d, reverse):
    rows = s.shape[0]
    ridx = lax.broadcasted_iota(jnp.int32, s.shape, 0) % seg
    h = s
    k, step = 0, 1
    while step < seg:
        a1 = a_pow[d * 16 + 2 * k:d * 16 + 2 * k + 1]
        a2 = a_pow[d * 16 + 2 * k + 1:d * 16 + 2 * k + 2]
        if reverse:
            sh = jnp.where(ridx < seg - step, pltpu.roll(h, rows - step, 0), 0.0)
        else:
            sh = jnp.where(ridx >= step, pltpu.roll(h, step, 0), 0.0)
        h = h + _cmul(a1, a2, sh)
        k += 1
        step *= 2
    return h


def _s5_kernel(u_ref, k_ref, bf_ref, bb_ref, cf_ref, cb_ref, ap_ref, h0_ref,
               y_ref, fin_ref, hf_ref, hb_ref, *, ctx_rows, ctx_seg, lat_seg, n_ctx, n_lat):
    u = u_ref[0]
    a_pow = ap_ref[0]
    s_f = _dot(u, bf_ref[0])
    s_b = _dot(u, bb_ref[0])
    lat_rows = u.shape[0] - ctx_rows

    lrow = lax.broadcasted_iota(jnp.int32, (lat_rows, 2 * S5_N), 0)
    lmod = lrow % lat_seg
    lseq = lrow // lat_seg
    h0 = h0_ref[0]
    h0_f = jnp.zeros((lat_rows, 2 * S5_N), F32)
    h0_b = jnp.zeros((lat_rows, 2 * S5_N), F32)
    for s in range(n_lat):
        h0_f = jnp.where(lseq == s, h0[2 * s:2 * s + 1], h0_f)
        h0_b = jnp.where(lseq == s, h0[2 * s + 1:2 * s + 2], h0_b)
    first, last = lmod == 0, lmod == lat_seg - 1
    sl_f = s_f[ctx_rows:] + jnp.where(first, _cmul(a_pow[0:1], a_pow[1:2], h0_f), 0.0)
    sl_b = s_b[ctx_rows:] + jnp.where(last, _cmul(a_pow[16:17], a_pow[17:18], h0_b), 0.0)

    hc_f = _block_scan(s_f[:ctx_rows], ctx_seg, a_pow, 0, False)
    hc_b = _block_scan(s_b[:ctx_rows], ctx_seg, a_pow, 1, True)
    hl_f = _block_scan(sl_f, lat_seg, a_pow, 0, False)
    hl_b = _block_scan(sl_b, lat_seg, a_pow, 1, True)

    cmod = lax.broadcasted_iota(jnp.int32, (ctx_rows, 2 * S5_N), 0) % ctx_seg
    pc_f = jnp.where(cmod == 0, 0.0, pltpu.roll(hc_f, 1, 0))
    pc_b = jnp.where(cmod == ctx_seg - 1, 0.0, pltpu.roll(hc_b, ctx_rows - 1, 0))
    pl_f = jnp.where(first, h0_f, pltpu.roll(hl_f, 1, 0))
    pl_b = jnp.where(last, h0_b, pltpu.roll(hl_b, lat_rows - 1, 0))
    prev_f = jnp.concatenate([pc_f, pl_f], axis=0).astype(BF16)
    prev_b = jnp.concatenate([pc_b, pl_b], axis=0).astype(BF16)
    y_ref[0] = _dot(u, k_ref[0]) + _dot(prev_f, cf_ref[0]) + _dot(prev_b, cb_ref[0])

    hf_ref[...] = hc_f
    hb_ref[...] = hc_b
    fin_ref[0, 0] = hf_ref[pl.ds(ctx_seg - 1, n_ctx, stride=ctx_seg), :]
    fin_ref[0, 1] = hb_ref[pl.ds(0, n_ctx, stride=ctx_seg), :]


def _s5_mixer(u_g, ops, h0, n_ctx, ctx_len, n_lat, lat_len):
    k_mat, b_f, b_b, c_f, c_b, a_pow = ops
    rows = u_g.shape[1]
    ctx_seg, lat_seg = ctx_len // S5_T, lat_len // S5_T
    ctx_rows = n_ctx * ctx_seg
    per_g = lambda a: pl.BlockSpec((1,) + a.shape[1:], lambda g: (g,) + (0,) * (a.ndim - 1))
    kern = functools.partial(_s5_kernel, ctx_rows=ctx_rows, ctx_seg=ctx_seg, lat_seg=lat_seg,
                             n_ctx=n_ctx, n_lat=n_lat)
    return pl.pallas_call(
        kern,
        grid=(S5_G,),
        in_specs=[per_g(a) for a in (u_g, k_mat, b_f, b_b, c_f, c_b, a_pow, h0)],
        out_specs=[pl.BlockSpec((1, rows, S5_TP), lambda g: (g, 0, 0)),
                   pl.BlockSpec((1, N_DIR, n_ctx, 2 * S5_N), lambda g: (g, 0, 0, 0))],
        out_shape=[jax.ShapeDtypeStruct((S5_G, rows, S5_TP), F32),
                   jax.ShapeDtypeStruct((S5_G, N_DIR, n_ctx, 2 * S5_N), F32)],
        scratch_shapes=[pltpu.VMEM((ctx_rows, 2 * S5_N), F32)] * N_DIR,
        compiler_params=_params("parallel"),
        name="s5_mixer",
    )(u_g, k_mat, b_f, b_b, c_f, c_b, a_pow, h0)


def _gdn_pre_kernel(x_ref, hp_ref, hn_ref, cw_ref, ba_ref, bat_ref, al_ref, dl_ref, ar_ref, dr_ref,
                    u_ref, wq_ref, kd_ref, a_ref, eg_ref,
                    q_ref, k_ref, v_ref, col_ref, row_ref, m_ref, acc_ref):
    x = x_ref[...]
    rid = lax.broadcasted_iota(jnp.int32, x.shape, 0)
    x_prev = jnp.where(rid == 0, hp_ref[0], pltpu.roll(x, 1, 0))
    x_next = jnp.where(rid == GT - 1, hn_ref[0], pltpu.roll(x, GT - 1, 0))
    cw = cw_ref[...]
    y = _silu(x_prev * cw[0:1] + x * cw[1:2] + x_next * cw[2:3])
    for h in range(GDN_H):
        lo, hi = h * GDN_DK, (h + 1) * GDN_DK
        qh = y[:, lo:hi]
        kh = y[:, GDN_W + lo:GDN_W + hi]
        q_ref[:, lo:hi] = qh * (lax.rsqrt(jnp.sum(qh * qh, axis=-1, keepdims=True) + EPS)
                                * (GDN_DK ** -0.5))
        k_ref[:, lo:hi] = kh * lax.rsqrt(jnp.sum(kh * kh, axis=-1, keepdims=True) + EPS)
    v_ref[...] = y[:, 2 * GDN_W:]

    half = N_GATE // 2
    ba = ba_ref[...]
    lane = lax.broadcasted_iota(jnp.int32, ba.shape, 1)
    gate = jnp.where(lane < half, _sigmoid(ba), -jnp.exp(al_ref[...]) * _softplus(ba + dl_ref[...]))
    ri = lax.broadcasted_iota(jnp.int32, (GT, GT), 0)
    ci = lax.broadcasted_iota(jnp.int32, (GT, GT), 1)
    same = (ri // CHUNK) == (ci // CHUNK)
    lower = jnp.where(same & (ci <= ri), 1.0, 0.0)
    upper = jnp.where(same & (ci >= ri), 1.0, 0.0)
    cum_f = jnp.dot(lower, gate, precision=HIGHEST, preferred_element_type=F32)
    cum_b = jnp.dot(upper, gate, precision=HIGHEST, preferred_element_type=F32)
    fwd_lane = (lane >= half) & (lane < half + GDN_H)
    bwd_lane = (lane >= half + GDN_H) & (lane < N_GATE)
    col_ref[...] = jnp.where(fwd_lane, cum_f, jnp.where(bwd_lane, cum_b, gate))
    bat = bat_ref[...]
    g_row = -jnp.exp(ar_ref[...]) * _softplus(bat + dr_ref[...])
    row_f = jnp.dot(g_row, upper, precision=HIGHEST, preferred_element_type=F32)
    row_b = jnp.dot(g_row, lower, precision=HIGHEST, preferred_element_type=F32)
    srow = lax.broadcasted_iota(jnp.int32, bat.shape, 0)
    row_ref[...] = jnp.where(srow < half + GDN_H, row_f, row_b)

    ii = lax.broadcasted_iota(jnp.int32, (CHUNK, CHUNK), 0)
    jj = lax.broadcasted_iota(jnp.int32, (CHUNK, CHUNK), 1)
    units = [(c, d, h) for c in range(GT // CHUNK) for d in range(N_DIR) for h in range(GDN_H)]

    def gates(c, d, h):
        rows = slice(c * CHUNK, (c + 1) * CHUNK)
        lane_ = d * GDN_H + h
        beta = col_ref[rows, lane_:lane_ + 1]
        gc = col_ref[rows, half + lane_:half + lane_ + 1]
        return rows, beta, gc

    for n, (c, d, h) in enumerate(units):
        rows, beta, gc = gates(c, d, h)
        cols = slice(h * GDN_DK, (h + 1) * GDN_DK)
        gr = row_ref[half + d * GDN_H + h:half + d * GDN_H + h + 1, rows]
        incl = (jj >= ii) if d == 1 else (jj <= ii)
        strict = (jj > ii) if d == 1 else (jj < ii)
        decay = jnp.where(incl, jnp.exp(jnp.where(incl, gc - gr, 0.0)), 0.0)
        k16 = k_ref[rows, cols].astype(BF16)
        kb16 = (k_ref[rows, cols] * beta).astype(BF16)
        m = jnp.where(strict, _dot_nt(kb16, k16) * decay, 0.0)
        a_ref[d, rows, h * CHUNK:(h + 1) * CHUNK] = (
            _dot_nt(q_ref[rows, cols].astype(BF16), k16) * decay).astype(BF16)
        m_ref[n] = m
        acc_ref[n] = jnp.where((ii // 2) == (jj // 2), -m, 0.0)

    b = 2
    while b < CHUNK:
        pair = ((ii // (2 * b)) == (jj // (2 * b))) & ((ii // b) != (jj // b))
        for n in range(len(units)):
            cpart = jnp.where(pair, m_ref[n], 0.0)
            acc = acc_ref[n]
            a16 = acc.astype(BF16)
            y = cpart + _dot(a16, cpart.astype(BF16))
            acc_ref[n] = acc - y - _dot(y.astype(BF16), a16)
        b *= 2

    for n, (c, d, h) in enumerate(units):
        rows, beta, gc = gates(c, d, h)
        cols = slice(h * GDN_DK, (h + 1) * GDN_DK)
        kh = k_ref[rows, cols]
        egc = jnp.exp(gc)
        kb = kh * beta
        rhs = jnp.concatenate([v_ref[rows, cols] * beta, kb * egc], axis=1)
        sol = rhs + _dot(acc_ref[n].astype(BF16), rhs.astype(BF16))
        u_ref[d, rows, cols] = sol[:, :GDN_DK]
        wq_ref[d, c, 0:CHUNK, cols] = sol[:, GDN_DK:].astype(BF16)
        wq_ref[d, c, CHUNK:2 * CHUNK, cols] = (q_ref[rows, cols] * egc).astype(BF16)
        g_last = gc[0:1] if d == 1 else gc[CHUNK - 1:CHUNK]
        kd_ref[d, rows, cols] = (kh * jnp.exp(g_last - gc)).astype(BF16)
        eg_ref[c, d * GDN_H + h:d * GDN_H + h + 1, :] = jnp.broadcast_to(jnp.exp(g_last), (1, GDN_DK))


def _gdn_pre(qkv, halo_prev, halo_next, conv_w, ba, bat, a_log, dt_bias):
    tokens = qkv.shape[0]
    n_tiles = tokens // GT
    cpt = GT // CHUNK
    half = N_GATE // 2
    al = a_log.reshape(-1)
    db = dt_bias.reshape(-1)
    al_l = jnp.zeros((1, GATE_PAD), F32).at[0, half:N_GATE].set(al)
    db_l = jnp.zeros((1, GATE_PAD), F32).at[0, half:N_GATE].set(db)
    al_r = jnp.zeros((N_GATE, 1), F32).at[half:, 0].set(al)
    db_r = jnp.zeros((N_GATE, 1), F32).at[half:, 0].set(db)
    const = lambda i: (0, 0)
    tile = lambda n: pl.BlockSpec((GT, n), lambda i: (i, 0))
    halo = pl.BlockSpec((1, 1, 3 * GDN_W), lambda i: (i, 0, 0))
    n_chunks = tokens // CHUNK
    n_units = cpt * N_DIR * GDN_H
    return pl.pallas_call(
        _gdn_pre_kernel,
        grid=(n_tiles,),
        in_specs=[tile(3 * GDN_W), halo, halo, pl.BlockSpec(conv_w.shape, const),
                  tile(GATE_PAD), pl.BlockSpec((N_GATE, GT), lambda i: (0, i)),
                  pl.BlockSpec((1, GATE_PAD), const), pl.BlockSpec((1, GATE_PAD), const),
                  pl.BlockSpec((N_GATE, 1), const), pl.BlockSpec((N_GATE, 1), const)],
        out_specs=[pl.BlockSpec((N_DIR, GT, GDN_W), lambda i: (0, i, 0)),
                   pl.BlockSpec((N_DIR, cpt, 2 * CHUNK, GDN_W), lambda i: (0, i, 0, 0)),
                   pl.BlockSpec((N_DIR, GT, GDN_W), lambda i: (0, i, 0)),
                   pl.BlockSpec((N_DIR, GT, GDN_H * CHUNK), lambda i: (0, i, 0)),
                   pl.BlockSpec((cpt, N_DIR * GDN_H, GDN_DK), lambda i: (i, 0, 0))],
        out_shape=[jax.ShapeDtypeStruct((N_DIR, tokens, GDN_W), F32),
                   jax.ShapeDtypeStruct((N_DIR, n_chunks, 2 * CHUNK, GDN_W), BF16),
                   jax.ShapeDtypeStruct((N_DIR, tokens, GDN_W), BF16),
                   jax.ShapeDtypeStruct((N_DIR, tokens, GDN_H * CHUNK), BF16),
                   jax.ShapeDtypeStruct((n_chunks, N_DIR * GDN_H, GDN_DK), F32)],
        scratch_shapes=[pltpu.VMEM((GT, GDN_W), F32)] * 3
                       + [pltpu.VMEM((GT, GATE_PAD), F32), pltpu.VMEM((N_GATE, GT), F32),
                          pltpu.VMEM((n_units, CHUNK, CHUNK), F32),
                          pltpu.VMEM((n_units, CHUNK, CHUNK), F32)],
        compiler_params=_params("parallel"),
        name="gdn_pre",
    )(qkv, halo_prev, halo_next, conv_w, ba, bat, al_l, db_l, al_r, db_r)


def _gdn_scan_kernel(*refs, has_s0, has_fin):
    uf, wqf, kdf, af, egf, ub, wqb, kdb, ab, egb = refs[:10]
    pos = 10
    s0_ref = None
    if has_s0:
        s0_ref = refs[pos]
        pos += 1
    of_ref, ob_ref = refs[pos], refs[pos + 1]
    pos += 2
    fin_ref = None
    if has_fin:
        fin_ref = refs[pos]
        pos += 1
    s_ref = refs[pos]
    j = pl.program_id(1)

    @pl.when(j == 0)
    def _():
        if has_s0:
            s_ref[...] = s0_ref[0]
        else:
            s_ref[...] = jnp.zeros(s_ref.shape, F32)

    cpb = SB // CHUNK
    dirs = ((uf, wqf, kdf, af, egf, of_ref), (ub, wqb, kdb, ab, egb, ob_ref))
    for step in range(cpb):
        work = []
        for d, (u_r, wq_r, kd_r, a_r, eg_r, o_r) in enumerate(dirs):
            c = step if d == 0 else cpb - 1 - step
            rows = slice(c * CHUNK, (c + 1) * CHUNK)
            for h in range(GDN_H):
                work.append((d, h, c, rows, slice(h * GDN_DK, (h + 1) * GDN_DK), u_r, wq_r, kd_r, a_r,
                             eg_r, o_r))
        ps = [_dot(wq_r[0, c, :, cols], s_ref[d, h].astype(BF16))
              for (d, h, c, rows, cols, u_r, wq_r, kd_r, a_r, eg_r, o_r) in work]
        v16 = [(u_r[0, rows, cols] - p[:CHUNK]).astype(BF16)
               for p, (d, h, c, rows, cols, u_r, wq_r, kd_r, a_r, eg_r, o_r) in zip(ps, work)]
        for p, v, (d, h, c, rows, cols, u_r, wq_r, kd_r, a_r, eg_r, o_r) in zip(ps, v16, work):
            o_r[rows, cols] = p[CHUNK:] + _dot(a_r[0, rows, h * CHUNK:(h + 1) * CHUNK], v)
        for v, (d, h, c, rows, cols, u_r, wq_r, kd_r, a_r, eg_r, o_r) in zip(v16, work):
            eg = eg_r[c, d * GDN_H + h:d * GDN_H + h + 1, :]
            s_ref[d, h] = s_ref[d, h] * eg + _dot_tn(kd_r[0, rows, cols], v)

    if has_fin:
        @pl.when(j == pl.num_programs(1) - 1)
        def _():
            fin_ref[0] = s_ref[...]


def _gdn_scan(u, wq, kd, a, eg, s0, n_seq, length, base_tok, want_fin):
    n_b = length // SB
    cpb = SB // CHUNK
    base = base_tok // SB
    in_specs, args = [], []
    for d in range(N_DIR):
        blk = (lambda s, j: base + s * n_b + j) if d == 0 else (lambda s, j: base + s * n_b + (n_b - 1 - j))
        in_specs += [pl.BlockSpec((1, SB, GDN_W), lambda s, j, d=d, blk=blk: (d, blk(s, j), 0)),
                     pl.BlockSpec((1, cpb, 2 * CHUNK, GDN_W), lambda s, j, d=d, blk=blk: (d, blk(s, j), 0, 0)),
                     pl.BlockSpec((1, SB, GDN_W), lambda s, j, d=d, blk=blk: (d, blk(s, j), 0)),
                     pl.BlockSpec((1, SB, GDN_H * CHUNK), lambda s, j, d=d, blk=blk: (d, blk(s, j), 0)),
                     pl.BlockSpec((cpb, N_DIR * GDN_H, GDN_DK), lambda s, j, blk=blk: (blk(s, j), 0, 0))]
        args += [u, wq, kd, a, eg]
    state_spec = pl.BlockSpec((1, N_DIR, GDN_H, GDN_DK, GDN_DK), lambda s, j: (s, 0, 0, 0, 0))
    if s0 is not None:
        in_specs.append(state_spec)
        args.append(s0)
    tok = lambda m: pl.BlockSpec((SB, GDN_W), m)
    out_specs = [tok(lambda s, j: (s * n_b + j, 0)), tok(lambda s, j: (s * n_b + (n_b - 1 - j), 0))]
    out_shape = [jax.ShapeDtypeStruct((n_seq * length, GDN_W), F32)] * 2
    if want_fin:
        out_specs.append(state_spec)
        out_shape.append(jax.ShapeDtypeStruct((n_seq, N_DIR, GDN_H, GDN_DK, GDN_DK), F32))
    return pl.pallas_call(
        functools.partial(_gdn_scan_kernel, has_s0=s0 is not None, has_fin=want_fin),
        grid=(n_seq, n_b),
        in_specs=in_specs,
        out_specs=out_specs,
        out_shape=out_shape,
        scratch_shapes=[pltpu.VMEM((N_DIR, GDN_H, GDN_DK, GDN_DK), F32)],
        compiler_params=_params("parallel", "arbitrary"),
        name="gdn_scan",
    )(*args)


def _out_kernel(x_ref, m_ref, yg_ref, u_ref, z_ref, dsk_ref, ofc_ref, obc_ref, ofl_ref, obl_ref, zg_ref,
                gn_ref, w_ref, o_ref, *y5_refs, n_ctx_tiles):
    m = m_ref[0]
    _regroup_out(yg_ref, y5_refs)
    y5 = jnp.concatenate([r[...] for r in y5_refs], axis=1)
    s5 = _gelu_tanh(y5 + u_ref[...] * dsk_ref[...]) * _sigmoid(z_ref[...])
    acc = _dot(s5.astype(BF16), w_ref[0:S5_W, :])
    in_ctx = pl.program_id(0) < n_ctx_tiles
    o = jnp.where(in_ctx, ofc_ref[...] + obc_ref[...], ofl_ref[...] + obl_ref[...])
    zg = zg_ref[...]
    gn = gn_ref[...]
    for h in range(GDN_H):
        lo, hi = h * GDN_DK, (h + 1) * GDN_DK
        oh = o[:, lo:hi]
        nh = oh * lax.rsqrt(jnp.mean(oh * oh, axis=-1, keepdims=True) + EPS) * gn
        gh = (nh * _silu(zg[:, lo:hi])).astype(BF16)
        acc = acc + _dot(gh, w_ref[S5_W + lo:S5_W + hi, :])
    o_ref[...] = x_ref[...] + m[:, 2 * D_MODEL:3 * D_MODEL] * acc


def _out_proj(x, mods_l, y_g, u, z, d_skip, o_ctx, o_lat, zg, gdn_norm, w_out, mod_map, n_ctx_tiles):
    tokens = x.shape[0]
    n_lat_tiles = tokens // TM - n_ctx_tiles
    const = lambda i: (0, 0)
    tile = lambda n: pl.BlockSpec((TM, n), lambda i: (i, 0))
    ctx_tile = pl.BlockSpec((TM, GDN_W), lambda i: (jnp.minimum(i, n_ctx_tiles - 1), 0))
    lat_tile = pl.BlockSpec((TM, GDN_W), lambda i: (jnp.maximum(i - n_ctx_tiles, 0), 0))
    assert n_lat_tiles > 0
    return pl.pallas_call(
        functools.partial(_out_kernel, n_ctx_tiles=n_ctx_tiles),
        grid=(tokens // TM,),
        in_specs=[tile(D_MODEL), pl.BlockSpec((1, 1, N_MOD * D_MODEL), mod_map),
                  pl.BlockSpec((S5_G, TM // S5_T, S5_TP), lambda i: (0, i, 0)),
                  tile(S5_W), tile(S5_W), pl.BlockSpec((1, S5_W), const),
                  ctx_tile, ctx_tile, lat_tile, lat_tile,
                  tile(GDN_W), pl.BlockSpec((1, GDN_DK), const),
                  pl.BlockSpec((D_MODEL, D_MODEL), const)],
        out_specs=tile(D_MODEL),
        out_shape=jax.ShapeDtypeStruct((tokens, D_MODEL), F32),
        scratch_shapes=[pltpu.VMEM((TM, LANES), F32)] * (S5_W // LANES),
        compiler_params=_params("parallel"),
        name="out_proj",
    )(x, mods_l, y_g, u, z, d_skip, *o_ctx, *o_lat, zg, gdn_norm, w_out.astype(BF16))


def _mlp_kernel(x_ref, m_ref, g_ref, w1_ref, w2_ref, o_ref):
    m = m_ref[0]
    x = x_ref[...]
    h = _norm_mod(x, g_ref[...], m[:, 3 * D_MODEL:4 * D_MODEL], m[:, 4 * D_MODEL:5 * D_MODEL])
    ff = jnp.maximum(_dot(h.astype(BF16), w1_ref[...]), 0.0)
    ff = (ff * ff).astype(BF16)
    o_ref[...] = x + m[:, 5 * D_MODEL:6 * D_MODEL] * _dot(ff, w2_ref[...])


def _mlp(x, mods_l, g, w1, w2, mod_map):
    tokens = x.shape[0]
    const = lambda i: (0, 0)
    tile = pl.BlockSpec((TM, D_MODEL), lambda i: (i, 0))
    once = pl.Buffered(1)
    return pl.pallas_call(
        _mlp_kernel,
        grid=(tokens // TM,),
        in_specs=[tile, pl.BlockSpec((1, 1, N_MOD * D_MODEL), mod_map),
                  pl.BlockSpec((1, D_MODEL), const),
                  pl.BlockSpec((D_MODEL, D_FF), const, pipeline_mode=once),
                  pl.BlockSpec((D_FF, D_MODEL), const, pipeline_mode=once)],
        out_specs=tile,
        out_shape=jax.ShapeDtypeStruct((tokens, D_MODEL), F32),
        compiler_params=_params("parallel"),
        name="mlp",
    )(x, mods_l, g, w1.astype(BF16), w2.astype(BF16))


def _final_kernel(x_ref, g_ref, o_ref):
    x = x_ref[...]
    o_ref[...] = x * lax.rsqrt(jnp.mean(x * x, axis=-1, keepdims=True) + EPS) * g_ref[...]


def _final_norm(x, g, first_tile, n_tiles):
    return pl.pallas_call(
        _final_kernel,
        grid=(n_tiles,),
        in_specs=[pl.BlockSpec((TM, D_MODEL), lambda i: (first_tile + i, 0)),
                  pl.BlockSpec((1, D_MODEL), lambda i: (0, 0))],
        out_specs=pl.BlockSpec((TM, D_MODEL), lambda i: (i, 0)),
        out_shape=jax.ShapeDtypeStruct((n_tiles * TM, D_MODEL), F32),
        compiler_params=_params("parallel"),
        name="final_norm",
    )(x, g)


def kernel(x_prompt, x_sample, state_s5_re, state_s5_im, state_gdn, c, c_ctx, norm_mix, norm_mlp, w_ada, b_ada, w_in, conv_qkv, s5_lambda_re, s5_lambda_im, s5_log_dt, s5_b_re, s5_b_im, s5_c_re, s5_c_im, s5_d, gdn_a_log, gdn_dt_bias, gdn_norm, w_out, w_mlp_in, w_mlp_out, norm_final):
    n_ctx, ctx_len, _ = x_prompt.shape
    n_lat, lat_len, _ = x_sample.shape
    depth = w_in.shape[0]
    assert ctx_len % TM == 0 and lat_len % TM == 0 and lat_len % (8 * GRID_W) == 0
    assert n_lat + 1 <= 8
    ctx_tok, lat_tok = n_ctx * ctx_len, n_lat * lat_len
    tokens = ctx_tok + lat_tok
    n_ctx_tiles, tiles_per_lat = ctx_tok // TM, lat_len // TM
    mod_map = _mod_row_map(n_ctx_tiles, tiles_per_lat)

    cond = jnp.zeros((8, D_MODEL), F32).at[0].set(c_ctx).at[1:1 + n_lat].set(c)
    mods = _modulation(cond, w_ada, b_ada)
    x = jnp.concatenate([x_prompt.reshape(ctx_tok, D_MODEL), _embed(x_sample)], axis=0)

    starts = np.arange(tokens // GT) * GT
    in_ctx = starts < ctx_tok
    pos = np.where(in_ctx, starts % ctx_len, (starts - ctx_tok) % lat_len)
    has_prev = pos != 0
    has_next = pos + GT != np.where(in_ctx, ctx_len, lat_len)

    s5_ops = _s5_operators(s5_lambda_re, s5_lambda_im, s5_log_dt, s5_b_re, s5_b_im, s5_c_re, s5_c_im)
    fin_re, fin_im, fin_gdn = [], [], []
    for l in range(depth):
        mods_l = mods[l].reshape(8, 1, N_MOD * D_MODEL)
        u, u_g, z, qkv, zg, ba, bat = _in_proj(x, mods_l, norm_mix[l].reshape(1, D_MODEL), w_in[l],
                                               mod_map)

        h0 = jnp.concatenate([state_s5_re[:, l], state_s5_im[:, l]], axis=-1)
        h0 = h0.transpose(2, 0, 1, 3).reshape(S5_G, n_lat * N_DIR, 2 * S5_N)
        y_g, s5_fin = _s5_mixer(u_g, tuple(o[l] for o in s5_ops), h0, n_ctx, ctx_len, n_lat, lat_len)
        s5_fin = s5_fin.transpose(2, 1, 0, 3)
        fin_re.append(s5_fin[..., :S5_N])
        fin_im.append(s5_fin[..., S5_N:])

        qkv_t = qkv.reshape(tokens // GT, GT, 3 * GDN_W)
        zero_row = jnp.zeros((1, 3 * GDN_W), F32)
        prev_rows = jnp.concatenate([zero_row, qkv_t[:-1, GT - 1]], axis=0)
        next_rows = jnp.concatenate([qkv_t[1:, 0], zero_row], axis=0)
        halo_prev = jnp.where(has_prev[:, None], prev_rows, 0.0)[:, None, :]
        halo_next = jnp.where(has_next[:, None], next_rows, 0.0)[:, None, :]
        pre = _gdn_pre(qkv, halo_prev, halo_next, conv_qkv[l], ba, bat, gdn_a_log[l], gdn_dt_bias[l])
        of_c, ob_c, gdn_fin = _gdn_scan(*pre, None, n_ctx, ctx_len, 0, True)
        of_l, ob_l = _gdn_scan(*pre, state_gdn[:, l], n_lat, lat_len, ctx_tok, False)
        fin_gdn.append(gdn_fin)

        x = _out_proj(x, mods_l, y_g, u, z, s5_d[l].reshape(1, S5_W), (of_c, ob_c), (of_l, ob_l), zg,
                      gdn_norm[l].reshape(1, GDN_DK), w_out[l], mod_map, n_ctx_tiles)
        x = _mlp(x, mods_l, norm_mlp[l].reshape(1, D_MODEL), w_mlp_in[l], w_mlp_out[l], mod_map)

    g_fin = norm_final.reshape(1, D_MODEL)
    y_prompt = _final_norm(x, g_fin, 0, n_ctx_tiles).reshape(n_ctx, ctx_len, D_MODEL)
    y_sample = _final_norm(x, g_fin, n_ctx_tiles, lat_tok // TM).reshape(n_lat, lat_len, D_MODEL)
    return (y_prompt, y_sample, jnp.stack(fin_re, axis=1), jnp.stack(fin_im, axis=1),
            jnp.stack(fin_gdn, axis=1))
```

```python
import functools
import math

import numpy as np
import jax
import jax.numpy as jnp
from jax import lax
from jax.experimental import pallas as pl
from jax.experimental.pallas import tpu as pltpu

F32 = jnp.float32
BF16 = jnp.bfloat16
HIGHEST = lax.Precision.HIGHEST

D_MODEL = 1024
S5_W = 512
S5_P = 16
S5_G = 32
S5_N = 64
GDN_W = 512
GDN_DK = 128
GDN_H = 4
CHUNK = 64
N_DIR = 2
N_MOD = 6
D_FF = 4 * D_MODEL
GRID_W = 64
POS_BASE = 10000.0
EPS = 1e-6
S5_T = 16
S5_TP = S5_T * S5_P
N_GATE = 2 * N_DIR * GDN_H
GATE_PAD = 128
TM = 256
GT = 128
SB = 256
VMEM_LIMIT = 56 * 1024 * 1024


def _dot(a, b):
    return jnp.dot(a, b, preferred_element_type=F32)


def _dot_nt(a, b):
    return lax.dot_general(a, b, (((1,), (1,)), ((), ())), preferred_element_type=F32)


def _dot_tn(a, b):
    return lax.dot_general(a, b, (((0,), (0,)), ((), ())), preferred_element_type=F32)


def _sigmoid(x):
    return 1.0 / (1.0 + jnp.exp(-x))


def _silu(x):
    return x * _sigmoid(x)


def _softplus(x):
    return jnp.maximum(x, 0.0) + jnp.log1p(jnp.exp(-jnp.abs(x)))


def _gelu_tanh(x):
    c = math.sqrt(2.0 / math.pi)
    return 0.5 * x * (1.0 + jnp.tanh(c * (x + 0.044715 * (x * x * x))))


def _params(*sem):
    return pltpu.CompilerParams(dimension_semantics=sem, vmem_limit_bytes=VMEM_LIMIT)


def _mod_kernel(c_ref, w_ref, b_ref, o_ref):
    sc = _silu(c_ref[...])
    o_ref[0] = jnp.dot(sc, w_ref[0], precision=HIGHEST, preferred_element_type=F32) + b_ref[0]


def _modulation(cond, w_ada, b_ada):
    depth = w_ada.shape[0]
    tn = 1536
    n_w = N_MOD * D_MODEL
    return pl.pallas_call(
        _mod_kernel,
        grid=(depth, n_w // tn),
        in_specs=[pl.BlockSpec((8, D_MODEL), lambda l, j: (0, 0)),
                  pl.BlockSpec((1, D_MODEL, tn), lambda l, j: (l, 0, j)),
                  pl.BlockSpec((1, 1, tn), lambda l, j: (l, 0, j))],
        out_specs=pl.BlockSpec((1, 8, tn), lambda l, j: (l, 0, j)),
        out_shape=jax.ShapeDtypeStruct((depth, 8, n_w), F32),
        compiler_params=_params("parallel", "parallel"),
        name="modulation",
    )(cond, w_ada, b_ada.reshape(depth, 1, n_w))


def _embed_kernel(x_ref, tr_ref, tc_ref, o_ref, *, rows_per_tile, tiles_per_seq):
    r0 = (pl.program_id(0) % tiles_per_seq) * rows_per_tile
    half = D_MODEL // 2
    tc = tc_ref[...]
    for k in range(rows_per_tile):
        lo, hi = k * GRID_W, (k + 1) * GRID_W
        o_ref[lo:hi, 0:half] = x_ref[lo:hi, 0:half] + tr_ref[pl.ds(r0 + k, 1), :]
        o_ref[lo:hi, half:D_MODEL] = x_ref[lo:hi, half:D_MODEL] + tc


def _embed(x_sample):
    n_seq, length, _ = x_sample.shape
    rows = length // GRID_W
    quarter = D_MODEL // 4
    omega = 1.0 / (POS_BASE ** (jnp.arange(quarter, dtype=F32) / quarter))
    ang_r = jnp.arange(rows, dtype=F32)[:, None] * omega
    ang_c = jnp.arange(GRID_W, dtype=F32)[:, None] * omega
    tab_r = jnp.concatenate([jnp.sin(ang_r), jnp.cos(ang_r)], axis=-1)
    tab_c = jnp.concatenate([jnp.sin(ang_c), jnp.cos(ang_c)], axis=-1)
    rows_per_tile = 8
    tile = rows_per_tile * GRID_W
    tiles_per_seq = length // tile
    x2 = x_sample.reshape(n_seq * length, D_MODEL)
    return pl.pallas_call(
        functools.partial(_embed_kernel, rows_per_tile=rows_per_tile, tiles_per_seq=tiles_per_seq),
        grid=(n_seq * tiles_per_seq,),
        in_specs=[pl.BlockSpec((tile, D_MODEL), lambda i: (i, 0)),
                  pl.BlockSpec((rows, D_MODEL // 2), lambda i: (0, 0)),
                  pl.BlockSpec((GRID_W, D_MODEL // 2), lambda i: (0, 0))],
        out_specs=pl.BlockSpec((tile, D_MODEL), lambda i: (i, 0)),
        out_shape=jax.ShapeDtypeStruct(x2.shape, F32),
        compiler_params=_params("parallel"),
        name="pos_embed",
    )(x2, tab_r, tab_c)


def _norm_mod(x, g, shift, scale):
    ms = jnp.mean(x * x, axis=-1, keepdims=True)
    h = x * lax.rsqrt(ms + EPS) * g
    return h * (1.0 + scale) + shift


LANES = 128
SLOTS = LANES // S5_P


def _regroup_in(u_refs, ug_ref):
    blocks = TM // S5_T
    slot = lax.broadcasted_iota(jnp.int32, (blocks, LANES), 1) // S5_P
    rot = {}
    for j in range(S5_T):
        for a in range(S5_W // LANES):
            t = u_refs[a][pl.ds(j, blocks, stride=S5_T), :]
            rot[j, a] = t if j % SLOTS == 0 else pltpu.roll(t, S5_P * (j % SLOTS), 1)
    for g in range(S5_G):
        a, s = g // SLOTS, g % SLOTS
        for jt in range(S5_T // SLOTS):
            acc = None
            for sig in range(SLOTS):
                piece = rot[SLOTS * jt + (sig - s) % SLOTS, a]
                acc = piece if acc is None else jnp.where(slot == sig, piece, acc)
            ug_ref[g, :, jt * LANES:(jt + 1) * LANES] = acc.astype(ug_ref.dtype)


def _regroup_out(yg_ref, y_refs):
    blocks = TM // S5_T
    slot = lax.broadcasted_iota(jnp.int32, (blocks, LANES), 1) // S5_P
    for j in range(S5_T):
        jt = j // SLOTS
        for a in range(S5_W // LANES):
            acc = None
            for s in range(SLOTS):
                piece = yg_ref[SLOTS * a + s, :, jt * LANES:(jt + 1) * LANES].astype(F32)
                acc = piece if acc is None else jnp.where(slot == (s + j) % SLOTS, piece, acc)
            if j % SLOTS:
                acc = pltpu.roll(acc, LANES - S5_P * (j % SLOTS), 1)
            y_refs[a][pl.ds(j, blocks, stride=S5_T), :] = acc


def _in_kernel(x_ref, m_ref, g_ref, wu_ref, wz_ref, wqkv_ref, wzg_ref, wba_ref, wbat_ref,
               u_ref, ug_ref, z_ref, qkv_ref, zg_ref, ba_ref, bat_ref, *ut_refs):
    m = m_ref[0]
    h = _norm_mod(x_ref[...], g_ref[...], m[:, 0:D_MODEL], m[:, D_MODEL:2 * D_MODEL]).astype(BF16)
    u = _dot(h, wu_ref[...])
    u_ref[...] = u.astype(u_ref.dtype)
    for a, ut_ref in enumerate(ut_refs):
        ut_ref[...] = u[:, a * LANES:(a + 1) * LANES]
    _regroup_in(ut_refs, ug_ref)
    z_ref[...] = _dot(h, wz_ref[...]).astype(z_ref.dtype)
    qkv_ref[...] = _dot(h, wqkv_ref[...]).astype(qkv_ref.dtype)
    zg_ref[...] = _dot(h, wzg_ref[...]).astype(zg_ref.dtype)
    ba_ref[...] = _dot(h, wba_ref[...])
    bat_ref[...] = _dot_nt(wbat_ref[...], h)


def _mod_row_map(n_ctx_tiles, tiles_per_lat):
    def index(i):
        return (jnp.where(i < n_ctx_tiles, 0, 1 + (i - n_ctx_tiles) // tiles_per_lat), 0, 0)
    return index


def _in_proj(x, mods_l, g, w_in, mod_map):
    tokens = x.shape[0]
    o1, o2, o3, o4 = S5_W, 2 * S5_W, 2 * S5_W + 3 * GDN_W, 2 * S5_W + 4 * GDN_W
    wb = w_in.astype(BF16)
    w_ba = jnp.pad(wb[:, o4:], ((0, 0), (0, GATE_PAD - N_GATE)))
    w_bat = wb[:, o4:].T
    const = lambda i: (0, 0)
    tile = lambda n: pl.BlockSpec((TM, n), lambda i: (i, 0))
    full = lambda a: pl.BlockSpec(a.shape, const)
    ws = (wb[:, :o1], wb[:, o1:o2], wb[:, o2:o3], wb[:, o3:o4], w_ba, w_bat)
    return pl.pallas_call(
        _in_kernel,
        grid=(tokens // TM,),
        in_specs=[tile(D_MODEL), pl.BlockSpec((1, 1, N_MOD * D_MODEL), mod_map), full(g)]
                 + [full(w) for w in ws],
        out_specs=[tile(S5_W), pl.BlockSpec((S5_G, TM // S5_T, S5_TP), lambda i: (0, i, 0)),
                   tile(S5_W), tile(3 * GDN_W), tile(GDN_W), tile(GATE_PAD),
                   pl.BlockSpec((N_GATE, TM), lambda i: (0, i))],
        out_shape=[jax.ShapeDtypeStruct((tokens, S5_W), BF16),
                   jax.ShapeDtypeStruct((S5_G, tokens // S5_T, S5_TP), BF16),
                   jax.ShapeDtypeStruct((tokens, S5_W), BF16),
                   jax.ShapeDtypeStruct((tokens, 3 * GDN_W), BF16),
                   jax.ShapeDtypeStruct((tokens, GDN_W), BF16),
                   jax.ShapeDtypeStruct((tokens, GATE_PAD), F32),
                   jax.ShapeDtypeStruct((N_GATE, tokens), F32)],
        scratch_shapes=[pltpu.VMEM((TM, LANES), F32)] * (S5_W // LANES),
        compiler_params=_params("parallel"),
        name="in_proj",
    )(x, mods_l, g, *ws)


N_POW = 8


def _zoh(lam_re, lam_im, log_dt):
    lam_re = jnp.minimum(lam_re, -1e-4)
    dt = jnp.exp(log_dt)
    x, th = lam_re * dt, lam_im * dt
    mag = jnp.exp(x)
    num_re, num_im = mag * jnp.cos(th) - 1.0, mag * jnp.sin(th)
    den = lam_re * lam_re + lam_im * lam_im
    return (x, th, (num_re * lam_re + num_im * lam_im) / den, (num_im * lam_re - num_re * lam_im) / den)


def _power(x, th, e):
    mag = jnp.exp(x * e)
    return mag * jnp.cos(th * e), mag * jnp.sin(th * e)


def _shift_lanes(lo, hi, t, left):
    if t == 0:
        return lo, hi
    lane = lax.broadcasted_iota(jnp.int32, lo.shape, 1)
    zero = jnp.zeros_like(lo)
    r = t % LANES
    if not left:
        if t >= LANES:
            return zero, (lo if r == 0 else jnp.where(lane >= r, pltpu.roll(lo, r, 1), 0.0))
        rl, rh = pltpu.roll(lo, r, 1), pltpu.roll(hi, r, 1)
        return jnp.where(lane >= r, rl, 0.0), jnp.where(lane >= r, rh, rl)
    if t >= LANES:
        return (hi if r == 0 else jnp.where(lane < LANES - r, pltpu.roll(hi, LANES - r, 1), 0.0)), zero
    rl, rh = pltpu.roll(lo, LANES - r, 1), pltpu.roll(hi, LANES - r, 1)
    return jnp.where(lane < LANES - r, rl, rh), jnp.where(lane < LANES - r, rh, 0.0)


def _s5_ops_kernel(prow_ref, pcol_ref, bb_ref, ct_ref, k_ref, bf_ref, bbk_ref, cf_ref, cb_ref, ap_ref,
                   knat_ref):
    s16 = (pl.program_id(1) % SLOTS) * S5_P
    prow = prow_ref[0, 0]
    pcol = pcol_ref[0, 0]
    lane = lax.broadcasted_iota(jnp.int32, (1, 2 * S5_N), 1)
    sign_k = jnp.where(lane < S5_N, 1.0, -1.0)
    rep = jnp.where(lax.broadcasted_iota(jnp.int32, (S5_P, S5_TP), 1) % S5_P
                    == lax.broadcasted_iota(jnp.int32, (S5_P, S5_TP), 0), 1.0, 0.0)
    exact = functools.partial(jnp.dot, precision=HIGHEST, preferred_element_type=F32)
    e_rows = lax.broadcasted_iota(jnp.int32, (S5_T, 1), 0).astype(F32)
    e_lanes = lax.broadcasted_iota(jnp.int32, (1, LANES), 1).astype(F32)
    tok_r = lax.broadcasted_iota(jnp.int32, (S5_TP, S5_T), 0) // S5_P
    e_r = lax.broadcasted_iota(jnp.int32, (S5_TP, S5_T), 1)
    tok_l = lax.broadcasted_iota(jnp.int32, (LANES, S5_TP), 1) // S5_P
    e_l = lax.broadcasted_iota(jnp.int32, (LANES, S5_TP), 0)
    pick = lambda cond: jnp.where(cond, 1.0, 0.0)
    kall = []
    for d in range(N_DIR):
        x_r, th_r, fr_r, fi_r = _zoh(prow[d:d + 1], prow[2 + d:3 + d], prow[4 + d:5 + d])
        x_c, th_c, _, _ = _zoh(pcol[:, d:d + 1], pcol[:, 2 + d:3 + d], pcol[:, 4 + d:5 + d])
        bpack, bswap = bb_ref[0, 0, d, 0], bb_ref[0, 0, d, 1]
        bbar = fr_r * bpack + fi_r * bswap
        bbar_sw = fr_r * bswap - fi_r * bpack
        pr, pi = _power(x_r, th_r, e_rows)
        sel = pick(e_r == ((S5_T - 1) - tok_r if d == 0 else tok_r))
        pr, pi = exact(sel, pr), exact(sel, pi)
        b_nat = pr * jnp.concatenate([bbar] * S5_T, axis=0) + pi * jnp.concatenate([bbar_sw] * S5_T, axis=0)
        b_out = bf_ref if d == 0 else bbk_ref
        for rt in range(S5_TP // LANES):
            b_out[0, 0, rt * LANES:(rt + 1) * LANES, :] = pltpu.roll(
                b_nat[rt * LANES:(rt + 1) * LANES], s16, 0).astype(b_out.dtype)
        c_re = jnp.dot(ct_ref[0, 0, d, 0], rep, precision=HIGHEST, preferred_element_type=F32)
        c_im = jnp.dot(ct_ref[0, 0, d, 1], rep, precision=HIGHEST, preferred_element_type=F32)
        pw_r, pw_i = _power(x_c, th_c, e_lanes)
        sel = pick(e_l == (tok_l + 1 if d == 0 else S5_T - tok_l))
        pr, pi = exact(pw_r, sel), exact(pw_i, sel)
        c_nat = jnp.concatenate([c_re * pr - c_im * pi, -(c_re * pi + c_im * pr)], axis=0)
        c_out = cf_ref if d == 0 else cb_ref
        for ct in range(S5_TP // LANES):
            c_out[0, 0, :, ct * LANES:(ct + 1) * LANES] = pltpu.roll(
                c_nat[:, ct * LANES:(ct + 1) * LANES], s16, 1).astype(c_out.dtype)
        sel = pick(e_l == (tok_l if d == 0 else (S5_T - 1) - tok_l))
        pr, pi = exact(pw_r, sel), exact(pw_i, sel)
        ca = jnp.concatenate([c_re * pr - c_im * pi, c_re * pi + c_im * pr], axis=0)
        kall.append(jnp.dot(bbar * sign_k, ca, precision=HIGHEST, preferred_element_type=F32))
        row = lax.broadcasted_iota(jnp.int32, (2 * N_POW, 1), 0)
        expo = (S5_T * jnp.left_shift(1, row // 2)).astype(F32)
        pr, pi = _power(x_r, th_r, expo)
        ap_ref[0, 0, d * 2 * N_POW:(d + 1) * 2 * N_POW, :] = jnp.where(row % 2 == 0, pr, -pi * sign_k)

    kf_lo, kf_hi = kall[0][:, :LANES], kall[0][:, LANES:]
    kb_lo, kb_hi = kall[1][:, :LANES], kall[1][:, LANES:]
    for j in range(S5_T):
        f_lo, f_hi = _shift_lanes(kf_lo, kf_hi, S5_P * j, False)
        b_lo, b_hi = _shift_lanes(kb_lo, kb_hi, S5_P * (S5_T - 1 - j), True)
        knat_ref[j * S5_P:(j + 1) * S5_P, 0:LANES] = f_lo + b_lo
        knat_ref[j * S5_P:(j + 1) * S5_P, LANES:2 * LANES] = f_hi + b_hi
    for rt in range(S5_TP // LANES):
        for ct in range(S5_TP // LANES):
            tile = knat_ref[rt * LANES:(rt + 1) * LANES, ct * LANES:(ct + 1) * LANES]
            k_ref[0, 0, rt * LANES:(rt + 1) * LANES, ct * LANES:(ct + 1) * LANES] = pltpu.roll(
                pltpu.roll(tile, s16, 0), s16, 1).astype(k_ref.dtype)


def _s5_operators(lam_re, lam_im, log_dt, b_re, b_im, c_re, c_im):
    depth = lam_re.shape[0]
    dup = lambda a: jnp.concatenate([a, a], axis=-1)
    ldt = jnp.broadcast_to(log_dt[..., None], lam_re.shape)
    g_major = lambda a: jnp.moveaxis(a, 1, 2)
    prow = jnp.concatenate([dup(g_major(a)) for a in (lam_re, lam_im, ldt)], axis=2)
    pcol = jnp.concatenate([g_major(a) for a in (lam_re, lam_im, ldt)], axis=2)
    pcol = jnp.pad(jnp.swapaxes(pcol, 2, 3), ((0, 0), (0, 0), (0, 0), (0, 2)))
    bt_re, bt_im = (jnp.swapaxes(g_major(a), 3, 4) for a in (b_re, b_im))
    bb = jnp.stack([jnp.concatenate([bt_re, bt_im], axis=-1),
                    jnp.concatenate([-bt_im, bt_re], axis=-1)], axis=3)
    ct = jnp.stack([jnp.swapaxes(g_major(a), 3, 4) for a in (c_re, c_im)], axis=3)
    per = lambda a: pl.BlockSpec((1, 1) + a.shape[2:], lambda l, g: (l, g) + (0,) * (a.ndim - 2))
    mat = lambda r, c: pl.BlockSpec((1, 1, r, c), lambda l, g: (l, g, 0, 0))
    shape = lambda r, c, dt: jax.ShapeDtypeStruct((depth, S5_G, r, c), dt)
    return pl.pallas_call(
        _s5_ops_kernel,
        grid=(depth, S5_G),
        in_specs=[per(prow), per(pcol), per(bb), per(ct)],
        out_specs=[mat(S5_TP, S5_TP), mat(S5_TP, 2 * S5_N), mat(S5_TP, 2 * S5_N),
                   mat(2 * S5_N, S5_TP), mat(2 * S5_N, S5_TP), mat(N_DIR * 2 * N_POW, 2 * S5_N)],
        out_shape=[shape(S5_TP, S5_TP, BF16), shape(S5_TP, 2 * S5_N, BF16), shape(S5_TP, 2 * S5_N, BF16),
                   shape(2 * S5_N, S5_TP, BF16), shape(2 * S5_N, S5_TP, BF16),
                   shape(N_DIR * 2 * N_POW, 2 * S5_N, F32)],
        scratch_shapes=[pltpu.VMEM((S5_TP, S5_TP), F32)],
        compiler_params=_params("parallel", "parallel"),
        name="s5_operators",
    )(prow, pcol, bb, ct)


def _lane_swap(x):
    return pltpu.roll(x, S5_N, 1)


def _cmul(a1, a2, h):
    return a1 * h + a2 * _lane_swap(h)


def _block_scan(s, seg, a_pow, d, reverse):
    rows = s.shape[0]
    ridx = lax.broadcasted_iota(jnp.int32, s.shape, 0) % seg
    h = s
    k, step = 0, 1
    while step < seg:
        a1 = a_pow[d * 16 + 2 * k:d * 16 + 2 * k + 1]
        a2 = a_pow[d * 16 + 2 * k + 1:d * 16 + 2 * k + 2]
        if reverse:
            sh = jnp.where(ridx < seg - step, pltpu.roll(h, rows - step, 0), 0.0)
        else:
            sh = jnp.where(ridx >= step, pltpu.roll(h, step, 0), 0.0)
        h = h + _cmul(a1, a2, sh)
        k += 1
        step *= 2
    return h


def _s5_kernel(u_ref, k_ref, bf_ref, bb_ref, cf_ref, cb_ref, ap_ref, h0_ref,
               y_ref, fin_ref, hf_ref, hb_ref, *, ctx_rows, ctx_seg, lat_seg, n_ctx, n_lat):
    u = u_ref[0]
    a_pow = ap_ref[0]
    s_f = _dot(u, bf_ref[0])
    s_b = _dot(u, bb_ref[0])
    lat_rows = u.shape[0] - ctx_rows

    lrow = lax.broadcasted_iota(jnp.int32, (lat_rows, 2 * S5_N), 0)
    lmod = lrow % lat_seg
    lseq = lrow // lat_seg
    h0 = h0_ref[0]
    h0_f = jnp.zeros((lat_rows, 2 * S5_N), F32)
    h0_b = jnp.zeros((lat_rows, 2 * S5_N), F32)
    for s in range(n_lat):
        h0_f = jnp.where(lseq == s, h0[2 * s:2 * s + 1], h0_f)
        h0_b = jnp.where(lseq == s, h0[2 * s + 1:2 * s + 2], h0_b)
    first, last = lmod == 0, lmod == lat_seg - 1
    sl_f = s_f[ctx_rows:] + jnp.where(first, _cmul(a_pow[0:1], a_pow[1:2], h0_f), 0.0)
    sl_b = s_b[ctx_rows:] + jnp.where(last, _cmul(a_pow[16:17], a_pow[17:18], h0_b), 0.0)

    hc_f = _block_scan(s_f[:ctx_rows], ctx_seg, a_pow, 0, False)
    hc_b = _block_scan(s_b[:ctx_rows], ctx_seg, a_pow, 1, True)
    hl_f = _block_scan(sl_f, lat_seg, a_pow, 0, False)
    hl_b = _block_scan(sl_b, lat_seg, a_pow, 1, True)

    cmod = lax.broadcasted_iota(jnp.int32, (ctx_rows, 2 * S5_N), 0) % ctx_seg
    pc_f = jnp.where(cmod == 0, 0.0, pltpu.roll(hc_f, 1, 0))
    pc_b = jnp.where(cmod == ctx_seg - 1, 0.0, pltpu.roll(hc_b, ctx_rows - 1, 0))
    pl_f = jnp.where(first, h0_f, pltpu.roll(hl_f, 1, 0))
    pl_b = jnp.where(last, h0_b, pltpu.roll(hl_b, lat_rows - 1, 0))
    prev_f = jnp.concatenate([pc_f, pl_f], axis=0).astype(BF16)
    prev_b = jnp.concatenate([pc_b, pl_b], axis=0).astype(BF16)
    y_ref[0] = (_dot(u, k_ref[0]) + _dot(prev_f, cf_ref[0]) + _dot(prev_b, cb_ref[0])).astype(y_ref.dtype)

    hf_ref[...] = hc_f
    hb_ref[...] = hc_b
    fin_ref[0, 0] = hf_ref[pl.ds(ctx_seg - 1, n_ctx, stride=ctx_seg), :]
    fin_ref[0, 1] = hb_ref[pl.ds(0, n_ctx, stride=ctx_seg), :]


def _s5_mixer(u_g, ops, h0, n_ctx, ctx_len, n_lat, lat_len):
    k_mat, b_f, b_b, c_f, c_b, a_pow = ops
    rows = u_g.shape[1]
    ctx_seg, lat_seg = ctx_len // S5_T, lat_len // S5_T
    ctx_rows = n_ctx * ctx_seg
    per_g = lambda a: pl.BlockSpec((1,) + a.shape[1:], lambda g: (g,) + (0,) * (a.ndim - 1))
    kern = functools.partial(_s5_kernel, ctx_rows=ctx_rows, ctx_seg=ctx_seg, lat_seg=lat_seg,
                             n_ctx=n_ctx, n_lat=n_lat)
    return pl.pallas_call(
        kern,
        grid=(S5_G,),
        in_specs=[per_g(a) for a in (u_g, k_mat, b_f, b_b, c_f, c_b, a_pow, h0)],
        out_specs=[pl.BlockSpec((1, rows, S5_TP), lambda g: (g, 0, 0)),
                   pl.BlockSpec((1, N_DIR, n_ctx, 2 * S5_N), lambda g: (g, 0, 0, 0))],
        out_shape=[jax.ShapeDtypeStruct((S5_G, rows, S5_TP), BF16),
                   jax.ShapeDtypeStruct((S5_G, N_DIR, n_ctx, 2 * S5_N), F32)],
        scratch_shapes=[pltpu.VMEM((ctx_rows, 2 * S5_N), F32)] * N_DIR,
        compiler_params=_params("parallel"),
        name="s5_mixer",
    )(u_g, k_mat, b_f, b_b, c_f, c_b, a_pow, h0)


def _gdn_pre_kernel(x_ref, hp_ref, hn_ref, cw_ref, ba_ref, bat_ref, al_ref, dl_ref, ar_ref, dr_ref,
                    u_ref, wq_ref, kd_ref, a_ref, eg_ref,
                    q_ref, k_ref, v_ref, col_ref, row_ref, m_ref, acc_ref):
    x = x_ref[...].astype(F32)
    rid = lax.broadcasted_iota(jnp.int32, x.shape, 0)
    x_prev = jnp.where(rid == 0, hp_ref[0].astype(F32), pltpu.roll(x, 1, 0))
    x_next = jnp.where(rid == GT - 1, hn_ref[0].astype(F32), pltpu.roll(x, GT - 1, 0))
    cw = cw_ref[...]
    y = _silu(x_prev * cw[0:1] + x * cw[1:2] + x_next * cw[2:3])
    for h in range(GDN_H):
        lo, hi = h * GDN_DK, (h + 1) * GDN_DK
        qh = y[:, lo:hi]
        kh = y[:, GDN_W + lo:GDN_W + hi]
        q_ref[:, lo:hi] = qh * (lax.rsqrt(jnp.sum(qh * qh, axis=-1, keepdims=True) + EPS)
                                * (GDN_DK ** -0.5))
        k_ref[:, lo:hi] = kh * lax.rsqrt(jnp.sum(kh * kh, axis=-1, keepdims=True) + EPS)
    v_ref[...] = y[:, 2 * GDN_W:]

    half = N_GATE // 2
    ba = ba_ref[...]
    lane = lax.broadcasted_iota(jnp.int32, ba.shape, 1)
    gate = jnp.where(lane < half, _sigmoid(ba), -jnp.exp(al_ref[...]) * _softplus(ba + dl_ref[...]))
    ri = lax.broadcasted_iota(jnp.int32, (GT, GT), 0)
    ci = lax.broadcasted_iota(jnp.int32, (GT, GT), 1)
    same = (ri // CHUNK) == (ci // CHUNK)
    lower = jnp.where(same & (ci <= ri), 1.0, 0.0)
    upper = jnp.where(same & (ci >= ri), 1.0, 0.0)
    cum_f = jnp.dot(lower, gate, precision=HIGHEST, preferred_element_type=F32)
    cum_b = jnp.dot(upper, gate, precision=HIGHEST, preferred_element_type=F32)
    fwd_lane = (lane >= half) & (lane < half + GDN_H)
    bwd_lane = (lane >= half + GDN_H) & (lane < N_GATE)
    col_ref[...] = jnp.where(fwd_lane, cum_f, jnp.where(bwd_lane, cum_b, gate))
    bat = bat_ref[...]
    g_row = -jnp.exp(ar_ref[...]) * _softplus(bat + dr_ref[...])
    row_f = jnp.dot(g_row, upper, precision=HIGHEST, preferred_element_type=F32)
    row_b = jnp.dot(g_row, lower, precision=HIGHEST, preferred_element_type=F32)
    srow = lax.broadcasted_iota(jnp.int32, bat.shape, 0)
    row_ref[...] = jnp.where(srow < half + GDN_H, row_f, row_b)

    assert 2 * CHUNK == LANES and GT == 2 * CHUNK
    ii = lax.broadcasted_iota(jnp.int32, (CHUNK, LANES), 0)
    ll = lax.broadcasted_iota(jnp.int32, (CHUNK, LANES), 1)
    jj = ll % CHUNK
    fwd = ll < CHUNK
    incl = (fwd & (jj <= ii)) | (~fwd & (jj >= ii))
    strict = (fwd & (jj < ii)) | (~fwd & (jj > ii))
    fwd1 = lax.broadcasted_iota(jnp.int32, (1, LANES), 1) < CHUNK
    keep_f = jnp.where(fwd, 1.0, 0.0).astype(BF16)
    keep_b = jnp.where(fwd, 0.0, 1.0).astype(BF16)
    pairs = [(c, h) for c in range(GT // CHUNK) for h in range(GDN_H)]

    def both(c, lane_f, lane_b):
        rows = slice(c * CHUNK, (c + 1) * CHUNK)
        return jnp.where(fwd, col_ref[rows, lane_f:lane_f + 1], col_ref[rows, lane_b:lane_b + 1])

    def block_diag(x16):
        return jnp.concatenate([x16 * keep_f, x16 * keep_b], axis=0)

    for n, (c, h) in enumerate(pairs):
        rows = slice(c * CHUNK, (c + 1) * CHUNK)
        cols = slice(h * GDN_DK, (h + 1) * GDN_DK)
        beta = both(c, h, GDN_H + h)
        gc = both(c, half + h, half + GDN_H + h)
        r_f = row_ref[half + h:half + h + 1, :]
        r_b = row_ref[half + GDN_H + h:half + GDN_H + h + 1, :]
        gr = (jnp.where(fwd1, r_f, pltpu.roll(r_b, CHUNK, 1)) if c == 0
              else jnp.where(fwd1, pltpu.roll(r_f, CHUNK, 1), r_b))
        decay = jnp.where(incl, jnp.exp(jnp.where(incl, gc - gr, 0.0)), 0.0)
        k16 = k_ref[rows, cols].astype(BF16)
        kk16 = jnp.concatenate([k16, k16], axis=0)
        m = jnp.where(strict, beta * _dot_nt(k16, kk16) * decay, 0.0)
        a_ref[rows, cols] = (_dot_nt(q_ref[rows, cols].astype(BF16), kk16) * decay).astype(BF16)
        m_ref[n] = m
        acc_ref[n] = jnp.where((ii // 2) == (jj // 2), -m, 0.0)

    b = 2
    while b < CHUNK:
        pair = ((ii // (2 * b)) == (jj // (2 * b))) & ((ii // b) != (jj // b))
        for n in range(len(pairs)):
            cpart = jnp.where(pair, m_ref[n], 0.0)
            acc = acc_ref[n]
            a16 = acc.astype(BF16)
            y = cpart + _dot(a16, block_diag(cpart.astype(BF16)))
            acc_ref[n] = acc - y - _dot(y.astype(BF16), block_diag(a16))
        b *= 2

    zeros = jnp.zeros((CHUNK, 2 * GDN_DK), BF16)
    for n, (c, h) in enumerate(pairs):
        rows = slice(c * CHUNK, (c + 1) * CHUNK)
        cols = slice(h * GDN_DK, (h + 1) * GDN_DK)
        kh, vh, qh = k_ref[rows, cols], v_ref[rows, cols], q_ref[rows, cols]
        rhs, egc, gcs = [], [], []
        for d in range(N_DIR):
            lane_ = d * GDN_H + h
            beta = col_ref[rows, lane_:lane_ + 1]
            gc = col_ref[rows, half + lane_:half + lane_ + 1]
            gcs.append(gc)
            egc.append(jnp.exp(gc))
            rhs.append(jnp.concatenate([vh * beta, kh * beta * egc[d]], axis=1))
        stacked = jnp.concatenate(
            [jnp.concatenate([rhs[0].astype(BF16), zeros], axis=1),
             jnp.concatenate([zeros, rhs[1].astype(BF16)], axis=1)], axis=0)
        prod = _dot(acc_ref[n].astype(BF16), stacked)
        for d in range(N_DIR):
            sol = rhs[d] + prod[:, d * 2 * GDN_DK:(d + 1) * 2 * GDN_DK]
            u_ref[d, rows, cols] = sol[:, :GDN_DK]
            wq_ref[d, c, 0:CHUNK, cols] = sol[:, GDN_DK:].astype(BF16)
            wq_ref[d, c, CHUNK:2 * CHUNK, cols] = (qh * egc[d]).astype(BF16)
            g_last = gcs[d][0:1] if d == 1 else gcs[d][CHUNK - 1:CHUNK]
            kd_ref[d, rows, cols] = (kh * jnp.exp(g_last - gcs[d])).astype(BF16)
            eg_ref[c, d * GDN_H + h:d * GDN_H + h + 1, :] = jnp.broadcast_to(jnp.exp(g_last), (1, GDN_DK))


def _gdn_pre(qkv, halo_prev, halo_next, conv_w, ba, bat, a_log, dt_bias):
    tokens = qkv.shape[0]
    n_tiles = tokens // GT
    cpt = GT // CHUNK
    half = N_GATE // 2
    al = a_log.reshape(-1)
    db = dt_bias.reshape(-1)
    al_l = jnp.zeros((1, GATE_PAD), F32).at[0, half:N_GATE].set(al)
    db_l = jnp.zeros((1, GATE_PAD), F32).at[0, half:N_GATE].set(db)
    al_r = jnp.zeros((N_GATE, 1), F32).at[half:, 0].set(al)
    db_r = jnp.zeros((N_GATE, 1), F32).at[half:, 0].set(db)
    const = lambda i: (0, 0)
    tile = lambda n: pl.BlockSpec((GT, n), lambda i: (i, 0))
    halo = pl.BlockSpec((1, 1, 3 * GDN_W), lambda i: (i, 0, 0))
    n_chunks = tokens // CHUNK
    n_pairs = cpt * GDN_H
    return pl.pallas_call(
        _gdn_pre_kernel,
        grid=(n_tiles,),
        in_specs=[tile(3 * GDN_W), halo, halo, pl.BlockSpec(conv_w.shape, const),
                  tile(GATE_PAD), pl.BlockSpec((N_GATE, GT), lambda i: (0, i)),
                  pl.BlockSpec((1, GATE_PAD), const), pl.BlockSpec((1, GATE_PAD), const),
                  pl.BlockSpec((N_GATE, 1), const), pl.BlockSpec((N_GATE, 1), const)],
        out_specs=[pl.BlockSpec((N_DIR, GT, GDN_W), lambda i: (0, i, 0)),
                   pl.BlockSpec((N_DIR, cpt, 2 * CHUNK, GDN_W), lambda i: (0, i, 0, 0)),
                   pl.BlockSpec((N_DIR, GT, GDN_W), lambda i: (0, i, 0)),
                   pl.BlockSpec((GT, GDN_W), lambda i: (i, 0)),
                   pl.BlockSpec((cpt, N_DIR * GDN_H, GDN_DK), lambda i: (i, 0, 0))],
        out_shape=[jax.ShapeDtypeStruct((N_DIR, tokens, GDN_W), F32),
                   jax.ShapeDtypeStruct((N_DIR, n_chunks, 2 * CHUNK, GDN_W), BF16),
                   jax.ShapeDtypeStruct((N_DIR, tokens, GDN_W), BF16),
                   jax.ShapeDtypeStruct((tokens, GDN_W), BF16),
                   jax.ShapeDtypeStruct((n_chunks, N_DIR * GDN_H, GDN_DK), F32)],
        scratch_shapes=[pltpu.VMEM((GT, GDN_W), F32)] * 3
                       + [pltpu.VMEM((GT, GATE_PAD), F32), pltpu.VMEM((N_GATE, GT), F32),
                          pltpu.VMEM((n_pairs, CHUNK, N_DIR * CHUNK), F32),
                          pltpu.VMEM((n_pairs, CHUNK, N_DIR * CHUNK), F32)],
        compiler_params=_params("parallel"),
        name="gdn_pre",
    )(qkv, halo_prev, halo_next, conv_w, ba, bat, al_l, db_l, al_r, db_r)


def _gdn_scan_kernel(*refs, has_s0, has_fin):
    uf, wqf, kdf, af, egf, ub, wqb, kdb, ab, egb = refs[:10]
    pos = 10
    s0_ref = None
    if has_s0:
        s0_ref = refs[pos]
        pos += 1
    of_ref, ob_ref = refs[pos], refs[pos + 1]
    pos += 2
    fin_ref = None
    if has_fin:
        fin_ref = refs[pos]
        pos += 1
    s_ref = refs[pos]
    j = pl.program_id(1)

    @pl.when(j == 0)
    def _():
        if has_s0:
            s_ref[...] = s0_ref[0]
        else:
            s_ref[...] = jnp.zeros(s_ref.shape, F32)

    cpb = SB // CHUNK
    dirs = ((uf, wqf, kdf, af, egf, of_ref), (ub, wqb, kdb, ab, egb, ob_ref))
    for step in range(cpb):
        work = []
        for d, (u_r, wq_r, kd_r, a_r, eg_r, o_r) in enumerate(dirs):
            c = step if d == 0 else cpb - 1 - step
            rows = slice(c * CHUNK, (c + 1) * CHUNK)
            for h in range(GDN_H):
                work.append((d, h, c, rows, slice(h * GDN_DK, (h + 1) * GDN_DK), u_r, wq_r, kd_r, a_r,
                             eg_r, o_r))
        ps = [_dot(wq_r[0, c, :, cols], s_ref[d, h].astype(BF16))
              for (d, h, c, rows, cols, u_r, wq_r, kd_r, a_r, eg_r, o_r) in work]
        v16 = [(u_r[0, rows, cols] - p[:CHUNK]).astype(BF16)
               for p, (d, h, c, rows, cols, u_r, wq_r, kd_r, a_r, eg_r, o_r) in zip(ps, work)]
        zero = jnp.zeros((CHUNK, GDN_DK), BF16)
        for p, v, (d, h, c, rows, cols, u_r, wq_r, kd_r, a_r, eg_r, o_r) in zip(ps, v16, work):
            v2 = jnp.concatenate([v, zero] if d == 0 else [zero, v], axis=0)
            o_r[rows, cols] = (p[CHUNK:] + _dot(a_r[rows, cols], v2)).astype(o_r.dtype)
        for v, (d, h, c, rows, cols, u_r, wq_r, kd_r, a_r, eg_r, o_r) in zip(v16, work):
            eg = eg_r[c, d * GDN_H + h:d * GDN_H + h + 1, :]
            s_ref[d, h] = s_ref[d, h] * eg + _dot_tn(kd_r[0, rows, cols], v)

    if has_fin:
        @pl.when(j == pl.num_programs(1) - 1)
        def _():
            fin_ref[0] = s_ref[...]


def _gdn_scan(u, wq, kd, a, eg, s0, n_seq, length, base_tok, want_fin):
    n_b = length // SB
    cpb = SB // CHUNK
    base = base_tok // SB
    in_specs, args = [], []
    for d in range(N_DIR):
        blk = (lambda s, j: base + s * n_b + j) if d == 0 else (lambda s, j: base + s * n_b + (n_b - 1 - j))
        in_specs += [pl.BlockSpec((1, SB, GDN_W), lambda s, j, d=d, blk=blk: (d, blk(s, j), 0)),
                     pl.BlockSpec((1, cpb, 2 * CHUNK, GDN_W), lambda s, j, d=d, blk=blk: (d, blk(s, j), 0, 0)),
                     pl.BlockSpec((1, SB, GDN_W), lambda s, j, d=d, blk=blk: (d, blk(s, j), 0)),
                     pl.BlockSpec((SB, GDN_W), lambda s, j, blk=blk: (blk(s, j), 0)),
                     pl.BlockSpec((cpb, N_DIR * GDN_H, GDN_DK), lambda s, j, blk=blk: (blk(s, j), 0, 0))]
        args += [u, wq, kd, a, eg]
    state_spec = pl.BlockSpec((1, N_DIR, GDN_H, GDN_DK, GDN_DK), lambda s, j: (s, 0, 0, 0, 0))
    if s0 is not None:
        in_specs.append(state_spec)
        args.append(s0)
    tok = lambda m: pl.BlockSpec((SB, GDN_W), m)
    out_specs = [tok(lambda s, j: (s * n_b + j, 0)), tok(lambda s, j: (s * n_b + (n_b - 1 - j), 0))]
    out_shape = [jax.ShapeDtypeStruct((n_seq * length, GDN_W), BF16)] * 2
    if want_fin:
        out_specs.append(state_spec)
        out_shape.append(jax.ShapeDtypeStruct((n_seq, N_DIR, GDN_H, GDN_DK, GDN_DK), F32))
    return pl.pallas_call(
        functools.partial(_gdn_scan_kernel, has_s0=s0 is not None, has_fin=want_fin),
        grid=(n_seq, n_b),
        in_specs=in_specs,
        out_specs=out_specs,
        out_shape=out_shape,
        scratch_shapes=[pltpu.VMEM((N_DIR, GDN_H, GDN_DK, GDN_DK), F32)],
        compiler_params=_params("parallel", "arbitrary"),
        name="gdn_scan",
    )(*args)


def _out_mlp_kernel(x_ref, m_ref, yg_ref, u_ref, z_ref, dsk_ref, ofc_ref, obc_ref, ofl_ref, obl_ref,
                    zg_ref, gn_ref, w_ref, g2_ref, w1_ref, w2_ref, o_ref, *y5_refs, n_ctx_tiles):
    m = m_ref[0]
    up = lambda r: r[...].astype(F32)
    _regroup_out(yg_ref, y5_refs)
    y5 = jnp.concatenate([r[...] for r in y5_refs], axis=1)
    s5 = _gelu_tanh(y5 + up(u_ref) * dsk_ref[...]) * _sigmoid(up(z_ref))
    acc = _dot(s5.astype(BF16), w_ref[0:S5_W, :])
    in_ctx = pl.program_id(0) < n_ctx_tiles
    o = jnp.where(in_ctx, up(ofc_ref) + up(obc_ref), up(ofl_ref) + up(obl_ref))
    zg = up(zg_ref)
    gn = gn_ref[...]
    for h in range(GDN_H):
        lo, hi = h * GDN_DK, (h + 1) * GDN_DK
        oh = o[:, lo:hi]
        nh = oh * lax.rsqrt(jnp.mean(oh * oh, axis=-1, keepdims=True) + EPS) * gn
        gh = (nh * _silu(zg[:, lo:hi])).astype(BF16)
        acc = acc + _dot(gh, w_ref[S5_W + lo:S5_W + hi, :])
    x = x_ref[...] + m[:, 2 * D_MODEL:3 * D_MODEL] * acc
    h = _norm_mod(x, g2_ref[...], m[:, 3 * D_MODEL:4 * D_MODEL], m[:, 4 * D_MODEL:5 * D_MODEL])
    ff = jnp.maximum(_dot(h.astype(BF16), w1_ref[...]), 0.0)
    ff = (ff * ff).astype(BF16)
    o_ref[...] = x + m[:, 5 * D_MODEL:6 * D_MODEL] * _dot(ff, w2_ref[...])


def _out_mlp(x, mods_l, y_g, u, z, d_skip, o_ctx, o_lat, zg, gdn_norm, w_out, g2, w1, w2, mod_map,
             n_ctx_tiles):
    tokens = x.shape[0]
    n_lat_tiles = tokens // TM - n_ctx_tiles
    const = lambda i: (0, 0)
    tile = lambda n: pl.BlockSpec((TM, n), lambda i: (i, 0))
    ctx_tile = pl.BlockSpec((TM, GDN_W), lambda i: (jnp.minimum(i, n_ctx_tiles - 1), 0))
    lat_tile = pl.BlockSpec((TM, GDN_W), lambda i: (jnp.maximum(i - n_ctx_tiles, 0), 0))
    once = pl.Buffered(1)
    assert n_lat_tiles > 0
    return pl.pallas_call(
        functools.partial(_out_mlp_kernel, n_ctx_tiles=n_ctx_tiles),
        grid=(tokens // TM,),
        in_specs=[tile(D_MODEL), pl.BlockSpec((1, 1, N_MOD * D_MODEL), mod_map),
                  pl.BlockSpec((S5_G, TM // S5_T, S5_TP), lambda i: (0, i, 0)),
                  tile(S5_W), tile(S5_W), pl.BlockSpec((1, S5_W), const),
                  ctx_tile, ctx_tile, lat_tile, lat_tile,
                  tile(GDN_W), pl.BlockSpec((1, GDN_DK), const),
                  pl.BlockSpec((D_MODEL, D_MODEL), const, pipeline_mode=once),
                  pl.BlockSpec((1, D_MODEL), const),
                  pl.BlockSpec((D_MODEL, D_FF), const, pipeline_mode=once),
                  pl.BlockSpec((D_FF, D_MODEL), const, pipeline_mode=once)],
        out_specs=tile(D_MODEL),
        out_shape=jax.ShapeDtypeStruct((tokens, D_MODEL), F32),
        scratch_shapes=[pltpu.VMEM((TM, LANES), F32)] * (S5_W // LANES),
        compiler_params=_params("parallel"),
        name="out_mlp",
    )(x, mods_l, y_g, u, z, d_skip, *o_ctx, *o_lat, zg, gdn_norm, w_out.astype(BF16), g2,
      w1.astype(BF16), w2.astype(BF16))


def _final_kernel(x_ref, g_ref, o_ref):
    x = x_ref[...]
    o_ref[...] = x * lax.rsqrt(jnp.mean(x * x, axis=-1, keepdims=True) + EPS) * g_ref[...]


def _final_norm(x, g, first_tile, n_tiles):
    return pl.pallas_call(
        _final_kernel,
        grid=(n_tiles,),
        in_specs=[pl.BlockSpec((TM, D_MODEL), lambda i: (first_tile + i, 0)),
                  pl.BlockSpec((1, D_MODEL), lambda i: (0, 0))],
        out_specs=pl.BlockSpec((TM, D_MODEL), lambda i: (i, 0)),
        out_shape=jax.ShapeDtypeStruct((n_tiles * TM, D_MODEL), F32),
        compiler_params=_params("parallel"),
        name="final_norm",
    )(x, g)


def kernel(x_prompt, x_sample, state_s5_re, state_s5_im, state_gdn, c, c_ctx, norm_mix, norm_mlp, w_ada, b_ada, w_in, conv_qkv, s5_lambda_re, s5_lambda_im, s5_log_dt, s5_b_re, s5_b_im, s5_c_re, s5_c_im, s5_d, gdn_a_log, gdn_dt_bias, gdn_norm, w_out, w_mlp_in, w_mlp_out, norm_final):
    n_ctx, ctx_len, _ = x_prompt.shape
    n_lat, lat_len, _ = x_sample.shape
    depth = w_in.shape[0]
    assert ctx_len % TM == 0 and lat_len % TM == 0 and lat_len % (8 * GRID_W) == 0
    assert n_lat + 1 <= 8
    ctx_tok, lat_tok = n_ctx * ctx_len, n_lat * lat_len
    tokens = ctx_tok + lat_tok
    n_ctx_tiles, tiles_per_lat = ctx_tok // TM, lat_len // TM
    mod_map = _mod_row_map(n_ctx_tiles, tiles_per_lat)

    cond = jnp.zeros((8, D_MODEL), F32).at[0].set(c_ctx).at[1:1 + n_lat].set(c)
    mods = _modulation(cond, w_ada, b_ada)
    x = jnp.concatenate([x_prompt.reshape(ctx_tok, D_MODEL), _embed(x_sample)], axis=0)

    starts = np.arange(tokens // GT) * GT
    in_ctx = starts < ctx_tok
    pos = np.where(in_ctx, starts % ctx_len, (starts - ctx_tok) % lat_len)
    has_prev = pos != 0
    has_next = pos + GT != np.where(in_ctx, ctx_len, lat_len)

    s5_ops = _s5_operators(s5_lambda_re, s5_lambda_im, s5_log_dt, s5_b_re, s5_b_im, s5_c_re, s5_c_im)
    fin_re, fin_im, fin_gdn = [], [], []
    for l in range(depth):
        mods_l = mods[l].reshape(8, 1, N_MOD * D_MODEL)
        u, u_g, z, qkv, zg, ba, bat = _in_proj(x, mods_l, norm_mix[l].reshape(1, D_MODEL), w_in[l],
                                               mod_map)

        h0 = jnp.concatenate([state_s5_re[:, l], state_s5_im[:, l]], axis=-1)
        h0 = h0.transpose(2, 0, 1, 3).reshape(S5_G, n_lat * N_DIR, 2 * S5_N)
        y_g, s5_fin = _s5_mixer(u_g, tuple(o[l] for o in s5_ops), h0, n_ctx, ctx_len, n_lat, lat_len)
        s5_fin = s5_fin.transpose(2, 1, 0, 3)
        fin_re.append(s5_fin[..., :S5_N])
        fin_im.append(s5_fin[..., S5_N:])

        qkv_t = qkv.reshape(tokens // GT, GT, 3 * GDN_W)
        zero_row = jnp.zeros((1, 3 * GDN_W), qkv.dtype)
        prev_rows = jnp.concatenate([zero_row, qkv_t[:-1, GT - 1]], axis=0)
        next_rows = jnp.concatenate([qkv_t[1:, 0], zero_row], axis=0)
        halo_prev = jnp.where(has_prev[:, None], prev_rows, 0.0)[:, None, :]
        halo_next = jnp.where(has_next[:, None], next_rows, 0.0)[:, None, :]
        pre = _gdn_pre(qkv, halo_prev, halo_next, conv_qkv[l], ba, bat, gdn_a_log[l], gdn_dt_bias[l])
        of_c, ob_c, gdn_fin = _gdn_scan(*pre, None, n_ctx, ctx_len, 0, True)
        of_l, ob_l = _gdn_scan(*pre, state_gdn[:, l], n_lat, lat_len, ctx_tok, False)
        fin_gdn.append(gdn_fin)

        x = _out_mlp(x, mods_l, y_g, u, z, s5_d[l].reshape(1, S5_W), (of_c, ob_c), (of_l, ob_l), zg,
                     gdn_norm[l].reshape(1, GDN_DK), w_out[l], norm_mlp[l].reshape(1, D_MODEL),
                     w_mlp_in[l], w_mlp_out[l], mod_map, n_ctx_tiles)

    g_fin = norm_final.reshape(1, D_MODEL)
    y_prompt = _final_norm(x, g_fin, 0, n_ctx_tiles).reshape(n_ctx, ctx_len, D_MODEL)
    y_sample = _final_norm(x, g_fin, n_ctx_tiles, lat_tok // TM).reshape(n_lat, lat_len, D_MODEL)
    return (y_prompt, y_sample, jnp.stack(fin_re, axis=1), jnp.stack(fin_im, axis=1),
            jnp.stack(fin_gdn, axis=1))
```

```python
import functools
import math

import numpy as np
import jax
import jax.numpy as jnp
from jax import lax
from jax.experimental import pallas as pl
from jax.experimental.pallas import tpu as pltpu

F32 = jnp.float32
BF16 = jnp.bfloat16
HIGHEST = lax.Precision.HIGHEST

D_MODEL = 1024
S5_W = 512
S5_P = 16
S5_G = 32
S5_N = 64
GDN_W = 512
GDN_DK = 128
GDN_H = 4
CHUNK = 64
N_DIR = 2
N_MOD = 6
D_FF = 4 * D_MODEL
GRID_W = 64
POS_BASE = 10000.0
EPS = 1e-6
S5_T = 16
S5_TP = S5_T * S5_P
N_GATE = 2 * N_DIR * GDN_H
GATE_PAD = 128
TM = 256
TO = 512
GT = 128
SB = 256
NS = 2
VMEM_LIMIT = 56 * 1024 * 1024


def _dot(a, b):
    return jnp.dot(a, b, preferred_element_type=F32)


def _dot_nt(a, b):
    return lax.dot_general(a, b, (((1,), (1,)), ((), ())), preferred_element_type=F32)


def _dot_tn(a, b):
    return lax.dot_general(a, b, (((0,), (0,)), ((), ())), preferred_element_type=F32)


def _sigmoid(x):
    return 1.0 / (1.0 + jnp.exp(-x))


def _silu(x):
    return x * _sigmoid(x)


def _softplus(x):
    return jnp.maximum(x, 0.0) + jnp.log1p(jnp.exp(-jnp.abs(x)))


def _gelu_tanh(x):
    c = math.sqrt(2.0 / math.pi)
    return 0.5 * x * (1.0 + jnp.tanh(c * (x + 0.044715 * (x * x * x))))


def _params(*sem):
    return pltpu.CompilerParams(dimension_semantics=sem, vmem_limit_bytes=VMEM_LIMIT)


def _mod_kernel(c_ref, w_ref, b_ref, o_ref):
    sc = _silu(c_ref[...])
    o_ref[0] = jnp.dot(sc, w_ref[0], precision=HIGHEST, preferred_element_type=F32) + b_ref[0]


def _modulation(cond, w_ada, b_ada):
    depth = w_ada.shape[0]
    tn = 1536
    n_w = N_MOD * D_MODEL
    return pl.pallas_call(
        _mod_kernel,
        grid=(depth, n_w // tn),
        in_specs=[pl.BlockSpec((8, D_MODEL), lambda l, j: (0, 0)),
                  pl.BlockSpec((1, D_MODEL, tn), lambda l, j: (l, 0, j)),
                  pl.BlockSpec((1, 1, tn), lambda l, j: (l, 0, j))],
        out_specs=pl.BlockSpec((1, 8, tn), lambda l, j: (l, 0, j)),
        out_shape=jax.ShapeDtypeStruct((depth, 8, n_w), F32),
        compiler_params=_params("parallel", "parallel"),
        name="modulation",
    )(cond, w_ada, b_ada.reshape(depth, 1, n_w))


def _embed_kernel(x_ref, tr_ref, tc_ref, o_ref, *, rows_per_tile, tiles_per_seq):
    r0 = (pl.program_id(0) % tiles_per_seq) * rows_per_tile
    half = D_MODEL // 2
    tc = tc_ref[...]
    for k in range(rows_per_tile):
        lo, hi = k * GRID_W, (k + 1) * GRID_W
        o_ref[lo:hi, 0:half] = x_ref[lo:hi, 0:half] + tr_ref[pl.ds(r0 + k, 1), :]
        o_ref[lo:hi, half:D_MODEL] = x_ref[lo:hi, half:D_MODEL] + tc


def _embed(x_sample):
    n_seq, length, _ = x_sample.shape
    rows = length // GRID_W
    quarter = D_MODEL // 4
    omega = 1.0 / (POS_BASE ** (jnp.arange(quarter, dtype=F32) / quarter))
    ang_r = jnp.arange(rows, dtype=F32)[:, None] * omega
    ang_c = jnp.arange(GRID_W, dtype=F32)[:, None] * omega
    tab_r = jnp.concatenate([jnp.sin(ang_r), jnp.cos(ang_r)], axis=-1)
    tab_c = jnp.concatenate([jnp.sin(ang_c), jnp.cos(ang_c)], axis=-1)
    rows_per_tile = 8
    tile = rows_per_tile * GRID_W
    tiles_per_seq = length // tile
    x2 = x_sample.reshape(n_seq * length, D_MODEL)
    return pl.pallas_call(
        functools.partial(_embed_kernel, rows_per_tile=rows_per_tile, tiles_per_seq=tiles_per_seq),
        grid=(n_seq * tiles_per_seq,),
        in_specs=[pl.BlockSpec((tile, D_MODEL), lambda i: (i, 0)),
                  pl.BlockSpec((rows, D_MODEL // 2), lambda i: (0, 0)),
                  pl.BlockSpec((GRID_W, D_MODEL // 2), lambda i: (0, 0))],
        out_specs=pl.BlockSpec((tile, D_MODEL), lambda i: (i, 0)),
        out_shape=jax.ShapeDtypeStruct(x2.shape, F32),
        compiler_params=_params("parallel"),
        name="pos_embed",
    )(x2, tab_r, tab_c)


def _norm_mod(x, g, shift, scale):
    ms = jnp.mean(x * x, axis=-1, keepdims=True)
    h = x * lax.rsqrt(ms + EPS) * g
    return h * (1.0 + scale) + shift


LANES = 128
SLOTS = LANES // S5_P


def _regroup_in(u_refs, ug_ref):
    blocks = TM // S5_T
    slot = lax.broadcasted_iota(jnp.int32, (blocks, LANES), 1) // S5_P
    rot = {}
    for j in range(S5_T):
        for a in range(S5_W // LANES):
            t = u_refs[a][pl.ds(j, blocks, stride=S5_T), :]
            rot[j, a] = t if j % SLOTS == 0 else pltpu.roll(t, S5_P * (j % SLOTS), 1)
    for g in range(S5_G):
        a, s = g // SLOTS, g % SLOTS
        for jt in range(S5_T // SLOTS):
            acc = None
            for sig in range(SLOTS):
                piece = rot[SLOTS * jt + (sig - s) % SLOTS, a]
                acc = piece if acc is None else jnp.where(slot == sig, piece, acc)
            ug_ref[g, :, jt * LANES:(jt + 1) * LANES] = acc.astype(ug_ref.dtype)


def _regroup_out(yg_ref, y_refs):
    blocks = yg_ref.shape[1]
    slot = lax.broadcasted_iota(jnp.int32, (blocks, LANES), 1) // S5_P
    for j in range(S5_T):
        jt = j // SLOTS
        for a in range(S5_W // LANES):
            acc = None
            for s in range(SLOTS):
                piece = yg_ref[SLOTS * a + s, :, jt * LANES:(jt + 1) * LANES].astype(F32)
                acc = piece if acc is None else jnp.where(slot == (s + j) % SLOTS, piece, acc)
            if j % SLOTS:
                acc = pltpu.roll(acc, LANES - S5_P * (j % SLOTS), 1)
            y_refs[a][pl.ds(j, blocks, stride=S5_T), :] = acc


def _in_kernel(x_ref, m_ref, g_ref, w_ref, wba_ref, wbat_ref,
               u_ref, ug_ref, z_ref, qkv_ref, zg_ref, ba_ref, bat_ref, *ut_refs):
    m = m_ref[0]
    h = _norm_mod(x_ref[...], g_ref[...], m[:, 0:D_MODEL], m[:, D_MODEL:2 * D_MODEL]).astype(BF16)
    o1, o2, o3, o4 = S5_W, 2 * S5_W, 2 * S5_W + 3 * GDN_W, 2 * S5_W + 4 * GDN_W
    u = _dot(h, w_ref[:, 0:o1])
    u_ref[...] = u.astype(u_ref.dtype)
    for a, ut_ref in enumerate(ut_refs):
        ut_ref[...] = u[:, a * LANES:(a + 1) * LANES]
    _regroup_in(ut_refs, ug_ref)
    z_ref[...] = _dot(h, w_ref[:, o1:o2]).astype(z_ref.dtype)
    qkv_ref[...] = _dot(h, w_ref[:, o2:o3]).astype(qkv_ref.dtype)
    zg_ref[...] = _dot(h, w_ref[:, o3:o4]).astype(zg_ref.dtype)
    ba_ref[...] = _dot(h, wba_ref[...])
    bat_ref[...] = _dot_nt(wbat_ref[...], h)


def _mod_row_map(n_ctx_tiles, tiles_per_lat):
    def index(i):
        return (jnp.where(i < n_ctx_tiles, 0, 1 + (i - n_ctx_tiles) // tiles_per_lat), 0, 0)
    return index


def _in_proj(x, mods_l, g, w_in, w_ba, w_bat, layer, mod_map):
    tokens = x.shape[0]
    const = lambda i: (0, 0)
    tile = lambda n: pl.BlockSpec((TM, n), lambda i: (i, 0))
    weight = lambda w: pl.BlockSpec((None,) + w.shape[1:], lambda i: (layer, 0, 0))
    return pl.pallas_call(
        _in_kernel,
        grid=(tokens // TM,),
        in_specs=[tile(D_MODEL), pl.BlockSpec((1, 1, N_MOD * D_MODEL), mod_map),
                  pl.BlockSpec(g.shape, const), weight(w_in), weight(w_ba), weight(w_bat)],
        out_specs=[tile(S5_W), pl.BlockSpec((S5_G, TM // S5_T, S5_TP), lambda i: (0, i, 0)),
                   tile(S5_W), tile(3 * GDN_W), tile(GDN_W), tile(GATE_PAD),
                   pl.BlockSpec((N_GATE, TM), lambda i: (0, i))],
        out_shape=[jax.ShapeDtypeStruct((tokens, S5_W), BF16),
                   jax.ShapeDtypeStruct((S5_G, tokens // S5_T, S5_TP), BF16),
                   jax.ShapeDtypeStruct((tokens, S5_W), BF16),
                   jax.ShapeDtypeStruct((tokens, 3 * GDN_W), BF16),
                   jax.ShapeDtypeStruct((tokens, GDN_W), BF16),
                   jax.ShapeDtypeStruct((tokens, GATE_PAD), F32),
                   jax.ShapeDtypeStruct((N_GATE, tokens), F32)],
        scratch_shapes=[pltpu.VMEM((TM, LANES), F32)] * (S5_W // LANES),
        compiler_params=_params("parallel"),
        name="in_proj",
    )(x, mods_l, g, w_in, w_ba, w_bat)


N_POW = 8


def _zoh(lam_re, lam_im, log_dt):
    lam_re = jnp.minimum(lam_re, -1e-4)
    dt = jnp.exp(log_dt)
    x, th = lam_re * dt, lam_im * dt
    mag = jnp.exp(x)
    num_re, num_im = mag * jnp.cos(th) - 1.0, mag * jnp.sin(th)
    den = lam_re * lam_re + lam_im * lam_im
    return (x, th, (num_re * lam_re + num_im * lam_im) / den, (num_im * lam_re - num_re * lam_im) / den)


def _power(x, th, e):
    mag = jnp.exp(x * e)
    return mag * jnp.cos(th * e), mag * jnp.sin(th * e)


def _shift_lanes(lo, hi, t, left):
    if t == 0:
        return lo, hi
    lane = lax.broadcasted_iota(jnp.int32, lo.shape, 1)
    zero = jnp.zeros_like(lo)
    r = t % LANES
    if not left:
        if t >= LANES:
            return zero, (lo if r == 0 else jnp.where(lane >= r, pltpu.roll(lo, r, 1), 0.0))
        rl, rh = pltpu.roll(lo, r, 1), pltpu.roll(hi, r, 1)
        return jnp.where(lane >= r, rl, 0.0), jnp.where(lane >= r, rh, rl)
    if t >= LANES:
        return (hi if r == 0 else jnp.where(lane < LANES - r, pltpu.roll(hi, LANES - r, 1), 0.0)), zero
    rl, rh = pltpu.roll(lo, LANES - r, 1), pltpu.roll(hi, LANES - r, 1)
    return jnp.where(lane < LANES - r, rl, rh), jnp.where(lane < LANES - r, rh, 0.0)


def _s5_ops_kernel(prow_ref, pcol_ref, bb_ref, ct_ref, k_ref, bf_ref, bbk_ref, cf_ref, cb_ref, ap_ref,
                   knat_ref):
    s16 = (pl.program_id(1) % SLOTS) * S5_P
    prow = prow_ref[0, 0]
    pcol = pcol_ref[0, 0]
    lane = lax.broadcasted_iota(jnp.int32, (1, 2 * S5_N), 1)
    sign_k = jnp.where(lane < S5_N, 1.0, -1.0)
    rep = jnp.where(lax.broadcasted_iota(jnp.int32, (S5_P, S5_TP), 1) % S5_P
                    == lax.broadcasted_iota(jnp.int32, (S5_P, S5_TP), 0), 1.0, 0.0)
    exact = functools.partial(jnp.dot, precision=HIGHEST, preferred_element_type=F32)
    e_rows = lax.broadcasted_iota(jnp.int32, (S5_T, 1), 0).astype(F32)
    e_lanes = lax.broadcasted_iota(jnp.int32, (1, LANES), 1).astype(F32)
    tok_r = lax.broadcasted_iota(jnp.int32, (S5_TP, S5_T), 0) // S5_P
    e_r = lax.broadcasted_iota(jnp.int32, (S5_TP, S5_T), 1)
    tok_l = lax.broadcasted_iota(jnp.int32, (LANES, S5_TP), 1) // S5_P
    e_l = lax.broadcasted_iota(jnp.int32, (LANES, S5_TP), 0)
    pick = lambda cond: jnp.where(cond, 1.0, 0.0)
    kall = []
    for d in range(N_DIR):
        x_r, th_r, fr_r, fi_r = _zoh(prow[d:d + 1], prow[2 + d:3 + d], prow[4 + d:5 + d])
        x_c, th_c, _, _ = _zoh(pcol[:, d:d + 1], pcol[:, 2 + d:3 + d], pcol[:, 4 + d:5 + d])
        bpack, bswap = bb_ref[0, 0, d, 0], bb_ref[0, 0, d, 1]
        bbar = fr_r * bpack + fi_r * bswap
        bbar_sw = fr_r * bswap - fi_r * bpack
        pr, pi = _power(x_r, th_r, e_rows)
        sel = pick(e_r == ((S5_T - 1) - tok_r if d == 0 else tok_r))
        pr, pi = exact(sel, pr), exact(sel, pi)
        b_nat = pr * jnp.concatenate([bbar] * S5_T, axis=0) + pi * jnp.concatenate([bbar_sw] * S5_T, axis=0)
        b_out = bf_ref if d == 0 else bbk_ref
        for rt in range(S5_TP // LANES):
            b_out[0, 0, rt * LANES:(rt + 1) * LANES, :] = pltpu.roll(
                b_nat[rt * LANES:(rt + 1) * LANES], s16, 0).astype(b_out.dtype)
        c_re = jnp.dot(ct_ref[0, 0, d, 0], rep, precision=HIGHEST, preferred_element_type=F32)
        c_im = jnp.dot(ct_ref[0, 0, d, 1], rep, precision=HIGHEST, preferred_element_type=F32)
        pw_r, pw_i = _power(x_c, th_c, e_lanes)
        sel = pick(e_l == (tok_l + 1 if d == 0 else S5_T - tok_l))
        pr, pi = exact(pw_r, sel), exact(pw_i, sel)
        c_nat = jnp.concatenate([c_re * pr - c_im * pi, -(c_re * pi + c_im * pr)], axis=0)
        c_out = cf_ref if d == 0 else cb_ref
        for ct in range(S5_TP // LANES):
            c_out[0, 0, :, ct * LANES:(ct + 1) * LANES] = pltpu.roll(
                c_nat[:, ct * LANES:(ct + 1) * LANES], s16, 1).astype(c_out.dtype)
        sel = pick(e_l == (tok_l if d == 0 else (S5_T - 1) - tok_l))
        pr, pi = exact(pw_r, sel), exact(pw_i, sel)
        ca = jnp.concatenate([c_re * pr - c_im * pi, c_re * pi + c_im * pr], axis=0)
        kall.append(jnp.dot(bbar * sign_k, ca, precision=HIGHEST, preferred_element_type=F32))
        row = lax.broadcasted_iota(jnp.int32, (2 * N_POW, 1), 0)
        expo = (S5_T * jnp.left_shift(1, row // 2)).astype(F32)
        pr, pi = _power(x_r, th_r, expo)
        ap_ref[0, 0, d * 2 * N_POW:(d + 1) * 2 * N_POW, :] = jnp.where(row % 2 == 0, pr, -pi * sign_k)

    kf_lo, kf_hi = kall[0][:, :LANES], kall[0][:, LANES:]
    kb_lo, kb_hi = kall[1][:, :LANES], kall[1][:, LANES:]
    for j in range(S5_T):
        f_lo, f_hi = _shift_lanes(kf_lo, kf_hi, S5_P * j, False)
        b_lo, b_hi = _shift_lanes(kb_lo, kb_hi, S5_P * (S5_T - 1 - j), True)
        knat_ref[j * S5_P:(j + 1) * S5_P, 0:LANES] = f_lo + b_lo
        knat_ref[j * S5_P:(j + 1) * S5_P, LANES:2 * LANES] = f_hi + b_hi
    for rt in range(S5_TP // LANES):
        for ct in range(S5_TP // LANES):
            tile = knat_ref[rt * LANES:(rt + 1) * LANES, ct * LANES:(ct + 1) * LANES]
            k_ref[0, 0, rt * LANES:(rt + 1) * LANES, ct * LANES:(ct + 1) * LANES] = pltpu.roll(
                pltpu.roll(tile, s16, 0), s16, 1).astype(k_ref.dtype)


def _s5_operators(lam_re, lam_im, log_dt, b_re, b_im, c_re, c_im):
    depth = lam_re.shape[0]
    dup = lambda a: jnp.concatenate([a, a], axis=-1)
    ldt = jnp.broadcast_to(log_dt[..., None], lam_re.shape)
    g_major = lambda a: jnp.moveaxis(a, 1, 2)
    prow = jnp.concatenate([dup(g_major(a)) for a in (lam_re, lam_im, ldt)], axis=2)
    pcol = jnp.concatenate([g_major(a) for a in (lam_re, lam_im, ldt)], axis=2)
    pcol = jnp.pad(jnp.swapaxes(pcol, 2, 3), ((0, 0), (0, 0), (0, 0), (0, 2)))
    bt_re, bt_im = (jnp.swapaxes(g_major(a), 3, 4) for a in (b_re, b_im))
    bb = jnp.stack([jnp.concatenate([bt_re, bt_im], axis=-1),
                    jnp.concatenate([-bt_im, bt_re], axis=-1)], axis=3)
    ct = jnp.stack([jnp.swapaxes(g_major(a), 3, 4) for a in (c_re, c_im)], axis=3)
    per = lambda a: pl.BlockSpec((1, 1) + a.shape[2:], lambda l, g: (l, g) + (0,) * (a.ndim - 2))
    mat = lambda r, c: pl.BlockSpec((1, 1, r, c), lambda l, g: (l, g, 0, 0))
    shape = lambda r, c, dt: jax.ShapeDtypeStruct((depth, S5_G, r, c), dt)
    return pl.pallas_call(
        _s5_ops_kernel,
        grid=(depth, S5_G),
        in_specs=[per(prow), per(pcol), per(bb), per(ct)],
        out_specs=[mat(S5_TP, S5_TP), mat(S5_TP, 2 * S5_N), mat(S5_TP, 2 * S5_N),
                   mat(2 * S5_N, S5_TP), mat(2 * S5_N, S5_TP), mat(N_DIR * 2 * N_POW, 2 * S5_N)],
        out_shape=[shape(S5_TP, S5_TP, BF16), shape(S5_TP, 2 * S5_N, BF16), shape(S5_TP, 2 * S5_N, BF16),
                   shape(2 * S5_N, S5_TP, BF16), shape(2 * S5_N, S5_TP, BF16),
                   shape(N_DIR * 2 * N_POW, 2 * S5_N, F32)],
        scratch_shapes=[pltpu.VMEM((S5_TP, S5_TP), F32)],
        compiler_params=_params("parallel", "parallel"),
        name="s5_operators",
    )(prow, pcol, bb, ct)


def _lane_swap(x):
    return pltpu.roll(x, S5_N, 1)


def _cmul(a1, a2, h):
    return a1 * h + a2 * _lane_swap(h)


def _block_scan(s, seg, a_pow, d, reverse):
    rows = s.shape[0]
    ridx = lax.broadcasted_iota(jnp.int32, s.shape, 0) % seg
    h = s
    k, step = 0, 1
    while step < seg:
        a1 = a_pow[d * 16 + 2 * k:d * 16 + 2 * k + 1]
        a2 = a_pow[d * 16 + 2 * k + 1:d * 16 + 2 * k + 2]
        if reverse:
            sh = jnp.where(ridx < seg - step, pltpu.roll(h, rows - step, 0), 0.0)
        else:
            sh = jnp.where(ridx >= step, pltpu.roll(h, step, 0), 0.0)
        h = h + _cmul(a1, a2, sh)
        k += 1
        step *= 2
    return h


def _s5_kernel(u_ref, k_ref, bf_ref, bb_ref, cf_ref, cb_ref, ap_ref, h0_ref,
               y_ref, fin_ref, hf_ref, hb_ref, *, ctx_rows, ctx_seg, lat_seg, n_ctx, n_lat):
    u = u_ref[0]
    a_pow = ap_ref[0]
    s_f = _dot(u, bf_ref[0])
    s_b = _dot(u, bb_ref[0])
    lat_rows = u.shape[0] - ctx_rows

    lrow = lax.broadcasted_iota(jnp.int32, (lat_rows, 2 * S5_N), 0)
    lmod = lrow % lat_seg
    lseq = lrow // lat_seg
    h0 = h0_ref[0]
    h0_f = jnp.zeros((lat_rows, 2 * S5_N), F32)
    h0_b = jnp.zeros((lat_rows, 2 * S5_N), F32)
    for s in range(n_lat):
        h0_f = jnp.where(lseq == s, h0[2 * s:2 * s + 1], h0_f)
        h0_b = jnp.where(lseq == s, h0[2 * s + 1:2 * s + 2], h0_b)
    first, last = lmod == 0, lmod == lat_seg - 1
    sl_f = s_f[ctx_rows:] + jnp.where(first, _cmul(a_pow[0:1], a_pow[1:2], h0_f), 0.0)
    sl_b = s_b[ctx_rows:] + jnp.where(last, _cmul(a_pow[16:17], a_pow[17:18], h0_b), 0.0)

    hc_f = _block_scan(s_f[:ctx_rows], ctx_seg, a_pow, 0, False)
    hc_b = _block_scan(s_b[:ctx_rows], ctx_seg, a_pow, 1, True)
    hl_f = _block_scan(sl_f, lat_seg, a_pow, 0, False)
    hl_b = _block_scan(sl_b, lat_seg, a_pow, 1, True)

    cmod = lax.broadcasted_iota(jnp.int32, (ctx_rows, 2 * S5_N), 0) % ctx_seg
    pc_f = jnp.where(cmod == 0, 0.0, pltpu.roll(hc_f, 1, 0))
    pc_b = jnp.where(cmod == ctx_seg - 1, 0.0, pltpu.roll(hc_b, ctx_rows - 1, 0))
    pl_f = jnp.where(first, h0_f, pltpu.roll(hl_f, 1, 0))
    pl_b = jnp.where(last, h0_b, pltpu.roll(hl_b, lat_rows - 1, 0))
    prev_f = jnp.concatenate([pc_f, pl_f], axis=0).astype(BF16)
    prev_b = jnp.concatenate([pc_b, pl_b], axis=0).astype(BF16)
    y_ref[0] = (_dot(u, k_ref[0]) + _dot(prev_f, cf_ref[0]) + _dot(prev_b, cb_ref[0])).astype(y_ref.dtype)

    hf_ref[...] = hc_f
    hb_ref[...] = hc_b
    fin_ref[0, 0] = hf_ref[pl.ds(ctx_seg - 1, n_ctx, stride=ctx_seg), :]
    fin_ref[0, 1] = hb_ref[pl.ds(0, n_ctx, stride=ctx_seg), :]


def _s5_mixer(u_g, ops, h0, n_ctx, ctx_len, n_lat, lat_len):
    k_mat, b_f, b_b, c_f, c_b, a_pow = ops
    rows = u_g.shape[1]
    ctx_seg, lat_seg = ctx_len // S5_T, lat_len // S5_T
    ctx_rows = n_ctx * ctx_seg
    per_g = lambda a: pl.BlockSpec((1,) + a.shape[1:], lambda g: (g,) + (0,) * (a.ndim - 1))
    kern = functools.partial(_s5_kernel, ctx_rows=ctx_rows, ctx_seg=ctx_seg, lat_seg=lat_seg,
                             n_ctx=n_ctx, n_lat=n_lat)
    return pl.pallas_call(
        kern,
        grid=(S5_G,),
        in_specs=[per_g(a) for a in (u_g, k_mat, b_f, b_b, c_f, c_b, a_pow, h0)],
        out_specs=[pl.BlockSpec((1, rows, S5_TP), lambda g: (g, 0, 0)),
                   pl.BlockSpec((1, N_DIR, n_ctx, 2 * S5_N), lambda g: (g, 0, 0, 0))],
        out_shape=[jax.ShapeDtypeStruct((S5_G, rows, S5_TP), BF16),
                   jax.ShapeDtypeStruct((S5_G, N_DIR, n_ctx, 2 * S5_N), F32)],
        scratch_shapes=[pltpu.VMEM((ctx_rows, 2 * S5_N), F32)] * N_DIR,
        compiler_params=_params("parallel"),
        name="s5_mixer",
    )(u_g, k_mat, b_f, b_b, c_f, c_b, a_pow, h0)


def _gdn_pre_kernel(x_ref, hp_ref, hn_ref, cw_ref, ba_ref, bat_ref, al_ref, dl_ref, ar_ref, dr_ref,
                    u_ref, wq_ref, kd_ref, a_ref, eg_ref,
                    q_ref, k_ref, v_ref, col_ref, row_ref, m_ref, acc_ref):
    x = x_ref[...].astype(F32)
    rid = lax.broadcasted_iota(jnp.int32, x.shape, 0)
    x_prev = jnp.where(rid == 0, hp_ref[0].astype(F32), pltpu.roll(x, 1, 0))
    x_next = jnp.where(rid == GT - 1, hn_ref[0].astype(F32), pltpu.roll(x, GT - 1, 0))
    cw = cw_ref[...]
    y = _silu(x_prev * cw[0:1] + x * cw[1:2] + x_next * cw[2:3])
    for h in range(GDN_H):
        lo, hi = h * GDN_DK, (h + 1) * GDN_DK
        qh = y[:, lo:hi]
        kh = y[:, GDN_W + lo:GDN_W + hi]
        q_ref[:, lo:hi] = qh * (lax.rsqrt(jnp.sum(qh * qh, axis=-1, keepdims=True) + EPS)
                                * (GDN_DK ** -0.5))
        k_ref[:, lo:hi] = kh * lax.rsqrt(jnp.sum(kh * kh, axis=-1, keepdims=True) + EPS)
    v_ref[...] = y[:, 2 * GDN_W:]

    half = N_GATE // 2
    ba = ba_ref[...]
    lane = lax.broadcasted_iota(jnp.int32, ba.shape, 1)
    gate = jnp.where(lane < half, _sigmoid(ba), -jnp.exp(al_ref[...]) * _softplus(ba + dl_ref[...]))
    ri = lax.broadcasted_iota(jnp.int32, (GT, GT), 0)
    ci = lax.broadcasted_iota(jnp.int32, (GT, GT), 1)
    same = (ri // CHUNK) == (ci // CHUNK)
    lower = jnp.where(same & (ci <= ri), 1.0, 0.0)
    upper = jnp.where(same & (ci >= ri), 1.0, 0.0)
    cum_f = jnp.dot(lower, gate, precision=HIGHEST, preferred_element_type=F32)
    cum_b = jnp.dot(upper, gate, precision=HIGHEST, preferred_element_type=F32)
    fwd_lane = (lane >= half) & (lane < half + GDN_H)
    bwd_lane = (lane >= half + GDN_H) & (lane < N_GATE)
    col_ref[...] = jnp.where(fwd_lane, cum_f, jnp.where(bwd_lane, cum_b, gate))
    bat = bat_ref[...]
    g_row = -jnp.exp(ar_ref[...]) * _softplus(bat + dr_ref[...])
    row_f = jnp.dot(g_row, upper, precision=HIGHEST, preferred_element_type=F32)
    row_b = jnp.dot(g_row, lower, precision=HIGHEST, preferred_element_type=F32)
    srow = lax.broadcasted_iota(jnp.int32, bat.shape, 0)
    row_ref[...] = jnp.where(srow < half + GDN_H, row_f, row_b)

    assert 2 * CHUNK == LANES and GT == 2 * CHUNK
    ii = lax.broadcasted_iota(jnp.int32, (CHUNK, LANES), 0)
    ll = lax.broadcasted_iota(jnp.int32, (CHUNK, LANES), 1)
    jj = ll % CHUNK
    fwd = ll < CHUNK
    incl = (fwd & (jj <= ii)) | (~fwd & (jj >= ii))
    strict = (fwd & (jj < ii)) | (~fwd & (jj > ii))
    fwd1 = lax.broadcasted_iota(jnp.int32, (1, LANES), 1) < CHUNK
    keep_f = jnp.where(fwd, 1.0, 0.0).astype(BF16)
    keep_b = jnp.where(fwd, 0.0, 1.0).astype(BF16)
    pairs = [(c, h) for c in range(GT // CHUNK) for h in range(GDN_H)]

    def both(c, lane_f, lane_b):
        rows = slice(c * CHUNK, (c + 1) * CHUNK)
        return jnp.where(fwd, col_ref[rows, lane_f:lane_f + 1], col_ref[rows, lane_b:lane_b + 1])

    def block_diag(x16):
        return jnp.concatenate([x16 * keep_f, x16 * keep_b], axis=0)

    for n, (c, h) in enumerate(pairs):
        rows = slice(c * CHUNK, (c + 1) * CHUNK)
        cols = slice(h * GDN_DK, (h + 1) * GDN_DK)
        beta = both(c, h, GDN_H + h)
        gc = both(c, half + h, half + GDN_H + h)
        r_f = row_ref[half + h:half + h + 1, :]
        r_b = row_ref[half + GDN_H + h:half + GDN_H + h + 1, :]
        gr = (jnp.where(fwd1, r_f, pltpu.roll(r_b, CHUNK, 1)) if c == 0
              else jnp.where(fwd1, pltpu.roll(r_f, CHUNK, 1), r_b))
        decay = jnp.where(incl, jnp.exp(jnp.where(incl, gc - gr, 0.0)), 0.0)
        k16 = k_ref[rows, cols].astype(BF16)
        kk16 = jnp.concatenate([k16, k16], axis=0)
        m = jnp.where(strict, beta * _dot_nt(k16, kk16) * decay, 0.0)
        a_ref[rows, cols] = (_dot_nt(q_ref[rows, cols].astype(BF16), kk16) * decay).astype(BF16)
        m_ref[n] = m
        acc_ref[n] = jnp.where((ii // 2) == (jj // 2), -m, 0.0)

    b = 2
    while b < CHUNK:
        pair = ((ii // (2 * b)) == (jj // (2 * b))) & ((ii // b) != (jj // b))
        for n in range(len(pairs)):
            cpart = jnp.where(pair, m_ref[n], 0.0)
            acc = acc_ref[n]
            a16 = acc.astype(BF16)
            y = cpart + _dot(a16, block_diag(cpart.astype(BF16)))
            acc_ref[n] = acc - y - _dot(y.astype(BF16), block_diag(a16))
        b *= 2

    zeros = jnp.zeros((CHUNK, 2 * GDN_DK), BF16)
    for n, (c, h) in enumerate(pairs):
        rows = slice(c * CHUNK, (c + 1) * CHUNK)
        cols = slice(h * GDN_DK, (h + 1) * GDN_DK)
        kh, vh, qh = k_ref[rows, cols], v_ref[rows, cols], q_ref[rows, cols]
        rhs, egc, gcs = [], [], []
        for d in range(N_DIR):
            lane_ = d * GDN_H + h
            beta = col_ref[rows, lane_:lane_ + 1]
            gc = col_ref[rows, half + lane_:half + lane_ + 1]
            gcs.append(gc)
            egc.append(jnp.exp(gc))
            rhs.append(jnp.concatenate([vh * beta, kh * beta * egc[d]], axis=1))
        stacked = jnp.concatenate(
            [jnp.concatenate([rhs[0].astype(BF16), zeros], axis=1),
             jnp.concatenate([zeros, rhs[1].astype(BF16)], axis=1)], axis=0)
        prod = _dot(acc_ref[n].astype(BF16), stacked)
        for d in range(N_DIR):
            sol = rhs[d] + prod[:, d * 2 * GDN_DK:(d + 1) * 2 * GDN_DK]
            u_ref[d, rows, cols] = sol[:, :GDN_DK]
            wq_ref[d, c, 0:CHUNK, cols] = sol[:, GDN_DK:].astype(BF16)
            wq_ref[d, c, CHUNK:2 * CHUNK, cols] = (qh * egc[d]).astype(BF16)
            g_last = gcs[d][0:1] if d == 1 else gcs[d][CHUNK - 1:CHUNK]
            kd_ref[d, rows, cols] = (kh * jnp.exp(g_last - gcs[d])).astype(BF16)
            eg_ref[c, d * GDN_H + h:d * GDN_H + h + 1, :] = jnp.broadcast_to(jnp.exp(g_last), (1, GDN_DK))


def _gdn_pre(qkv, halo_prev, halo_next, conv_w, ba, bat, a_log, dt_bias):
    tokens = qkv.shape[0]
    n_tiles = tokens // GT
    cpt = GT // CHUNK
    half = N_GATE // 2
    al = a_log.reshape(-1)
    db = dt_bias.reshape(-1)
    al_l = jnp.zeros((1, GATE_PAD), F32).at[0, half:N_GATE].set(al)
    db_l = jnp.zeros((1, GATE_PAD), F32).at[0, half:N_GATE].set(db)
    al_r = jnp.zeros((N_GATE, 1), F32).at[half:, 0].set(al)
    db_r = jnp.zeros((N_GATE, 1), F32).at[half:, 0].set(db)
    const = lambda i: (0, 0)
    tile = lambda n: pl.BlockSpec((GT, n), lambda i: (i, 0))
    halo = pl.BlockSpec((1, 1, 3 * GDN_W), lambda i: (i, 0, 0))
    n_chunks = tokens // CHUNK
    n_pairs = cpt * GDN_H
    return pl.pallas_call(
        _gdn_pre_kernel,
        grid=(n_tiles,),
        in_specs=[tile(3 * GDN_W), halo, halo, pl.BlockSpec(conv_w.shape, const),
                  tile(GATE_PAD), pl.BlockSpec((N_GATE, GT), lambda i: (0, i)),
                  pl.BlockSpec((1, GATE_PAD), const), pl.BlockSpec((1, GATE_PAD), const),
                  pl.BlockSpec((N_GATE, 1), const), pl.BlockSpec((N_GATE, 1), const)],
        out_specs=[pl.BlockSpec((N_DIR, GT, GDN_W), lambda i: (0, i, 0)),
                   pl.BlockSpec((N_DIR, cpt, 2 * CHUNK, GDN_W), lambda i: (0, i, 0, 0)),
                   pl.BlockSpec((N_DIR, GT, GDN_W), lambda i: (0, i, 0)),
                   pl.BlockSpec((GT, GDN_W), lambda i: (i, 0)),
                   pl.BlockSpec((cpt, N_DIR * GDN_H, GDN_DK), lambda i: (i, 0, 0))],
        out_shape=[jax.ShapeDtypeStruct((N_DIR, tokens, GDN_W), F32),
                   jax.ShapeDtypeStruct((N_DIR, n_chunks, 2 * CHUNK, GDN_W), BF16),
                   jax.ShapeDtypeStruct((N_DIR, tokens, GDN_W), BF16),
                   jax.ShapeDtypeStruct((tokens, GDN_W), BF16),
                   jax.ShapeDtypeStruct((n_chunks, N_DIR * GDN_H, GDN_DK), F32)],
        scratch_shapes=[pltpu.VMEM((GT, GDN_W), F32)] * 3
                       + [pltpu.VMEM((GT, GATE_PAD), F32), pltpu.VMEM((N_GATE, GT), F32),
                          pltpu.VMEM((n_pairs, CHUNK, N_DIR * CHUNK), F32),
                          pltpu.VMEM((n_pairs, CHUNK, N_DIR * CHUNK), F32)],
        compiler_params=_params("parallel"),
        name="gdn_pre",
    )(qkv, halo_prev, halo_next, conv_w, ba, bat, al_l, db_l, al_r, db_r)


def _gdn_scan_kernel(*refs, has_s0, has_fin):
    per_dir = 5
    pos = NS * N_DIR * per_dir
    ins = [refs[k * per_dir:(k + 1) * per_dir] for k in range(NS * N_DIR)]
    s0_ref = None
    if has_s0:
        s0_ref = refs[pos]
        pos += 1
    of_ref, ob_ref = refs[pos], refs[pos + 1]
    pos += 2
    fin_ref = None
    if has_fin:
        fin_ref = refs[pos]
        pos += 1
    s_ref = refs[pos]
    j = pl.program_id(1)

    @pl.when(j == 0)
    def _():
        if has_s0:
            s_ref[...] = s0_ref[...]
        else:
            s_ref[...] = jnp.zeros(s_ref.shape, F32)

    cpb = SB // CHUNK
    for step in range(cpb):
        work = []
        for p in range(NS):
            for d in range(N_DIR):
                u_r, wq_r, kd_r, a_r, eg_r = ins[p * N_DIR + d]
                o_r = of_ref if d == 0 else ob_ref
                c = step if d == 0 else cpb - 1 - step
                rows = slice(c * CHUNK, (c + 1) * CHUNK)
                for h in range(GDN_H):
                    work.append((p, d, h, c, rows, slice(h * GDN_DK, (h + 1) * GDN_DK), u_r, wq_r, kd_r,
                                 a_r, eg_r, o_r))
        ps = [_dot(wq_r[0, c, :, cols], s_ref[p, d, h].astype(BF16))
              for (p, d, h, c, rows, cols, u_r, wq_r, kd_r, a_r, eg_r, o_r) in work]
        v16 = [(u_r[0, rows, cols] - q[:CHUNK]).astype(BF16)
               for q, (p, d, h, c, rows, cols, u_r, wq_r, kd_r, a_r, eg_r, o_r) in zip(ps, work)]
        zero = jnp.zeros((CHUNK, GDN_DK), BF16)
        for q, v, (p, d, h, c, rows, cols, u_r, wq_r, kd_r, a_r, eg_r, o_r) in zip(ps, v16, work):
            v2 = jnp.concatenate([v, zero] if d == 0 else [zero, v], axis=0)
            o_r[0, p, rows, cols] = (q[CHUNK:] + _dot(a_r[rows, cols], v2)).astype(o_r.dtype)
        for v, (p, d, h, c, rows, cols, u_r, wq_r, kd_r, a_r, eg_r, o_r) in zip(v16, work):
            eg = eg_r[c, d * GDN_H + h:d * GDN_H + h + 1, :]
            s_ref[p, d, h] = s_ref[p, d, h] * eg + _dot_tn(kd_r[0, rows, cols], v)

    if has_fin:
        @pl.when(j == pl.num_programs(1) - 1)
        def _():
            fin_ref[...] = s_ref[...]


def _gdn_scan(u, wq, kd, a, eg, s0, n_seq, length, base_tok, want_fin):
    assert n_seq % NS == 0
    n_b = length // SB
    cpb = SB // CHUNK
    base = base_tok // SB
    in_specs, args = [], []
    for p in range(NS):
        for d in range(N_DIR):
            if d == 0:
                blk = lambda s, j, p=p: base + (s * NS + p) * n_b + j
            else:
                blk = lambda s, j, p=p: base + (s * NS + p) * n_b + (n_b - 1 - j)
            in_specs += [
                pl.BlockSpec((1, SB, GDN_W), lambda s, j, d=d, blk=blk: (d, blk(s, j), 0)),
                pl.BlockSpec((1, cpb, 2 * CHUNK, GDN_W), lambda s, j, d=d, blk=blk: (d, blk(s, j), 0, 0)),
                pl.BlockSpec((1, SB, GDN_W), lambda s, j, d=d, blk=blk: (d, blk(s, j), 0)),
                pl.BlockSpec((SB, GDN_W), lambda s, j, blk=blk: (blk(s, j), 0)),
                pl.BlockSpec((cpb, N_DIR * GDN_H, GDN_DK), lambda s, j, blk=blk: (blk(s, j), 0, 0))]
            args += [u, wq, kd, a, eg]
    state_spec = pl.BlockSpec((NS, N_DIR, GDN_H, GDN_DK, GDN_DK), lambda s, j: (s, 0, 0, 0, 0))
    if s0 is not None:
        in_specs.append(state_spec)
        args.append(s0)
    out_specs = [pl.BlockSpec((1, NS, SB, GDN_W), lambda s, j: (s, 0, j, 0)),
                 pl.BlockSpec((1, NS, SB, GDN_W), lambda s, j: (s, 0, n_b - 1 - j, 0))]
    out_shape = [jax.ShapeDtypeStruct((n_seq // NS, NS, length, GDN_W), BF16)] * 2
    if want_fin:
        out_specs.append(state_spec)
        out_shape.append(jax.ShapeDtypeStruct((n_seq, N_DIR, GDN_H, GDN_DK, GDN_DK), F32))
    outs = pl.pallas_call(
        functools.partial(_gdn_scan_kernel, has_s0=s0 is not None, has_fin=want_fin),
        grid=(n_seq // NS, n_b),
        in_specs=in_specs,
        out_specs=out_specs,
        out_shape=out_shape,
        scratch_shapes=[pltpu.VMEM((NS, N_DIR, GDN_H, GDN_DK, GDN_DK), F32)],
        compiler_params=_params("parallel", "arbitrary"),
        name="gdn_scan",
    )(*args)
    return [o.reshape(n_seq * length, GDN_W) for o in outs[:2]] + list(outs[2:])


def _out_mlp_kernel(x_ref, m_ref, yg_ref, u_ref, z_ref, dsk_ref, ofc_ref, obc_ref, ofl_ref, obl_ref,
                    zg_ref, gn_ref, w_ref, g2_ref, w1_ref, w2_ref, o_ref, *y5_refs, n_ctx_tiles):
    m = m_ref[0]
    _regroup_out(yg_ref, y5_refs)
    in_ctx = pl.program_id(0) < n_ctx_tiles
    gn = gn_ref[...]
    halves = [slice(k * TO // 2, (k + 1) * TO // 2) for k in range(2)]
    xs, hs = [], []
    for r in halves:
        up = lambda ref: ref[r, :].astype(F32)
        y5 = jnp.concatenate([ref[r, :] for ref in y5_refs], axis=1)
        s5 = _gelu_tanh(y5 + up(u_ref) * dsk_ref[...]) * _sigmoid(up(z_ref))
        acc = _dot(s5.astype(BF16), w_ref[0:S5_W, :])
        o = jnp.where(in_ctx, up(ofc_ref) + up(obc_ref), up(ofl_ref) + up(obl_ref))
        zg = up(zg_ref)
        for h in range(GDN_H):
            lo, hi = h * GDN_DK, (h + 1) * GDN_DK
            oh = o[:, lo:hi]
            nh = oh * lax.rsqrt(jnp.mean(oh * oh, axis=-1, keepdims=True) + EPS) * gn
            gh = (nh * _silu(zg[:, lo:hi])).astype(BF16)
            acc = acc + _dot(gh, w_ref[S5_W + lo:S5_W + hi, :])
        x = x_ref[r, :] + m[:, 2 * D_MODEL:3 * D_MODEL] * acc
        xs.append(x)
        hs.append(_norm_mod(x, g2_ref[...], m[:, 3 * D_MODEL:4 * D_MODEL],
                            m[:, 4 * D_MODEL:5 * D_MODEL]).astype(BF16))
    for r, x, h in zip(halves, xs, hs):
        ff = jnp.maximum(_dot(h, w1_ref[...]), 0.0)
        ff = (ff * ff).astype(BF16)
        o_ref[r, :] = x + m[:, 5 * D_MODEL:6 * D_MODEL] * _dot(ff, w2_ref[...])


def _out_mlp(x, mods_l, y_g, u, z, d_skip, o_ctx, o_lat, zg, gdn_norm, w_out, g2, w1, w2, layer,
             ctx_tok, lat_len):
    tokens = x.shape[0]
    assert ctx_tok % TO == 0 and lat_len % TO == 0 and tokens > ctx_tok
    n_ctx_tiles = ctx_tok // TO
    const = lambda i: (0, 0)
    tile = lambda n: pl.BlockSpec((TO, n), lambda i: (i, 0))
    ctx_tile = pl.BlockSpec((TO, GDN_W), lambda i: (jnp.minimum(i, n_ctx_tiles - 1), 0))
    lat_tile = pl.BlockSpec((TO, GDN_W), lambda i: (jnp.maximum(i - n_ctx_tiles, 0), 0))
    weight = lambda w: pl.BlockSpec((None,) + w.shape[1:], lambda i: (layer, 0, 0),
                                    pipeline_mode=pl.Buffered(1))
    return pl.pallas_call(
        functools.partial(_out_mlp_kernel, n_ctx_tiles=n_ctx_tiles),
        grid=(tokens // TO,),
        in_specs=[tile(D_MODEL),
                  pl.BlockSpec((1, 1, N_MOD * D_MODEL), _mod_row_map(n_ctx_tiles, lat_len // TO)),
                  pl.BlockSpec((S5_G, TO // S5_T, S5_TP), lambda i: (0, i, 0)),
                  tile(S5_W), tile(S5_W), pl.BlockSpec((1, S5_W), const),
                  ctx_tile, ctx_tile, lat_tile, lat_tile,
                  tile(GDN_W), pl.BlockSpec((1, GDN_DK), const),
                  weight(w_out), pl.BlockSpec((1, D_MODEL), const), weight(w1), weight(w2)],
        out_specs=tile(D_MODEL),
        out_shape=jax.ShapeDtypeStruct((tokens, D_MODEL), F32),
        scratch_shapes=[pltpu.VMEM((TO, LANES), F32)] * (S5_W // LANES),
        compiler_params=_params("parallel"),
        name="out_mlp",
    )(x, mods_l, y_g, u, z, d_skip, *o_ctx, *o_lat, zg, gdn_norm, w_out, g2, w1, w2)


def _final_kernel(x_ref, g_ref, o_ref):
    x = x_ref[...]
    o_ref[...] = x * lax.rsqrt(jnp.mean(x * x, axis=-1, keepdims=True) + EPS) * g_ref[...]


def _final_norm(x, g, first_tile, n_tiles):
    return pl.pallas_call(
        _final_kernel,
        grid=(n_tiles,),
        in_specs=[pl.BlockSpec((TM, D_MODEL), lambda i: (first_tile + i, 0)),
                  pl.BlockSpec((1, D_MODEL), lambda i: (0, 0))],
        out_specs=pl.BlockSpec((TM, D_MODEL), lambda i: (i, 0)),
        out_shape=jax.ShapeDtypeStruct((n_tiles * TM, D_MODEL), F32),
        compiler_params=_params("parallel"),
        name="final_norm",
    )(x, g)


def kernel(x_prompt, x_sample, state_s5_re, state_s5_im, state_gdn, c, c_ctx, norm_mix, norm_mlp, w_ada, b_ada, w_in, conv_qkv, s5_lambda_re, s5_lambda_im, s5_log_dt, s5_b_re, s5_b_im, s5_c_re, s5_c_im, s5_d, gdn_a_log, gdn_dt_bias, gdn_norm, w_out, w_mlp_in, w_mlp_out, norm_final):
    n_ctx, ctx_len, _ = x_prompt.shape
    n_lat, lat_len, _ = x_sample.shape
    depth = w_in.shape[0]
    assert ctx_len % TM == 0 and lat_len % TM == 0 and lat_len % (8 * GRID_W) == 0
    assert n_lat + 1 <= 8
    ctx_tok, lat_tok = n_ctx * ctx_len, n_lat * lat_len
    tokens = ctx_tok + lat_tok
    n_ctx_tiles, tiles_per_lat = ctx_tok // TM, lat_len // TM
    mod_map = _mod_row_map(n_ctx_tiles, tiles_per_lat)

    cond = jnp.zeros((8, D_MODEL), F32).at[0].set(c_ctx).at[1:1 + n_lat].set(c)
    mods = _modulation(cond, w_ada, b_ada)
    x = jnp.concatenate([x_prompt.reshape(ctx_tok, D_MODEL), _embed(x_sample)], axis=0)

    starts = np.arange(tokens // GT) * GT
    in_ctx = starts < ctx_tok
    pos = np.where(in_ctx, starts % ctx_len, (starts - ctx_tok) % lat_len)
    has_prev = pos != 0
    has_next = pos + GT != np.where(in_ctx, ctx_len, lat_len)

    s5_ops = _s5_operators(s5_lambda_re, s5_lambda_im, s5_log_dt, s5_b_re, s5_b_im, s5_c_re, s5_c_im)
    w_in16, w_out16 = w_in.astype(BF16), w_out.astype(BF16)
    w_mlp_in16, w_mlp_out16 = w_mlp_in.astype(BF16), w_mlp_out.astype(BF16)
    gate_cols = w_in16[:, :, 2 * S5_W + 4 * GDN_W:]
    w_ba = jnp.pad(gate_cols, ((0, 0), (0, 0), (0, GATE_PAD - N_GATE)))
    w_bat = jnp.swapaxes(gate_cols, 1, 2)
    fin_re, fin_im, fin_gdn = [], [], []
    for l in range(depth):
        mods_l = mods[l].reshape(8, 1, N_MOD * D_MODEL)
        u, u_g, z, qkv, zg, ba, bat = _in_proj(x, mods_l, norm_mix[l].reshape(1, D_MODEL), w_in16,
                                               w_ba, w_bat, l, mod_map)

        h0 = jnp.concatenate([state_s5_re[:, l], state_s5_im[:, l]], axis=-1)
        h0 = h0.transpose(2, 0, 1, 3).reshape(S5_G, n_lat * N_DIR, 2 * S5_N)
        y_g, s5_fin = _s5_mixer(u_g, tuple(o[l] for o in s5_ops), h0, n_ctx, ctx_len, n_lat, lat_len)
        s5_fin = s5_fin.transpose(2, 1, 0, 3)
        fin_re.append(s5_fin[..., :S5_N])
        fin_im.append(s5_fin[..., S5_N:])

        qkv_t = qkv.reshape(tokens // GT, GT, 3 * GDN_W)
        zero_row = jnp.zeros((1, 3 * GDN_W), qkv.dtype)
        prev_rows = jnp.concatenate([zero_row, qkv_t[:-1, GT - 1]], axis=0)
        next_rows = jnp.concatenate([qkv_t[1:, 0], zero_row], axis=0)
        halo_prev = jnp.where(has_prev[:, None], prev_rows, 0.0)[:, None, :]
        halo_next = jnp.where(has_next[:, None], next_rows, 0.0)[:, None, :]
        pre = _gdn_pre(qkv, halo_prev, halo_next, conv_qkv[l], ba, bat, gdn_a_log[l], gdn_dt_bias[l])
        of_c, ob_c, gdn_fin = _gdn_scan(*pre, None, n_ctx, ctx_len, 0, True)
        of_l, ob_l = _gdn_scan(*pre, state_gdn[:, l], n_lat, lat_len, ctx_tok, False)
        fin_gdn.append(gdn_fin)

        x = _out_mlp(x, mods_l, y_g, u, z, s5_d[l].reshape(1, S5_W), (of_c, ob_c), (of_l, ob_l), zg,
                     gdn_norm[l].reshape(1, GDN_DK), w_out16, norm_mlp[l].reshape(1, D_MODEL),
                     w_mlp_in16, w_mlp_out16, l, ctx_tok, lat_len)

    g_fin = norm_final.reshape(1, D_MODEL)
    y_prompt = _final_norm(x, g_fin, 0, n_ctx_tiles).reshape(n_ctx, ctx_len, D_MODEL)
    y_sample = _final_norm(x, g_fin, n_ctx_tiles, lat_tok // TM).reshape(n_lat, lat_len, D_MODEL)
    return (y_prompt, y_sample, jnp.stack(fin_re, axis=1), jnp.stack(fin_im, axis=1),
            jnp.stack(fin_gdn, axis=1))
```

```python
import functools
import math

import numpy as np
import jax
import jax.numpy as jnp
from jax import lax
from jax.experimental import pallas as pl
from jax.experimental.pallas import tpu as pltpu

F32 = jnp.float32
BF16 = jnp.bfloat16
HIGHEST = lax.Precision.HIGHEST

D_MODEL = 1024
S5_W = 512
S5_P = 16
S5_G = 32
S5_N = 64
GDN_W = 512
GDN_DK = 128
GDN_H = 4
CHUNK = 64
N_DIR = 2
N_MOD = 6
D_FF = 4 * D_MODEL
GRID_W = 64
POS_BASE = 10000.0
EPS = 1e-6
S5_T = 16
S5_TP = S5_T * S5_P
N_GATE = 2 * N_DIR * GDN_H
GATE_PAD = 128
TM = 256
TO = 512
GT = 128
SB = 256
NS = 2
VMEM_LIMIT = 56 * 1024 * 1024


def _dot(a, b):
    return jnp.dot(a, b, preferred_element_type=F32)


def _dot_nt(a, b):
    return lax.dot_general(a, b, (((1,), (1,)), ((), ())), preferred_element_type=F32)


def _dot_tn(a, b):
    return lax.dot_general(a, b, (((0,), (0,)), ((), ())), preferred_element_type=F32)


def _sigmoid(x):
    return 1.0 / (1.0 + jnp.exp(-x))


def _silu(x):
    return x * _sigmoid(x)


def _softplus(x):
    return jnp.maximum(x, 0.0) + jnp.log1p(jnp.exp(-jnp.abs(x)))


def _gelu_tanh(x):
    c = math.sqrt(2.0 / math.pi)
    return 0.5 * x * (1.0 + jnp.tanh(c * (x + 0.044715 * (x * x * x))))


def _params(*sem):
    return pltpu.CompilerParams(dimension_semantics=sem, vmem_limit_bytes=VMEM_LIMIT)


def _mod_kernel(c_ref, w_ref, b_ref, o_ref):
    sc = _silu(c_ref[...])
    o_ref[0] = jnp.dot(sc, w_ref[0], precision=HIGHEST, preferred_element_type=F32) + b_ref[0]


def _modulation(cond, w_ada, b_ada):
    depth = w_ada.shape[0]
    tn = 1536
    n_w = N_MOD * D_MODEL
    return pl.pallas_call(
        _mod_kernel,
        grid=(depth, n_w // tn),
        in_specs=[pl.BlockSpec((8, D_MODEL), lambda l, j: (0, 0)),
                  pl.BlockSpec((1, D_MODEL, tn), lambda l, j: (l, 0, j)),
                  pl.BlockSpec((1, 1, tn), lambda l, j: (l, 0, j))],
        out_specs=pl.BlockSpec((1, 8, tn), lambda l, j: (l, 0, j)),
        out_shape=jax.ShapeDtypeStruct((depth, 8, n_w), F32),
        compiler_params=_params("parallel", "parallel"),
        name="modulation",
    )(cond, w_ada, b_ada.reshape(depth, 1, n_w))


def _embed_kernel(x_ref, tr_ref, tc_ref, o_ref, *, rows_per_tile, tiles_per_seq):
    r0 = (pl.program_id(0) % tiles_per_seq) * rows_per_tile
    half = D_MODEL // 2
    tc = tc_ref[...]
    for k in range(rows_per_tile):
        lo, hi = k * GRID_W, (k + 1) * GRID_W
        o_ref[lo:hi, 0:half] = x_ref[lo:hi, 0:half] + tr_ref[pl.ds(r0 + k, 1), :]
        o_ref[lo:hi, half:D_MODEL] = x_ref[lo:hi, half:D_MODEL] + tc


def _embed(x_sample):
    n_seq, length, _ = x_sample.shape
    rows = length // GRID_W
    quarter = D_MODEL // 4
    omega = 1.0 / (POS_BASE ** (jnp.arange(quarter, dtype=F32) / quarter))
    ang_r = jnp.arange(rows, dtype=F32)[:, None] * omega
    ang_c = jnp.arange(GRID_W, dtype=F32)[:, None] * omega
    tab_r = jnp.concatenate([jnp.sin(ang_r), jnp.cos(ang_r)], axis=-1)
    tab_c = jnp.concatenate([jnp.sin(ang_c), jnp.cos(ang_c)], axis=-1)
    rows_per_tile = 8
    tile = rows_per_tile * GRID_W
    tiles_per_seq = length // tile
    x2 = x_sample.reshape(n_seq * length, D_MODEL)
    return pl.pallas_call(
        functools.partial(_embed_kernel, rows_per_tile=rows_per_tile, tiles_per_seq=tiles_per_seq),
        grid=(n_seq * tiles_per_seq,),
        in_specs=[pl.BlockSpec((tile, D_MODEL), lambda i: (i, 0)),
                  pl.BlockSpec((rows, D_MODEL // 2), lambda i: (0, 0)),
                  pl.BlockSpec((GRID_W, D_MODEL // 2), lambda i: (0, 0))],
        out_specs=pl.BlockSpec((tile, D_MODEL), lambda i: (i, 0)),
        out_shape=jax.ShapeDtypeStruct(x2.shape, F32),
        compiler_params=_params("parallel"),
        name="pos_embed",
    )(x2, tab_r, tab_c)


def _norm_mod(x, g, shift, scale):
    ms = jnp.mean(x * x, axis=-1, keepdims=True)
    h = x * lax.rsqrt(ms + EPS) * g
    return h * (1.0 + scale) + shift


LANES = 128
SLOTS = LANES // S5_P


def _regroup_in(u_refs, ug_ref):
    blocks = TM // S5_T
    slot = lax.broadcasted_iota(jnp.int32, (blocks, LANES), 1) // S5_P
    rot = {}
    for j in range(S5_T):
        for a in range(S5_W // LANES):
            t = u_refs[a][pl.ds(j, blocks, stride=S5_T), :]
            rot[j, a] = t if j % SLOTS == 0 else pltpu.roll(t, S5_P * (j % SLOTS), 1)
    for g in range(S5_G):
        a, s = g // SLOTS, g % SLOTS
        for jt in range(S5_T // SLOTS):
            acc = None
            for sig in range(SLOTS):
                piece = rot[SLOTS * jt + (sig - s) % SLOTS, a]
                acc = piece if acc is None else jnp.where(slot == sig, piece, acc)
            ug_ref[g, :, jt * LANES:(jt + 1) * LANES] = acc.astype(ug_ref.dtype)


def _regroup_out(yg_ref, y_refs):
    blocks = yg_ref.shape[1]
    slot = lax.broadcasted_iota(jnp.int32, (blocks, LANES), 1) // S5_P
    for j in range(S5_T):
        jt = j // SLOTS
        for a in range(S5_W // LANES):
            acc = None
            for s in range(SLOTS):
                piece = yg_ref[SLOTS * a + s, :, jt * LANES:(jt + 1) * LANES].astype(F32)
                acc = piece if acc is None else jnp.where(slot == (s + j) % SLOTS, piece, acc)
            if j % SLOTS:
                acc = pltpu.roll(acc, LANES - S5_P * (j % SLOTS), 1)
            y_refs[a][pl.ds(j, blocks, stride=S5_T), :] = acc


def _in_kernel(x_ref, m_ref, g_ref, w_ref, wba_ref, wbat_ref,
               u_ref, ug_ref, z_ref, qkv_ref, zg_ref, ba_ref, bat_ref, edge_ref, *ut_refs):
    m = m_ref[0]
    h = _norm_mod(x_ref[...], g_ref[...], m[:, 0:D_MODEL], m[:, D_MODEL:2 * D_MODEL]).astype(BF16)
    o1, o2, o3, o4 = S5_W, 2 * S5_W, 2 * S5_W + 3 * GDN_W, 2 * S5_W + 4 * GDN_W
    u = _dot(h, w_ref[:, 0:o1])
    u_ref[...] = u.astype(u_ref.dtype)
    for a, ut_ref in enumerate(ut_refs):
        ut_ref[...] = u[:, a * LANES:(a + 1) * LANES]
    _regroup_in(ut_refs, ug_ref)
    z_ref[...] = _dot(h, w_ref[:, o1:o2]).astype(z_ref.dtype)
    qkv = _dot(h, w_ref[:, o2:o3]).astype(qkv_ref.dtype)
    qkv_ref[...] = qkv
    for t in range(TM // GT):
        edge_ref[t, 0:1, :] = qkv[t * GT:t * GT + 1]
        edge_ref[t, 1:2, :] = qkv[(t + 1) * GT - 1:(t + 1) * GT]
    zg_ref[...] = _dot(h, w_ref[:, o3:o4]).astype(zg_ref.dtype)
    ba_ref[...] = _dot(h, wba_ref[...])
    bat_ref[...] = _dot_nt(wbat_ref[...], h)


def _mod_row_map(n_ctx_tiles, tiles_per_lat):
    def index(i):
        return (jnp.where(i < n_ctx_tiles, 0, 1 + (i - n_ctx_tiles) // tiles_per_lat), 0, 0)
    return index


def _in_proj(x, mods_l, g, w_in, w_ba, w_bat, layer, mod_map):
    tokens = x.shape[0]
    const = lambda i: (0, 0)
    tile = lambda n: pl.BlockSpec((TM, n), lambda i: (i, 0))
    weight = lambda w: pl.BlockSpec((None,) + w.shape[1:], lambda i: (layer, 0, 0))
    return pl.pallas_call(
        _in_kernel,
        grid=(tokens // TM,),
        in_specs=[tile(D_MODEL), pl.BlockSpec((1, 1, N_MOD * D_MODEL), mod_map),
                  pl.BlockSpec(g.shape, const), weight(w_in), weight(w_ba), weight(w_bat)],
        out_specs=[tile(S5_W), pl.BlockSpec((S5_G, TM // S5_T, S5_TP), lambda i: (0, i, 0)),
                   tile(S5_W), tile(3 * GDN_W), tile(GDN_W), tile(GATE_PAD),
                   pl.BlockSpec((N_GATE, TM), lambda i: (0, i)),
                   pl.BlockSpec((TM // GT, 2, 3 * GDN_W), lambda i: (i, 0, 0))],
        out_shape=[jax.ShapeDtypeStruct((tokens, S5_W), BF16),
                   jax.ShapeDtypeStruct((S5_G, tokens // S5_T, S5_TP), BF16),
                   jax.ShapeDtypeStruct((tokens, S5_W), BF16),
                   jax.ShapeDtypeStruct((tokens, 3 * GDN_W), BF16),
                   jax.ShapeDtypeStruct((tokens, GDN_W), BF16),
                   jax.ShapeDtypeStruct((tokens, GATE_PAD), F32),
                   jax.ShapeDtypeStruct((N_GATE, tokens), F32),
                   jax.ShapeDtypeStruct((tokens // GT, 2, 3 * GDN_W), BF16)],
        scratch_shapes=[pltpu.VMEM((TM, LANES), F32)] * (S5_W // LANES),
        compiler_params=_params("parallel"),
        name="in_proj",
    )(x, mods_l, g, w_in, w_ba, w_bat)


N_POW = 8


def _zoh(lam_re, lam_im, log_dt):
    lam_re = jnp.minimum(lam_re, -1e-4)
    dt = jnp.exp(log_dt)
    x, th = lam_re * dt, lam_im * dt
    mag = jnp.exp(x)
    num_re, num_im = mag * jnp.cos(th) - 1.0, mag * jnp.sin(th)
    den = lam_re * lam_re + lam_im * lam_im
    return (x, th, (num_re * lam_re + num_im * lam_im) / den, (num_im * lam_re - num_re * lam_im) / den)


def _power(x, th, e):
    mag = jnp.exp(x * e)
    return mag * jnp.cos(th * e), mag * jnp.sin(th * e)


def _shift_lanes(lo, hi, t, left):
    if t == 0:
        return lo, hi
    lane = lax.broadcasted_iota(jnp.int32, lo.shape, 1)
    zero = jnp.zeros_like(lo)
    r = t % LANES
    if not left:
        if t >= LANES:
            return zero, (lo if r == 0 else jnp.where(lane >= r, pltpu.roll(lo, r, 1), 0.0))
        rl, rh = pltpu.roll(lo, r, 1), pltpu.roll(hi, r, 1)
        return jnp.where(lane >= r, rl, 0.0), jnp.where(lane >= r, rh, rl)
    if t >= LANES:
        return (hi if r == 0 else jnp.where(lane < LANES - r, pltpu.roll(hi, LANES - r, 1), 0.0)), zero
    rl, rh = pltpu.roll(lo, LANES - r, 1), pltpu.roll(hi, LANES - r, 1)
    return jnp.where(lane < LANES - r, rl, rh), jnp.where(lane < LANES - r, rh, 0.0)


def _s5_ops_kernel(prow_ref, pcol_ref, bb_ref, ct_ref, k_ref, bf_ref, bbk_ref, cf_ref, cb_ref, ap_ref,
                   knat_ref):
    s16 = (pl.program_id(1) % SLOTS) * S5_P
    prow = prow_ref[0, 0]
    pcol = pcol_ref[0, 0]
    lane = lax.broadcasted_iota(jnp.int32, (1, 2 * S5_N), 1)
    sign_k = jnp.where(lane < S5_N, 1.0, -1.0)
    rep = jnp.where(lax.broadcasted_iota(jnp.int32, (S5_P, S5_TP), 1) % S5_P
                    == lax.broadcasted_iota(jnp.int32, (S5_P, S5_TP), 0), 1.0, 0.0)
    exact = functools.partial(jnp.dot, precision=HIGHEST, preferred_element_type=F32)
    e_rows = lax.broadcasted_iota(jnp.int32, (S5_T, 1), 0).astype(F32)
    e_lanes = lax.broadcasted_iota(jnp.int32, (1, LANES), 1).astype(F32)
    tok_r = lax.broadcasted_iota(jnp.int32, (S5_TP, S5_T), 0) // S5_P
    e_r = lax.broadcasted_iota(jnp.int32, (S5_TP, S5_T), 1)
    tok_l = lax.broadcasted_iota(jnp.int32, (LANES, S5_TP), 1) // S5_P
    e_l = lax.broadcasted_iota(jnp.int32, (LANES, S5_TP), 0)
    pick = lambda cond: jnp.where(cond, 1.0, 0.0)
    kall = []
    for d in range(N_DIR):
        x_r, th_r, fr_r, fi_r = _zoh(prow[d:d + 1], prow[2 + d:3 + d], prow[4 + d:5 + d])
        x_c, th_c, _, _ = _zoh(pcol[:, d:d + 1], pcol[:, 2 + d:3 + d], pcol[:, 4 + d:5 + d])
        bpack, bswap = bb_ref[0, 0, d, 0], bb_ref[0, 0, d, 1]
        bbar = fr_r * bpack + fi_r * bswap
        bbar_sw = fr_r * bswap - fi_r * bpack
        pr, pi = _power(x_r, th_r, e_rows)
        sel = pick(e_r == ((S5_T - 1) - tok_r if d == 0 else tok_r))
        pr, pi = exact(sel, pr), exact(sel, pi)
        b_nat = pr * jnp.concatenate([bbar] * S5_T, axis=0) + pi * jnp.concatenate([bbar_sw] * S5_T, axis=0)
        b_out = bf_ref if d == 0 else bbk_ref
        for rt in range(S5_TP // LANES):
            b_out[0, 0, rt * LANES:(rt + 1) * LANES, :] = pltpu.roll(
                b_nat[rt * LANES:(rt + 1) * LANES], s16, 0).astype(b_out.dtype)
        c_re = jnp.dot(ct_ref[0, 0, d, 0], rep, precision=HIGHEST, preferred_element_type=F32)
        c_im = jnp.dot(ct_ref[0, 0, d, 1], rep, precision=HIGHEST, preferred_element_type=F32)
        pw_r, pw_i = _power(x_c, th_c, e_lanes)
        sel = pick(e_l == (tok_l + 1 if d == 0 else S5_T - tok_l))
        pr, pi = exact(pw_r, sel), exact(pw_i, sel)
        c_nat = jnp.concatenate([c_re * pr - c_im * pi, -(c_re * pi + c_im * pr)], axis=0)
        c_out = cf_ref if d == 0 else cb_ref
        for ct in range(S5_TP // LANES):
            c_out[0, 0, :, ct * LANES:(ct + 1) * LANES] = pltpu.roll(
                c_nat[:, ct * LANES:(ct + 1) * LANES], s16, 1).astype(c_out.dtype)
        sel = pick(e_l == (tok_l if d == 0 else (S5_T - 1) - tok_l))
        pr, pi = exact(pw_r, sel), exact(pw_i, sel)
        ca = jnp.concatenate([c_re * pr - c_im * pi, c_re * pi + c_im * pr], axis=0)
        kall.append(jnp.dot(bbar * sign_k, ca, precision=HIGHEST, preferred_element_type=F32))
        row = lax.broadcasted_iota(jnp.int32, (2 * N_POW, 1), 0)
        expo = (S5_T * jnp.left_shift(1, row // 2)).astype(F32)
        pr, pi = _power(x_r, th_r, expo)
        ap_ref[0, 0, d * 2 * N_POW:(d + 1) * 2 * N_POW, :] = jnp.where(row % 2 == 0, pr, -pi * sign_k)

    kf_lo, kf_hi = kall[0][:, :LANES], kall[0][:, LANES:]
    kb_lo, kb_hi = kall[1][:, :LANES], kall[1][:, LANES:]
    for j in range(S5_T):
        f_lo, f_hi = _shift_lanes(kf_lo, kf_hi, S5_P * j, False)
        b_lo, b_hi = _shift_lanes(kb_lo, kb_hi, S5_P * (S5_T - 1 - j), True)
        knat_ref[j * S5_P:(j + 1) * S5_P, 0:LANES] = f_lo + b_lo
        knat_ref[j * S5_P:(j + 1) * S5_P, LANES:2 * LANES] = f_hi + b_hi
    for rt in range(S5_TP // LANES):
        for ct in range(S5_TP // LANES):
            tile = knat_ref[rt * LANES:(rt + 1) * LANES, ct * LANES:(ct + 1) * LANES]
            k_ref[0, 0, rt * LANES:(rt + 1) * LANES, ct * LANES:(ct + 1) * LANES] = pltpu.roll(
                pltpu.roll(tile, s16, 0), s16, 1).astype(k_ref.dtype)


def _s5_operators(lam_re, lam_im, log_dt, b_re, b_im, c_re, c_im):
    depth = lam_re.shape[0]
    dup = lambda a: jnp.concatenate([a, a], axis=-1)
    ldt = jnp.broadcast_to(log_dt[..., None], lam_re.shape)
    g_major = lambda a: jnp.moveaxis(a, 1, 2)
    prow = jnp.concatenate([dup(g_major(a)) for a in (lam_re, lam_im, ldt)], axis=2)
    pcol = jnp.concatenate([g_major(a) for a in (lam_re, lam_im, ldt)], axis=2)
    pcol = jnp.pad(jnp.swapaxes(pcol, 2, 3), ((0, 0), (0, 0), (0, 0), (0, 2)))
    bt_re, bt_im = (jnp.swapaxes(g_major(a), 3, 4) for a in (b_re, b_im))
    bb = jnp.stack([jnp.concatenate([bt_re, bt_im], axis=-1),
                    jnp.concatenate([-bt_im, bt_re], axis=-1)], axis=3)
    ct = jnp.stack([jnp.swapaxes(g_major(a), 3, 4) for a in (c_re, c_im)], axis=3)
    per = lambda a: pl.BlockSpec((1, 1) + a.shape[2:], lambda l, g: (l, g) + (0,) * (a.ndim - 2))
    mat = lambda r, c: pl.BlockSpec((1, 1, r, c), lambda l, g: (l, g, 0, 0))
    shape = lambda r, c, dt: jax.ShapeDtypeStruct((depth, S5_G, r, c), dt)
    return pl.pallas_call(
        _s5_ops_kernel,
        grid=(depth, S5_G),
        in_specs=[per(prow), per(pcol), per(bb), per(ct)],
        out_specs=[mat(S5_TP, S5_TP), mat(S5_TP, 2 * S5_N), mat(S5_TP, 2 * S5_N),
                   mat(2 * S5_N, S5_TP), mat(2 * S5_N, S5_TP), mat(N_DIR * 2 * N_POW, 2 * S5_N)],
        out_shape=[shape(S5_TP, S5_TP, BF16), shape(S5_TP, 2 * S5_N, BF16), shape(S5_TP, 2 * S5_N, BF16),
                   shape(2 * S5_N, S5_TP, BF16), shape(2 * S5_N, S5_TP, BF16),
                   shape(N_DIR * 2 * N_POW, 2 * S5_N, F32)],
        scratch_shapes=[pltpu.VMEM((S5_TP, S5_TP), F32)],
        compiler_params=_params("parallel", "parallel"),
        name="s5_operators",
    )(prow, pcol, bb, ct)


def _lane_swap(x):
    return pltpu.roll(x, S5_N, 1)


def _cmul(a1, a2, h):
    return a1 * h + a2 * _lane_swap(h)


def _block_scan(s, seg, a_pow, d, reverse):
    rows = s.shape[0]
    ridx = lax.broadcasted_iota(jnp.int32, s.shape, 0) % seg
    h = s
    k, step = 0, 1
    while step < seg:
        a1 = a_pow[d * 16 + 2 * k:d * 16 + 2 * k + 1]
        a2 = a_pow[d * 16 + 2 * k + 1:d * 16 + 2 * k + 2]
        if reverse:
            sh = jnp.where(ridx < seg - step, pltpu.roll(h, rows - step, 0), 0.0)
        else:
            sh = jnp.where(ridx >= step, pltpu.roll(h, step, 0), 0.0)
        h = h + _cmul(a1, a2, sh)
        k += 1
        step *= 2
    return h


def _s5_kernel(u_ref, k_ref, bf_ref, bb_ref, cf_ref, cb_ref, ap_ref, h0_ref,
               y_ref, fin_ref, hf_ref, hb_ref, *, ctx_rows, ctx_seg, lat_seg, n_ctx, n_lat):
    u = u_ref[0]
    a_pow = ap_ref[0]
    s_f = _dot(u, bf_ref[0])
    s_b = _dot(u, bb_ref[0])
    lat_rows = u.shape[0] - ctx_rows

    lrow = lax.broadcasted_iota(jnp.int32, (lat_rows, 2 * S5_N), 0)
    lmod = lrow % lat_seg
    lseq = lrow // lat_seg
    h0 = h0_ref[0]
    h0_f = jnp.zeros((lat_rows, 2 * S5_N), F32)
    h0_b = jnp.zeros((lat_rows, 2 * S5_N), F32)
    for s in range(n_lat):
        h0_f = jnp.where(lseq == s, h0[2 * s:2 * s + 1], h0_f)
        h0_b = jnp.where(lseq == s, h0[2 * s + 1:2 * s + 2], h0_b)
    first, last = lmod == 0, lmod == lat_seg - 1
    sl_f = s_f[ctx_rows:] + jnp.where(first, _cmul(a_pow[0:1], a_pow[1:2], h0_f), 0.0)
    sl_b = s_b[ctx_rows:] + jnp.where(last, _cmul(a_pow[16:17], a_pow[17:18], h0_b), 0.0)

    hc_f = _block_scan(s_f[:ctx_rows], ctx_seg, a_pow, 0, False)
    hc_b = _block_scan(s_b[:ctx_rows], ctx_seg, a_pow, 1, True)
    hl_f = _block_scan(sl_f, lat_seg, a_pow, 0, False)
    hl_b = _block_scan(sl_b, lat_seg, a_pow, 1, True)

    cmod = lax.broadcasted_iota(jnp.int32, (ctx_rows, 2 * S5_N), 0) % ctx_seg
    pc_f = jnp.where(cmod == 0, 0.0, pltpu.roll(hc_f, 1, 0))
    pc_b = jnp.where(cmod == ctx_seg - 1, 0.0, pltpu.roll(hc_b, ctx_rows - 1, 0))
    pl_f = jnp.where(first, h0_f, pltpu.roll(hl_f, 1, 0))
    pl_b = jnp.where(last, h0_b, pltpu.roll(hl_b, lat_rows - 1, 0))
    prev_f = jnp.concatenate([pc_f, pl_f], axis=0).astype(BF16)
    prev_b = jnp.concatenate([pc_b, pl_b], axis=0).astype(BF16)
    y_ref[0] = (_dot(u, k_ref[0]) + _dot(prev_f, cf_ref[0]) + _dot(prev_b, cb_ref[0])).astype(y_ref.dtype)

    hf_ref[...] = hc_f
    hb_ref[...] = hc_b
    fin_ref[0, 0] = hf_ref[pl.ds(ctx_seg - 1, n_ctx, stride=ctx_seg), :]
    fin_ref[0, 1] = hb_ref[pl.ds(0, n_ctx, stride=ctx_seg), :]


def _s5_mixer(u_g, ops, h0, n_ctx, ctx_len, n_lat, lat_len):
    k_mat, b_f, b_b, c_f, c_b, a_pow = ops
    rows = u_g.shape[1]
    ctx_seg, lat_seg = ctx_len // S5_T, lat_len // S5_T
    ctx_rows = n_ctx * ctx_seg
    per_g = lambda a: pl.BlockSpec((1,) + a.shape[1:], lambda g: (g,) + (0,) * (a.ndim - 1))
    kern = functools.partial(_s5_kernel, ctx_rows=ctx_rows, ctx_seg=ctx_seg, lat_seg=lat_seg,
                             n_ctx=n_ctx, n_lat=n_lat)
    return pl.pallas_call(
        kern,
        grid=(S5_G,),
        in_specs=[per_g(a) for a in (u_g, k_mat, b_f, b_b, c_f, c_b, a_pow, h0)],
        out_specs=[pl.BlockSpec((1, rows, S5_TP), lambda g: (g, 0, 0)),
                   pl.BlockSpec((1, N_DIR, n_ctx, 2 * S5_N), lambda g: (g, 0, 0, 0))],
        out_shape=[jax.ShapeDtypeStruct((S5_G, rows, S5_TP), BF16),
                   jax.ShapeDtypeStruct((S5_G, N_DIR, n_ctx, 2 * S5_N), F32)],
        scratch_shapes=[pltpu.VMEM((ctx_rows, 2 * S5_N), F32)] * N_DIR,
        compiler_params=_params("parallel"),
        name="s5_mixer",
    )(u_g, k_mat, b_f, b_b, c_f, c_b, a_pow, h0)


def _gdn_pre_kernel(x_ref, hp_ref, hn_ref, cw_ref, ba_ref, bat_ref, al_ref, dl_ref, ar_ref, dr_ref,
                    u_ref, wq_ref, kd_ref, a_ref, eg_ref,
                    q_ref, k_ref, v_ref, col_ref, row_ref, m_ref, acc_ref):
    x = x_ref[...].astype(F32)
    rid = lax.broadcasted_iota(jnp.int32, x.shape, 0)
    x_prev = jnp.where(rid == 0, hp_ref[0].astype(F32), pltpu.roll(x, 1, 0))
    x_next = jnp.where(rid == GT - 1, hn_ref[0].astype(F32), pltpu.roll(x, GT - 1, 0))
    cw = cw_ref[...]
    y = _silu(x_prev * cw[0:1] + x * cw[1:2] + x_next * cw[2:3])
    for h in range(GDN_H):
        lo, hi = h * GDN_DK, (h + 1) * GDN_DK
        qh = y[:, lo:hi]
        kh = y[:, GDN_W + lo:GDN_W + hi]
        q_ref[:, lo:hi] = qh * (lax.rsqrt(jnp.sum(qh * qh, axis=-1, keepdims=True) + EPS)
                                * (GDN_DK ** -0.5))
        k_ref[:, lo:hi] = kh * lax.rsqrt(jnp.sum(kh * kh, axis=-1, keepdims=True) + EPS)
    v_ref[...] = y[:, 2 * GDN_W:]

    half = N_GATE // 2
    ba = ba_ref[...]
    lane = lax.broadcasted_iota(jnp.int32, ba.shape, 1)
    gate = jnp.where(lane < half, _sigmoid(ba), -jnp.exp(al_ref[...]) * _softplus(ba + dl_ref[...]))
    ri = lax.broadcasted_iota(jnp.int32, (GT, GT), 0)
    ci = lax.broadcasted_iota(jnp.int32, (GT, GT), 1)
    same = (ri // CHUNK) == (ci // CHUNK)
    lower = jnp.where(same & (ci <= ri), 1.0, 0.0)
    upper = jnp.where(same & (ci >= ri), 1.0, 0.0)
    cum_f = jnp.dot(lower, gate, precision=HIGHEST, preferred_element_type=F32)
    cum_b = jnp.dot(upper, gate, precision=HIGHEST, preferred_element_type=F32)
    fwd_lane = (lane >= half) & (lane < half + GDN_H)
    bwd_lane = (lane >= half + GDN_H) & (lane < N_GATE)
    col_ref[...] = jnp.where(fwd_lane, cum_f, jnp.where(bwd_lane, cum_b, gate))
    bat = bat_ref[...]
    g_row = -jnp.exp(ar_ref[...]) * _softplus(bat + dr_ref[...])
    row_f = jnp.dot(g_row, upper, precision=HIGHEST, preferred_element_type=F32)
    row_b = jnp.dot(g_row, lower, precision=HIGHEST, preferred_element_type=F32)
    srow = lax.broadcasted_iota(jnp.int32, bat.shape, 0)
    row_ref[...] = jnp.where(srow < half + GDN_H, row_f, row_b)

    assert 2 * CHUNK == LANES and GT == 2 * CHUNK
    ii = lax.broadcasted_iota(jnp.int32, (CHUNK, LANES), 0)
    ll = lax.broadcasted_iota(jnp.int32, (CHUNK, LANES), 1)
    jj = ll % CHUNK
    fwd = ll < CHUNK
    incl = (fwd & (jj <= ii)) | (~fwd & (jj >= ii))
    strict = (fwd & (jj < ii)) | (~fwd & (jj > ii))
    fwd1 = lax.broadcasted_iota(jnp.int32, (1, LANES), 1) < CHUNK
    keep_f = jnp.where(fwd, 1.0, 0.0).astype(BF16)
    keep_b = jnp.where(fwd, 0.0, 1.0).astype(BF16)
    pairs = [(c, h) for c in range(GT // CHUNK) for h in range(GDN_H)]

    def both(c, lane_f, lane_b):
        rows = slice(c * CHUNK, (c + 1) * CHUNK)
        return jnp.where(fwd, col_ref[rows, lane_f:lane_f + 1], col_ref[rows, lane_b:lane_b + 1])

    def block_diag(x16):
        return jnp.concatenate([x16 * keep_f, x16 * keep_b], axis=0)

    for n, (c, h) in enumerate(pairs):
        rows = slice(c * CHUNK, (c + 1) * CHUNK)
        cols = slice(h * GDN_DK, (h + 1) * GDN_DK)
        beta = both(c, h, GDN_H + h)
        gc = both(c, half + h, half + GDN_H + h)
        r_f = row_ref[half + h:half + h + 1, :]
        r_b = row_ref[half + GDN_H + h:half + GDN_H + h + 1, :]
        gr = (jnp.where(fwd1, r_f, pltpu.roll(r_b, CHUNK, 1)) if c == 0
              else jnp.where(fwd1, pltpu.roll(r_f, CHUNK, 1), r_b))
        decay = jnp.where(incl, jnp.exp(jnp.where(incl, gc - gr, 0.0)), 0.0)
        k16 = k_ref[rows, cols].astype(BF16)
        kk16 = jnp.concatenate([k16, k16], axis=0)
        m = jnp.where(strict, beta * _dot_nt(k16, kk16) * decay, 0.0)
        a_ref[rows, cols] = (_dot_nt(q_ref[rows, cols].astype(BF16), kk16) * decay).astype(BF16)
        m_ref[n] = m
        acc_ref[n] = jnp.where((ii // 2) == (jj // 2), -m, 0.0)

    b = 2
    while b < CHUNK:
        pair = ((ii // (2 * b)) == (jj // (2 * b))) & ((ii // b) != (jj // b))
        for n in range(len(pairs)):
            cpart = jnp.where(pair, m_ref[n], 0.0)
            acc = acc_ref[n]
            a16 = acc.astype(BF16)
            y = cpart + _dot(a16, block_diag(cpart.astype(BF16)))
            acc_ref[n] = acc - y - _dot(y.astype(BF16), block_diag(a16))
        b *= 2

    zeros = jnp.zeros((CHUNK, 2 * GDN_DK), BF16)
    for n, (c, h) in enumerate(pairs):
        rows = slice(c * CHUNK, (c + 1) * CHUNK)
        cols = slice(h * GDN_DK, (h + 1) * GDN_DK)
        kh, vh, qh = k_ref[rows, cols], v_ref[rows, cols], q_ref[rows, cols]
        rhs, egc, gcs = [], [], []
        for d in range(N_DIR):
            lane_ = d * GDN_H + h
            beta = col_ref[rows, lane_:lane_ + 1]
            gc = col_ref[rows, half + lane_:half + lane_ + 1]
            gcs.append(gc)
            egc.append(jnp.exp(gc))
            rhs.append(jnp.concatenate([vh * beta, kh * beta * egc[d]], axis=1))
        stacked = jnp.concatenate(
            [jnp.concatenate([rhs[0].astype(BF16), zeros], axis=1),
             jnp.concatenate([zeros, rhs[1].astype(BF16)], axis=1)], axis=0)
        prod = _dot(acc_ref[n].astype(BF16), stacked)
        for d in range(N_DIR):
            sol = rhs[d] + prod[:, d * 2 * GDN_DK:(d + 1) * 2 * GDN_DK]
            u_ref[d, rows, cols] = sol[:, :GDN_DK]
            wq_ref[d, c, 0:CHUNK, cols] = sol[:, GDN_DK:].astype(BF16)
            wq_ref[d, c, CHUNK:2 * CHUNK, cols] = (qh * egc[d]).astype(BF16)
            g_last = gcs[d][0:1] if d == 1 else gcs[d][CHUNK - 1:CHUNK]
            kd_ref[d, rows, cols] = (kh * jnp.exp(g_last - gcs[d])).astype(BF16)
            eg_ref[c, d * GDN_H + h:d * GDN_H + h + 1, :] = jnp.broadcast_to(jnp.exp(g_last), (1, GDN_DK))


def _gdn_pre(qkv, halo_prev, halo_next, conv_w, ba, bat, a_log, dt_bias):
    tokens = qkv.shape[0]
    n_tiles = tokens // GT
    cpt = GT // CHUNK
    half = N_GATE // 2
    al = a_log.reshape(-1)
    db = dt_bias.reshape(-1)
    al_l = jnp.zeros((1, GATE_PAD), F32).at[0, half:N_GATE].set(al)
    db_l = jnp.zeros((1, GATE_PAD), F32).at[0, half:N_GATE].set(db)
    al_r = jnp.zeros((N_GATE, 1), F32).at[half:, 0].set(al)
    db_r = jnp.zeros((N_GATE, 1), F32).at[half:, 0].set(db)
    const = lambda i: (0, 0)
    tile = lambda n: pl.BlockSpec((GT, n), lambda i: (i, 0))
    halo = pl.BlockSpec((1, 1, 3 * GDN_W), lambda i: (i, 0, 0))
    n_chunks = tokens // CHUNK
    n_pairs = cpt * GDN_H
    return pl.pallas_call(
        _gdn_pre_kernel,
        grid=(n_tiles,),
        in_specs=[tile(3 * GDN_W), halo, halo, pl.BlockSpec(conv_w.shape, const),
                  tile(GATE_PAD), pl.BlockSpec((N_GATE, GT), lambda i: (0, i)),
                  pl.BlockSpec((1, GATE_PAD), const), pl.BlockSpec((1, GATE_PAD), const),
                  pl.BlockSpec((N_GATE, 1), const), pl.BlockSpec((N_GATE, 1), const)],
        out_specs=[pl.BlockSpec((N_DIR, GT, GDN_W), lambda i: (0, i, 0)),
                   pl.BlockSpec((N_DIR, cpt, 2 * CHUNK, GDN_W), lambda i: (0, i, 0, 0)),
                   pl.BlockSpec((N_DIR, GT, GDN_W), lambda i: (0, i, 0)),
                   pl.BlockSpec((GT, GDN_W), lambda i: (i, 0)),
                   pl.BlockSpec((cpt, N_DIR * GDN_H, GDN_DK), lambda i: (i, 0, 0))],
        out_shape=[jax.ShapeDtypeStruct((N_DIR, tokens, GDN_W), F32),
                   jax.ShapeDtypeStruct((N_DIR, n_chunks, 2 * CHUNK, GDN_W), BF16),
                   jax.ShapeDtypeStruct((N_DIR, tokens, GDN_W), BF16),
                   jax.ShapeDtypeStruct((tokens, GDN_W), BF16),
                   jax.ShapeDtypeStruct((n_chunks, N_DIR * GDN_H, GDN_DK), F32)],
        scratch_shapes=[pltpu.VMEM((GT, GDN_W), F32)] * 3
                       + [pltpu.VMEM((GT, GATE_PAD), F32), pltpu.VMEM((N_GATE, GT), F32),
                          pltpu.VMEM((n_pairs, CHUNK, N_DIR * CHUNK), F32),
                          pltpu.VMEM((n_pairs, CHUNK, N_DIR * CHUNK), F32)],
        compiler_params=_params("parallel"),
        name="gdn_pre",
    )(qkv, halo_prev, halo_next, conv_w, ba, bat, al_l, db_l, al_r, db_r)


def _gdn_scan_kernel(*refs, has_s0, has_fin):
    per_dir = 5
    pos = NS * N_DIR * per_dir
    ins = [refs[k * per_dir:(k + 1) * per_dir] for k in range(NS * N_DIR)]
    s0_ref = None
    if has_s0:
        s0_ref = refs[pos]
        pos += 1
    of_ref, ob_ref = refs[pos], refs[pos + 1]
    pos += 2
    fin_ref = None
    if has_fin:
        fin_ref = refs[pos]
        pos += 1
    s_ref = refs[pos]
    j = pl.program_id(1)

    @pl.when(j == 0)
    def _():
        if has_s0:
            s_ref[...] = s0_ref[...]
        else:
            s_ref[...] = jnp.zeros(s_ref.shape, F32)

    cpb = SB // CHUNK
    for step in range(cpb):
        work = []
        for p in range(NS):
            for d in range(N_DIR):
                u_r, wq_r, kd_r, a_r, eg_r = ins[p * N_DIR + d]
                o_r = of_ref if d == 0 else ob_ref
                c = step if d == 0 else cpb - 1 - step
                rows = slice(c * CHUNK, (c + 1) * CHUNK)
                for h in range(GDN_H):
                    work.append((p, d, h, c, rows, slice(h * GDN_DK, (h + 1) * GDN_DK), u_r, wq_r, kd_r,
                                 a_r, eg_r, o_r))
        ps = [_dot(wq_r[0, c, :, cols], s_ref[p, d, h].astype(BF16))
              for (p, d, h, c, rows, cols, u_r, wq_r, kd_r, a_r, eg_r, o_r) in work]
        v16 = [(u_r[0, rows, cols] - q[:CHUNK]).astype(BF16)
               for q, (p, d, h, c, rows, cols, u_r, wq_r, kd_r, a_r, eg_r, o_r) in zip(ps, work)]
        zero = jnp.zeros((CHUNK, GDN_DK), BF16)
        for q, v, (p, d, h, c, rows, cols, u_r, wq_r, kd_r, a_r, eg_r, o_r) in zip(ps, v16, work):
            v2 = jnp.concatenate([v, zero] if d == 0 else [zero, v], axis=0)
            o_r[0, p, rows, cols] = (q[CHUNK:] + _dot(a_r[rows, cols], v2)).astype(o_r.dtype)
        for v, (p, d, h, c, rows, cols, u_r, wq_r, kd_r, a_r, eg_r, o_r) in zip(v16, work):
            eg = eg_r[c, d * GDN_H + h:d * GDN_H + h + 1, :]
            s_ref[p, d, h] = s_ref[p, d, h] * eg + _dot_tn(kd_r[0, rows, cols], v)

    if has_fin:
        @pl.when(j == pl.num_programs(1) - 1)
        def _():
            fin_ref[...] = s_ref[...]


def _gdn_scan(u, wq, kd, a, eg, s0, n_seq, length, base_tok, want_fin):
    assert n_seq % NS == 0
    n_b = length // SB
    cpb = SB // CHUNK
    base = base_tok // SB
    in_specs, args = [], []
    for p in range(NS):
        for d in range(N_DIR):
            if d == 0:
                blk = lambda s, j, p=p: base + (s * NS + p) * n_b + j
            else:
                blk = lambda s, j, p=p: base + (s * NS + p) * n_b + (n_b - 1 - j)
            in_specs += [
                pl.BlockSpec((1, SB, GDN_W), lambda s, j, d=d, blk=blk: (d, blk(s, j), 0)),
                pl.BlockSpec((1, cpb, 2 * CHUNK, GDN_W), lambda s, j, d=d, blk=blk: (d, blk(s, j), 0, 0)),
                pl.BlockSpec((1, SB, GDN_W), lambda s, j, d=d, blk=blk: (d, blk(s, j), 0)),
                pl.BlockSpec((SB, GDN_W), lambda s, j, blk=blk: (blk(s, j), 0)),
                pl.BlockSpec((cpb, N_DIR * GDN_H, GDN_DK), lambda s, j, blk=blk: (blk(s, j), 0, 0))]
            args += [u, wq, kd, a, eg]
    state_spec = pl.BlockSpec((NS, N_DIR, GDN_H, GDN_DK, GDN_DK), lambda s, j: (s, 0, 0, 0, 0))
    if s0 is not None:
        in_specs.append(state_spec)
        args.append(s0)
    out_specs = [pl.BlockSpec((1, NS, SB, GDN_W), lambda s, j: (s, 0, j, 0)),
                 pl.BlockSpec((1, NS, SB, GDN_W), lambda s, j: (s, 0, n_b - 1 - j, 0))]
    out_shape = [jax.ShapeDtypeStruct((n_seq // NS, NS, length, GDN_W), BF16)] * 2
    if want_fin:
        out_specs.append(state_spec)
        out_shape.append(jax.ShapeDtypeStruct((n_seq, N_DIR, GDN_H, GDN_DK, GDN_DK), F32))
    outs = pl.pallas_call(
        functools.partial(_gdn_scan_kernel, has_s0=s0 is not None, has_fin=want_fin),
        grid=(n_seq // NS, n_b),
        in_specs=in_specs,
        out_specs=out_specs,
        out_shape=out_shape,
        scratch_shapes=[pltpu.VMEM((NS, N_DIR, GDN_H, GDN_DK, GDN_DK), F32)],
        compiler_params=_params("parallel", "arbitrary"),
        name="gdn_scan",
    )(*args)
    return [o.reshape(n_seq * length, GDN_W) for o in outs[:2]] + list(outs[2:])


def _out_mlp_kernel(x_ref, m_ref, yg_ref, u_ref, z_ref, dsk_ref, ofc_ref, obc_ref, ofl_ref, obl_ref,
                    zg_ref, gn_ref, w_ref, g2_ref, w1_ref, w2_ref, gf_ref, o_ref, *y5_refs,
                    n_ctx_tiles, first_tile, final):
    m = m_ref[0]
    _regroup_out(yg_ref, y5_refs)
    in_ctx = pl.program_id(0) + first_tile < n_ctx_tiles
    gn = gn_ref[...]
    halves = [slice(k * TO // 2, (k + 1) * TO // 2) for k in range(2)]
    xs, hs = [], []
    for r in halves:
        up = lambda ref: ref[r, :].astype(F32)
        y5 = jnp.concatenate([ref[r, :] for ref in y5_refs], axis=1)
        s5 = _gelu_tanh(y5 + up(u_ref) * dsk_ref[...]) * _sigmoid(up(z_ref))
        acc = _dot(s5.astype(BF16), w_ref[0:S5_W, :])
        o = jnp.where(in_ctx, up(ofc_ref) + up(obc_ref), up(ofl_ref) + up(obl_ref))
        zg = up(zg_ref)
        for h in range(GDN_H):
            lo, hi = h * GDN_DK, (h + 1) * GDN_DK
            oh = o[:, lo:hi]
            nh = oh * lax.rsqrt(jnp.mean(oh * oh, axis=-1, keepdims=True) + EPS) * gn
            gh = (nh * _silu(zg[:, lo:hi])).astype(BF16)
            acc = acc + _dot(gh, w_ref[S5_W + lo:S5_W + hi, :])
        x = x_ref[r, :] + m[:, 2 * D_MODEL:3 * D_MODEL] * acc
        xs.append(x)
        hs.append(_norm_mod(x, g2_ref[...], m[:, 3 * D_MODEL:4 * D_MODEL],
                            m[:, 4 * D_MODEL:5 * D_MODEL]).astype(BF16))
    for r, x, h in zip(halves, xs, hs):
        ff = jnp.maximum(_dot(h, w1_ref[...]), 0.0)
        ff = (ff * ff).astype(BF16)
        out = x + m[:, 5 * D_MODEL:6 * D_MODEL] * _dot(ff, w2_ref[...])
        if final:
            out = out * lax.rsqrt(jnp.mean(out * out, axis=-1, keepdims=True) + EPS) * gf_ref[...]
        o_ref[r, :] = out


def _out_mlp(x, mods_l, y_g, u, z, d_skip, o_ctx, o_lat, zg, gdn_norm, w_out, g2, w1, w2, layer,
             ctx_tok, lat_len, g_final=None, first_tok=0, n_tok=None):
    tokens = x.shape[0]
    n_tok = tokens if n_tok is None else n_tok
    assert ctx_tok % TO == 0 and lat_len % TO == 0 and tokens > ctx_tok
    assert first_tok % TO == 0 and n_tok % TO == 0
    n_ctx_tiles, t0 = ctx_tok // TO, first_tok // TO
    mod_row = _mod_row_map(n_ctx_tiles, lat_len // TO)
    const = lambda i: (0, 0)
    tile = lambda n: pl.BlockSpec((TO, n), lambda i: (i + t0, 0))
    ctx_tile = pl.BlockSpec((TO, GDN_W), lambda i: (jnp.minimum(i + t0, n_ctx_tiles - 1), 0))
    lat_tile = pl.BlockSpec((TO, GDN_W), lambda i: (jnp.maximum(i + t0 - n_ctx_tiles, 0), 0))
    weight = lambda w: pl.BlockSpec((None,) + w.shape[1:], lambda i: (layer, 0, 0),
                                    pipeline_mode=pl.Buffered(1))
    final = g_final is not None
    return pl.pallas_call(
        functools.partial(_out_mlp_kernel, n_ctx_tiles=n_ctx_tiles, first_tile=t0, final=final),
        grid=(n_tok // TO,),
        in_specs=[tile(D_MODEL),
                  pl.BlockSpec((1, 1, N_MOD * D_MODEL), lambda i: mod_row(i + t0)),
                  pl.BlockSpec((S5_G, TO // S5_T, S5_TP), lambda i: (0, i + t0, 0)),
                  tile(S5_W), tile(S5_W), pl.BlockSpec((1, S5_W), const),
                  ctx_tile, ctx_tile, lat_tile, lat_tile,
                  tile(GDN_W), pl.BlockSpec((1, GDN_DK), const),
                  weight(w_out), pl.BlockSpec((1, D_MODEL), const), weight(w1), weight(w2),
                  pl.BlockSpec((1, D_MODEL), const)],
        out_specs=pl.BlockSpec((TO, D_MODEL), lambda i: (i, 0)),
        out_shape=jax.ShapeDtypeStruct((n_tok, D_MODEL), F32),
        scratch_shapes=[pltpu.VMEM((TO, LANES), F32)] * (S5_W // LANES),
        compiler_params=_params("parallel"),
        name="out_mlp",
    )(x, mods_l, y_g, u, z, d_skip, *o_ctx, *o_lat, zg, gdn_norm, w_out, g2, w1, w2,
      g_final if final else g2)


def kernel(x_prompt, x_sample, state_s5_re, state_s5_im, state_gdn, c, c_ctx, norm_mix, norm_mlp, w_ada, b_ada, w_in, conv_qkv, s5_lambda_re, s5_lambda_im, s5_log_dt, s5_b_re, s5_b_im, s5_c_re, s5_c_im, s5_d, gdn_a_log, gdn_dt_bias, gdn_norm, w_out, w_mlp_in, w_mlp_out, norm_final):
    n_ctx, ctx_len, _ = x_prompt.shape
    n_lat, lat_len, _ = x_sample.shape
    depth = w_in.shape[0]
    assert ctx_len % TM == 0 and lat_len % TM == 0 and lat_len % (8 * GRID_W) == 0
    assert n_lat + 1 <= 8
    ctx_tok, lat_tok = n_ctx * ctx_len, n_lat * lat_len
    tokens = ctx_tok + lat_tok
    n_ctx_tiles, tiles_per_lat = ctx_tok // TM, lat_len // TM
    mod_map = _mod_row_map(n_ctx_tiles, tiles_per_lat)

    cond = jnp.zeros((8, D_MODEL), F32).at[0].set(c_ctx).at[1:1 + n_lat].set(c)
    mods = _modulation(cond, w_ada, b_ada)
    x = jnp.concatenate([x_prompt.reshape(ctx_tok, D_MODEL), _embed(x_sample)], axis=0)

    starts = np.arange(tokens // GT) * GT
    in_ctx = starts < ctx_tok
    pos = np.where(in_ctx, starts % ctx_len, (starts - ctx_tok) % lat_len)
    has_prev = pos != 0
    has_next = pos + GT != np.where(in_ctx, ctx_len, lat_len)

    s5_ops = _s5_operators(s5_lambda_re, s5_lambda_im, s5_log_dt, s5_b_re, s5_b_im, s5_c_re, s5_c_im)
    w_in16, w_out16 = w_in.astype(BF16), w_out.astype(BF16)
    w_mlp_in16, w_mlp_out16 = w_mlp_in.astype(BF16), w_mlp_out.astype(BF16)
    gate_cols = w_in16[:, :, 2 * S5_W + 4 * GDN_W:]
    w_ba = jnp.pad(gate_cols, ((0, 0), (0, 0), (0, GATE_PAD - N_GATE)))
    w_bat = jnp.swapaxes(gate_cols, 1, 2)
    fin_re, fin_im, fin_gdn = [], [], []
    for l in range(depth):
        mods_l = mods[l].reshape(8, 1, N_MOD * D_MODEL)
        u, u_g, z, qkv, zg, ba, bat, edge = _in_proj(x, mods_l, norm_mix[l].reshape(1, D_MODEL), w_in16,
                                                     w_ba, w_bat, l, mod_map)

        h0 = jnp.concatenate([state_s5_re[:, l], state_s5_im[:, l]], axis=-1)
        h0 = h0.transpose(2, 0, 1, 3).reshape(S5_G, n_lat * N_DIR, 2 * S5_N)
        y_g, s5_fin = _s5_mixer(u_g, tuple(o[l] for o in s5_ops), h0, n_ctx, ctx_len, n_lat, lat_len)
        s5_fin = s5_fin.transpose(2, 1, 0, 3)
        fin_re.append(s5_fin[..., :S5_N])
        fin_im.append(s5_fin[..., S5_N:])

        zero_row = jnp.zeros((1, 3 * GDN_W), qkv.dtype)
        prev_rows = jnp.concatenate([zero_row, edge[:-1, 1]], axis=0)
        next_rows = jnp.concatenate([edge[1:, 0], zero_row], axis=0)
        halo_prev = jnp.where(has_prev[:, None], prev_rows, 0.0)[:, None, :]
        halo_next = jnp.where(has_next[:, None], next_rows, 0.0)[:, None, :]
        pre = _gdn_pre(qkv, halo_prev, halo_next, conv_qkv[l], ba, bat, gdn_a_log[l], gdn_dt_bias[l])
        of_c, ob_c, gdn_fin = _gdn_scan(*pre, None, n_ctx, ctx_len, 0, True)
        of_l, ob_l = _gdn_scan(*pre, state_gdn[:, l], n_lat, lat_len, ctx_tok, False)
        fin_gdn.append(gdn_fin)

        tail = functools.partial(
            _out_mlp, x, mods_l, y_g, u, z, s5_d[l].reshape(1, S5_W), (of_c, ob_c), (of_l, ob_l), zg,
            gdn_norm[l].reshape(1, GDN_DK), w_out16, norm_mlp[l].reshape(1, D_MODEL),
            w_mlp_in16, w_mlp_out16, l, ctx_tok, lat_len)
        if l + 1 < depth:
            x = tail()
    g_fin = norm_final.reshape(1, D_MODEL)
    y_prompt = tail(g_final=g_fin, first_tok=0, n_tok=ctx_tok).reshape(n_ctx, ctx_len, D_MODEL)
    y_sample = tail(g_final=g_fin, first_tok=ctx_tok, n_tok=lat_tok).reshape(n_lat, lat_len, D_MODEL)
    return (y_prompt, y_sample, jnp.stack(fin_re, axis=1), jnp.stack(fin_im, axis=1),
            jnp.stack(fin_gdn, axis=1))
```

```python
import functools
import math

import numpy as np
import jax
import jax.numpy as jnp
from jax import lax
from jax.experimental import pallas as pl
from jax.experimental.pallas import tpu as pltpu

F32 = jnp.float32
BF16 = jnp.bfloat16
HIGHEST = lax.Precision.HIGHEST

D_MODEL = 1024
S5_W = 512
S5_P = 16
S5_G = 32
S5_N = 64
GDN_W = 512
GDN_DK = 128
GDN_H = 4
CHUNK = 64
N_DIR = 2
N_MOD = 6
D_FF = 4 * D_MODEL
GRID_W = 64
POS_BASE = 10000.0
EPS = 1e-6
S5_T = 16
S5_TP = S5_T * S5_P
N_GATE = 2 * N_DIR * GDN_H
GATE_PAD = 128
TM = 256
TO = 512
GT = 256
SB = 256
NS = 2
VMEM_LIMIT = 56 * 1024 * 1024


def _dot(a, b):
    return jnp.dot(a, b, preferred_element_type=F32)


def _dot_nt(a, b):
    return lax.dot_general(a, b, (((1,), (1,)), ((), ())), preferred_element_type=F32)


def _dot_tn(a, b):
    return lax.dot_general(a, b, (((0,), (0,)), ((), ())), preferred_element_type=F32)


def _sigmoid(x):
    return 1.0 / (1.0 + jnp.exp(-x))


def _silu(x):
    return x * _sigmoid(x)


def _softplus(x):
    return jnp.maximum(x, 0.0) + jnp.log1p(jnp.exp(-jnp.abs(x)))


def _gelu_tanh(x):
    c = math.sqrt(2.0 / math.pi)
    return 0.5 * x * (1.0 + jnp.tanh(c * (x + 0.044715 * (x * x * x))))


def _params(*sem):
    return pltpu.CompilerParams(dimension_semantics=sem, vmem_limit_bytes=VMEM_LIMIT)


def _mod_kernel(c_ref, w_ref, b_ref, o_ref):
    sc = _silu(c_ref[...])
    o_ref[0] = jnp.dot(sc, w_ref[0], precision=HIGHEST, preferred_element_type=F32) + b_ref[0]


def _modulation(cond, w_ada, b_ada):
    depth = w_ada.shape[0]
    tn = 1536
    n_w = N_MOD * D_MODEL
    return pl.pallas_call(
        _mod_kernel,
        grid=(depth, n_w // tn),
        in_specs=[pl.BlockSpec((8, D_MODEL), lambda l, j: (0, 0)),
                  pl.BlockSpec((1, D_MODEL, tn), lambda l, j: (l, 0, j)),
                  pl.BlockSpec((1, 1, tn), lambda l, j: (l, 0, j))],
        out_specs=pl.BlockSpec((1, 8, tn), lambda l, j: (l, 0, j)),
        out_shape=jax.ShapeDtypeStruct((depth, 8, n_w), F32),
        compiler_params=_params("parallel", "parallel"),
        name="modulation",
    )(cond, w_ada, b_ada.reshape(depth, 1, n_w))


def _embed_kernel(x_ref, tr_ref, tc_ref, o_ref, *, rows_per_tile, tiles_per_seq):
    r0 = (pl.program_id(0) % tiles_per_seq) * rows_per_tile
    half = D_MODEL // 2
    tc = tc_ref[...]
    for k in range(rows_per_tile):
        lo, hi = k * GRID_W, (k + 1) * GRID_W
        o_ref[lo:hi, 0:half] = x_ref[lo:hi, 0:half] + tr_ref[pl.ds(r0 + k, 1), :]
        o_ref[lo:hi, half:D_MODEL] = x_ref[lo:hi, half:D_MODEL] + tc


def _embed(x_sample):
    n_seq, length, _ = x_sample.shape
    rows = length // GRID_W
    quarter = D_MODEL // 4
    omega = 1.0 / (POS_BASE ** (jnp.arange(quarter, dtype=F32) / quarter))
    ang_r = jnp.arange(rows, dtype=F32)[:, None] * omega
    ang_c = jnp.arange(GRID_W, dtype=F32)[:, None] * omega
    tab_r = jnp.concatenate([jnp.sin(ang_r), jnp.cos(ang_r)], axis=-1)
    tab_c = jnp.concatenate([jnp.sin(ang_c), jnp.cos(ang_c)], axis=-1)
    rows_per_tile = 8
    tile = rows_per_tile * GRID_W
    tiles_per_seq = length // tile
    x2 = x_sample.reshape(n_seq * length, D_MODEL)
    return pl.pallas_call(
        functools.partial(_embed_kernel, rows_per_tile=rows_per_tile, tiles_per_seq=tiles_per_seq),
        grid=(n_seq * tiles_per_seq,),
        in_specs=[pl.BlockSpec((tile, D_MODEL), lambda i: (i, 0)),
                  pl.BlockSpec((rows, D_MODEL // 2), lambda i: (0, 0)),
                  pl.BlockSpec((GRID_W, D_MODEL // 2), lambda i: (0, 0))],
        out_specs=pl.BlockSpec((tile, D_MODEL), lambda i: (i, 0)),
        out_shape=jax.ShapeDtypeStruct(x2.shape, F32),
        compiler_params=_params("parallel"),
        name="pos_embed",
    )(x2, tab_r, tab_c)


def _norm_mod(x, g, shift, scale):
    ms = jnp.mean(x * x, axis=-1, keepdims=True)
    h = x * lax.rsqrt(ms + EPS) * g
    return h * (1.0 + scale) + shift


LANES = 128
SLOTS = LANES // S5_P


def _regroup_in(u_refs, ug_ref):
    blocks = TM // S5_T
    slot = lax.broadcasted_iota(jnp.int32, (blocks, LANES), 1) // S5_P
    rot = {}
    for j in range(S5_T):
        for a in range(S5_W // LANES):
            t = u_refs[a][pl.ds(j, blocks, stride=S5_T), :]
            rot[j, a] = t if j % SLOTS == 0 else pltpu.roll(t, S5_P * (j % SLOTS), 1)
    for g in range(S5_G):
        a, s = g // SLOTS, g % SLOTS
        for jt in range(S5_T // SLOTS):
            acc = None
            for sig in range(SLOTS):
                piece = rot[SLOTS * jt + (sig - s) % SLOTS, a]
                acc = piece if acc is None else jnp.where(slot == sig, piece, acc)
            ug_ref[g, :, jt * LANES:(jt + 1) * LANES] = acc.astype(ug_ref.dtype)


def _regroup_out(yg_ref, y_refs):
    blocks = yg_ref.shape[1]
    slot = lax.broadcasted_iota(jnp.int32, (blocks, LANES), 1) // S5_P
    for j in range(S5_T):
        jt = j // SLOTS
        for a in range(S5_W // LANES):
            acc = None
            for s in range(SLOTS):
                piece = yg_ref[SLOTS * a + s, :, jt * LANES:(jt + 1) * LANES].astype(F32)
                acc = piece if acc is None else jnp.where(slot == (s + j) % SLOTS, piece, acc)
            if j % SLOTS:
                acc = pltpu.roll(acc, LANES - S5_P * (j % SLOTS), 1)
            y_refs[a][pl.ds(j, blocks, stride=S5_T), :] = acc


def _in_kernel(x_ref, m_ref, g_ref, w_ref, wba_ref, wbat_ref,
               u_ref, ug_ref, z_ref, qkv_ref, zg_ref, ba_ref, bat_ref, edge_ref, *ut_refs):
    m = m_ref[0]
    h = _norm_mod(x_ref[...], g_ref[...], m[:, 0:D_MODEL], m[:, D_MODEL:2 * D_MODEL]).astype(BF16)
    o1, o2, o3, o4 = S5_W, 2 * S5_W, 2 * S5_W + 3 * GDN_W, 2 * S5_W + 4 * GDN_W
    u = _dot(h, w_ref[:, 0:o1])
    u_ref[...] = u.astype(u_ref.dtype)
    for a, ut_ref in enumerate(ut_refs):
        ut_ref[...] = u[:, a * LANES:(a + 1) * LANES]
    _regroup_in(ut_refs, ug_ref)
    z_ref[...] = _dot(h, w_ref[:, o1:o2]).astype(z_ref.dtype)
    qkv = _dot(h, w_ref[:, o2:o3]).astype(qkv_ref.dtype)
    qkv_ref[...] = qkv
    for t in range(TM // GT):
        edge_ref[t, 0:1, :] = qkv[t * GT:t * GT + 1]
        edge_ref[t, 1:2, :] = qkv[(t + 1) * GT - 1:(t + 1) * GT]
    zg_ref[...] = _dot(h, w_ref[:, o3:o4]).astype(zg_ref.dtype)
    ba_ref[...] = _dot(h, wba_ref[...])
    bat_ref[...] = _dot_nt(wbat_ref[...], h)


def _mod_row_map(n_ctx_tiles, tiles_per_lat):
    def index(i):
        return (jnp.where(i < n_ctx_tiles, 0, 1 + (i - n_ctx_tiles) // tiles_per_lat), 0, 0)
    return index


def _in_proj(x, mods_l, g, w_in, w_ba, w_bat, layer, mod_map):
    tokens = x.shape[0]
    const = lambda i: (0, 0)
    tile = lambda n: pl.BlockSpec((TM, n), lambda i: (i, 0))
    weight = lambda w: pl.BlockSpec((None,) + w.shape[1:], lambda i: (layer, 0, 0))
    return pl.pallas_call(
        _in_kernel,
        grid=(tokens // TM,),
        in_specs=[tile(D_MODEL), pl.BlockSpec((1, 1, N_MOD * D_MODEL), mod_map),
                  pl.BlockSpec(g.shape, const), weight(w_in), weight(w_ba), weight(w_bat)],
        out_specs=[tile(S5_W), pl.BlockSpec((S5_G, TM // S5_T, S5_TP), lambda i: (0, i, 0)),
                   tile(S5_W), tile(3 * GDN_W), tile(GDN_W), tile(GATE_PAD),
                   pl.BlockSpec((N_GATE, TM), lambda i: (0, i)),
                   pl.BlockSpec((TM // GT, 2, 3 * GDN_W), lambda i: (i, 0, 0))],
        out_shape=[jax.ShapeDtypeStruct((tokens, S5_W), BF16),
                   jax.ShapeDtypeStruct((S5_G, tokens // S5_T, S5_TP), BF16),
                   jax.ShapeDtypeStruct((tokens, S5_W), BF16),
                   jax.ShapeDtypeStruct((tokens, 3 * GDN_W), BF16),
                   jax.ShapeDtypeStruct((tokens, GDN_W), BF16),
                   jax.ShapeDtypeStruct((tokens, GATE_PAD), F32),
                   jax.ShapeDtypeStruct((N_GATE, tokens), F32),
                   jax.ShapeDtypeStruct((tokens // GT, 2, 3 * GDN_W), BF16)],
        scratch_shapes=[pltpu.VMEM((TM, LANES), F32)] * (S5_W // LANES),
        compiler_params=_params("parallel"),
        name="in_proj",
    )(x, mods_l, g, w_in, w_ba, w_bat)


N_POW = 8


def _zoh(lam_re, lam_im, log_dt):
    lam_re = jnp.minimum(lam_re, -1e-4)
    dt = jnp.exp(log_dt)
    x, th = lam_re * dt, lam_im * dt
    mag = jnp.exp(x)
    num_re, num_im = mag * jnp.cos(th) - 1.0, mag * jnp.sin(th)
    den = lam_re * lam_re + lam_im * lam_im
    return (x, th, (num_re * lam_re + num_im * lam_im) / den, (num_im * lam_re - num_re * lam_im) / den)


def _power(x, th, e):
    mag = jnp.exp(x * e)
    return mag * jnp.cos(th * e), mag * jnp.sin(th * e)


def _shift_lanes(lo, hi, t, left):
    if t == 0:
        return lo, hi
    lane = lax.broadcasted_iota(jnp.int32, lo.shape, 1)
    zero = jnp.zeros_like(lo)
    r = t % LANES
    if not left:
        if t >= LANES:
            return zero, (lo if r == 0 else jnp.where(lane >= r, pltpu.roll(lo, r, 1), 0.0))
        rl, rh = pltpu.roll(lo, r, 1), pltpu.roll(hi, r, 1)
        return jnp.where(lane >= r, rl, 0.0), jnp.where(lane >= r, rh, rl)
    if t >= LANES:
        return (hi if r == 0 else jnp.where(lane < LANES - r, pltpu.roll(hi, LANES - r, 1), 0.0)), zero
    rl, rh = pltpu.roll(lo, LANES - r, 1), pltpu.roll(hi, LANES - r, 1)
    return jnp.where(lane < LANES - r, rl, rh), jnp.where(lane < LANES - r, rh, 0.0)


def _s5_ops_kernel(prow_ref, pcol_ref, bb_ref, ct_ref, k_ref, bf_ref, bbk_ref, cf_ref, cb_ref, ap_ref,
                   knat_ref):
    s16 = (pl.program_id(1) % SLOTS) * S5_P
    prow = prow_ref[0, 0]
    pcol = pcol_ref[0, 0]
    lane = lax.broadcasted_iota(jnp.int32, (1, 2 * S5_N), 1)
    sign_k = jnp.where(lane < S5_N, 1.0, -1.0)
    rep = jnp.where(lax.broadcasted_iota(jnp.int32, (S5_P, S5_TP), 1) % S5_P
                    == lax.broadcasted_iota(jnp.int32, (S5_P, S5_TP), 0), 1.0, 0.0)
    exact = functools.partial(jnp.dot, precision=HIGHEST, preferred_element_type=F32)
    e_rows = lax.broadcasted_iota(jnp.int32, (S5_T, 1), 0).astype(F32)
    e_lanes = lax.broadcasted_iota(jnp.int32, (1, LANES), 1).astype(F32)
    tok_r = lax.broadcasted_iota(jnp.int32, (S5_TP, S5_T), 0) // S5_P
    e_r = lax.broadcasted_iota(jnp.int32, (S5_TP, S5_T), 1)
    tok_l = lax.broadcasted_iota(jnp.int32, (LANES, S5_TP), 1) // S5_P
    e_l = lax.broadcasted_iota(jnp.int32, (LANES, S5_TP), 0)
    pick = lambda cond: jnp.where(cond, 1.0, 0.0)
    kall = []
    for d in range(N_DIR):
        x_r, th_r, fr_r, fi_r = _zoh(prow[d:d + 1], prow[2 + d:3 + d], prow[4 + d:5 + d])
        x_c, th_c, _, _ = _zoh(pcol[:, d:d + 1], pcol[:, 2 + d:3 + d], pcol[:, 4 + d:5 + d])
        bpack, bswap = bb_ref[0, 0, d, 0], bb_ref[0, 0, d, 1]
        bbar = fr_r * bpack + fi_r * bswap
        bbar_sw = fr_r * bswap - fi_r * bpack
        pr, pi = _power(x_r, th_r, e_rows)
        sel = pick(e_r == ((S5_T - 1) - tok_r if d == 0 else tok_r))
        pr, pi = exact(sel, pr), exact(sel, pi)
        b_nat = pr * jnp.concatenate([bbar] * S5_T, axis=0) + pi * jnp.concatenate([bbar_sw] * S5_T, axis=0)
        b_out = bf_ref if d == 0 else bbk_ref
        for rt in range(S5_TP // LANES):
            b_out[0, 0, rt * LANES:(rt + 1) * LANES, :] = pltpu.roll(
                b_nat[rt * LANES:(rt + 1) * LANES], s16, 0).astype(b_out.dtype)
        c_re = jnp.dot(ct_ref[0, 0, d, 0], rep, precision=HIGHEST, preferred_element_type=F32)
        c_im = jnp.dot(ct_ref[0, 0, d, 1], rep, precision=HIGHEST, preferred_element_type=F32)
        pw_r, pw_i = _power(x_c, th_c, e_lanes)
        sel = pick(e_l == (tok_l + 1 if d == 0 else S5_T - tok_l))
        pr, pi = exact(pw_r, sel), exact(pw_i, sel)
        c_nat = jnp.concatenate([c_re * pr - c_im * pi, -(c_re * pi + c_im * pr)], axis=0)
        c_out = cf_ref if d == 0 else cb_ref
        for ct in range(S5_TP // LANES):
            c_out[0, 0, :, ct * LANES:(ct + 1) * LANES] = pltpu.roll(
                c_nat[:, ct * LANES:(ct + 1) * LANES], s16, 1).astype(c_out.dtype)
        sel = pick(e_l == (tok_l if d == 0 else (S5_T - 1) - tok_l))
        pr, pi = exact(pw_r, sel), exact(pw_i, sel)
        ca = jnp.concatenate([c_re * pr - c_im * pi, c_re * pi + c_im * pr], axis=0)
        kall.append(jnp.dot(bbar * sign_k, ca, precision=HIGHEST, preferred_element_type=F32))
        row = lax.broadcasted_iota(jnp.int32, (2 * N_POW, 1), 0)
        expo = (S5_T * jnp.left_shift(1, row // 2)).astype(F32)
        pr, pi = _power(x_r, th_r, expo)
        ap_ref[0, 0, d * 2 * N_POW:(d + 1) * 2 * N_POW, :] = jnp.where(row % 2 == 0, pr, -pi * sign_k)

    kf_lo, kf_hi = kall[0][:, :LANES], kall[0][:, LANES:]
    kb_lo, kb_hi = kall[1][:, :LANES], kall[1][:, LANES:]
    for j in range(S5_T):
        f_lo, f_hi = _shift_lanes(kf_lo, kf_hi, S5_P * j, False)
        b_lo, b_hi = _shift_lanes(kb_lo, kb_hi, S5_P * (S5_T - 1 - j), True)
        knat_ref[j * S5_P:(j + 1) * S5_P, 0:LANES] = f_lo + b_lo
        knat_ref[j * S5_P:(j + 1) * S5_P, LANES:2 * LANES] = f_hi + b_hi
    for rt in range(S5_TP // LANES):
        for ct in range(S5_TP // LANES):
            tile = knat_ref[rt * LANES:(rt + 1) * LANES, ct * LANES:(ct + 1) * LANES]
            k_ref[0, 0, rt * LANES:(rt + 1) * LANES, ct * LANES:(ct + 1) * LANES] = pltpu.roll(
                pltpu.roll(tile, s16, 0), s16, 1).astype(k_ref.dtype)


def _s5_operators(lam_re, lam_im, log_dt, b_re, b_im, c_re, c_im):
    depth = lam_re.shape[0]
    dup = lambda a: jnp.concatenate([a, a], axis=-1)
    ldt = jnp.broadcast_to(log_dt[..., None], lam_re.shape)
    g_major = lambda a: jnp.moveaxis(a, 1, 2)
    prow = jnp.concatenate([dup(g_major(a)) for a in (lam_re, lam_im, ldt)], axis=2)
    pcol = jnp.concatenate([g_major(a) for a in (lam_re, lam_im, ldt)], axis=2)
    pcol = jnp.pad(jnp.swapaxes(pcol, 2, 3), ((0, 0), (0, 0), (0, 0), (0, 2)))
    bt_re, bt_im = (jnp.swapaxes(g_major(a), 3, 4) for a in (b_re, b_im))
    bb = jnp.stack([jnp.concatenate([bt_re, bt_im], axis=-1),
                    jnp.concatenate([-bt_im, bt_re], axis=-1)], axis=3)
    ct = jnp.stack([jnp.swapaxes(g_major(a), 3, 4) for a in (c_re, c_im)], axis=3)
    per = lambda a: pl.BlockSpec((1, 1) + a.shape[2:], lambda l, g: (l, g) + (0,) * (a.ndim - 2))
    mat = lambda r, c: pl.BlockSpec((1, 1, r, c), lambda l, g: (l, g, 0, 0))
    shape = lambda r, c, dt: jax.ShapeDtypeStruct((depth, S5_G, r, c), dt)
    return pl.pallas_call(
        _s5_ops_kernel,
        grid=(depth, S5_G),
        in_specs=[per(prow), per(pcol), per(bb), per(ct)],
        out_specs=[mat(S5_TP, S5_TP), mat(S5_TP, 2 * S5_N), mat(S5_TP, 2 * S5_N),
                   mat(2 * S5_N, S5_TP), mat(2 * S5_N, S5_TP), mat(N_DIR * 2 * N_POW, 2 * S5_N)],
        out_shape=[shape(S5_TP, S5_TP, BF16), shape(S5_TP, 2 * S5_N, BF16), shape(S5_TP, 2 * S5_N, BF16),
                   shape(2 * S5_N, S5_TP, BF16), shape(2 * S5_N, S5_TP, BF16),
                   shape(N_DIR * 2 * N_POW, 2 * S5_N, F32)],
        scratch_shapes=[pltpu.VMEM((S5_TP, S5_TP), F32)],
        compiler_params=_params("parallel", "parallel"),
        name="s5_operators",
    )(prow, pcol, bb, ct)


def _lane_swap(x):
    return pltpu.roll(x, S5_N, 1)


def _cmul(a1, a2, h):
    return a1 * h + a2 * _lane_swap(h)


def _block_scan(s, seg, a_pow, d, reverse):
    rows = s.shape[0]
    ridx = lax.broadcasted_iota(jnp.int32, s.shape, 0) % seg
    h = s
    k, step = 0, 1
    while step < seg:
        a1 = a_pow[d * 16 + 2 * k:d * 16 + 2 * k + 1]
        a2 = a_pow[d * 16 + 2 * k + 1:d * 16 + 2 * k + 2]
        if reverse:
            sh = jnp.where(ridx < seg - step, pltpu.roll(h, rows - step, 0), 0.0)
        else:
            sh = jnp.where(ridx >= step, pltpu.roll(h, step, 0), 0.0)
        h = h + _cmul(a1, a2, sh)
        k += 1
        step *= 2
    return h


def _s5_kernel(u_ref, k_ref, bf_ref, bb_ref, cf_ref, cb_ref, ap_ref, h0_ref,
               y_ref, fin_ref, hf_ref, hb_ref, *, ctx_rows, ctx_seg, lat_seg, n_ctx, n_lat):
    u = u_ref[0]
    a_pow = ap_ref[0]
    s_f = _dot(u, bf_ref[0])
    s_b = _dot(u, bb_ref[0])
    lat_rows = u.shape[0] - ctx_rows

    lrow = lax.broadcasted_iota(jnp.int32, (lat_rows, 2 * S5_N), 0)
    lmod = lrow % lat_seg
    lseq = lrow // lat_seg
    h0 = h0_ref[0]
    h0_f = jnp.zeros((lat_rows, 2 * S5_N), F32)
    h0_b = jnp.zeros((lat_rows, 2 * S5_N), F32)
    for s in range(n_lat):
        h0_f = jnp.where(lseq == s, h0[2 * s:2 * s + 1], h0_f)
        h0_b = jnp.where(lseq == s, h0[2 * s + 1:2 * s + 2], h0_b)
    first, last = lmod == 0, lmod == lat_seg - 1
    sl_f = s_f[ctx_rows:] + jnp.where(first, _cmul(a_pow[0:1], a_pow[1:2], h0_f), 0.0)
    sl_b = s_b[ctx_rows:] + jnp.where(last, _cmul(a_pow[16:17], a_pow[17:18], h0_b), 0.0)

    hc_f = _block_scan(s_f[:ctx_rows], ctx_seg, a_pow, 0, False)
    hc_b = _block_scan(s_b[:ctx_rows], ctx_seg, a_pow, 1, True)
    hl_f = _block_scan(sl_f, lat_seg, a_pow, 0, False)
    hl_b = _block_scan(sl_b, lat_seg, a_pow, 1, True)

    cmod = lax.broadcasted_iota(jnp.int32, (ctx_rows, 2 * S5_N), 0) % ctx_seg
    pc_f = jnp.where(cmod == 0, 0.0, pltpu.roll(hc_f, 1, 0))
    pc_b = jnp.where(cmod == ctx_seg - 1, 0.0, pltpu.roll(hc_b, ctx_rows - 1, 0))
    pl_f = jnp.where(first, h0_f, pltpu.roll(hl_f, 1, 0))
    pl_b = jnp.where(last, h0_b, pltpu.roll(hl_b, lat_rows - 1, 0))
    prev_f = jnp.concatenate([pc_f, pl_f], axis=0).astype(BF16)
    prev_b = jnp.concatenate([pc_b, pl_b], axis=0).astype(BF16)
    y_ref[0] = (_dot(u, k_ref[0]) + _dot(prev_f, cf_ref[0]) + _dot(prev_b, cb_ref[0])).astype(y_ref.dtype)

    hf_ref[...] = hc_f
    hb_ref[...] = hc_b
    fin_ref[0, 0] = hf_ref[pl.ds(ctx_seg - 1, n_ctx, stride=ctx_seg), :]
    fin_ref[0, 1] = hb_ref[pl.ds(0, n_ctx, stride=ctx_seg), :]


def _s5_mixer(u_g, ops, h0, n_ctx, ctx_len, n_lat, lat_len):
    k_mat, b_f, b_b, c_f, c_b, a_pow = ops
    rows = u_g.shape[1]
    ctx_seg, lat_seg = ctx_len // S5_T, lat_len // S5_T
    ctx_rows = n_ctx * ctx_seg
    per_g = lambda a: pl.BlockSpec((1,) + a.shape[1:], lambda g: (g,) + (0,) * (a.ndim - 1))
    kern = functools.partial(_s5_kernel, ctx_rows=ctx_rows, ctx_seg=ctx_seg, lat_seg=lat_seg,
                             n_ctx=n_ctx, n_lat=n_lat)
    return pl.pallas_call(
        kern,
        grid=(S5_G,),
        in_specs=[per_g(a) for a in (u_g, k_mat, b_f, b_b, c_f, c_b, a_pow, h0)],
        out_specs=[pl.BlockSpec((1, rows, S5_TP), lambda g: (g, 0, 0)),
                   pl.BlockSpec((1, N_DIR, n_ctx, 2 * S5_N), lambda g: (g, 0, 0, 0))],
        out_shape=[jax.ShapeDtypeStruct((S5_G, rows, S5_TP), BF16),
                   jax.ShapeDtypeStruct((S5_G, N_DIR, n_ctx, 2 * S5_N), F32)],
        scratch_shapes=[pltpu.VMEM((ctx_rows, 2 * S5_N), F32)] * N_DIR,
        compiler_params=_params("parallel"),
        name="s5_mixer",
    )(u_g, k_mat, b_f, b_b, c_f, c_b, a_pow, h0)


def _gdn_pre_kernel(x_ref, hp_ref, hn_ref, cw_ref, ba_ref, bat_ref, al_ref, dl_ref, ar_ref, dr_ref,
                    u_ref, wq_ref, kd_ref, a_ref, eg_ref,
                    q_ref, k_ref, v_ref, col_ref, row_ref, m_ref, acc_ref):
    x = x_ref[...].astype(F32)
    rid = lax.broadcasted_iota(jnp.int32, x.shape, 0)
    x_prev = jnp.where(rid == 0, hp_ref[0].astype(F32), pltpu.roll(x, 1, 0))
    x_next = jnp.where(rid == GT - 1, hn_ref[0].astype(F32), pltpu.roll(x, GT - 1, 0))
    cw = cw_ref[...]
    y = _silu(x_prev * cw[0:1] + x * cw[1:2] + x_next * cw[2:3])
    for h in range(GDN_H):
        lo, hi = h * GDN_DK, (h + 1) * GDN_DK
        qh = y[:, lo:hi]
        kh = y[:, GDN_W + lo:GDN_W + hi]
        q_ref[:, lo:hi] = qh * (lax.rsqrt(jnp.sum(qh * qh, axis=-1, keepdims=True) + EPS)
                                * (GDN_DK ** -0.5))
        k_ref[:, lo:hi] = kh * lax.rsqrt(jnp.sum(kh * kh, axis=-1, keepdims=True) + EPS)
    v_ref[...] = y[:, 2 * GDN_W:]

    half = N_GATE // 2
    ba = ba_ref[...]
    lane = lax.broadcasted_iota(jnp.int32, ba.shape, 1)
    gate = jnp.where(lane < half, _sigmoid(ba), -jnp.exp(al_ref[...]) * _softplus(ba + dl_ref[...]))
    ri = lax.broadcasted_iota(jnp.int32, (GT, GT), 0)
    ci = lax.broadcasted_iota(jnp.int32, (GT, GT), 1)
    same = (ri // CHUNK) == (ci // CHUNK)
    lower = jnp.where(same & (ci <= ri), 1.0, 0.0)
    upper = jnp.where(same & (ci >= ri), 1.0, 0.0)
    cum_f = jnp.dot(lower, gate, precision=HIGHEST, preferred_element_type=F32)
    cum_b = jnp.dot(upper, gate, precision=HIGHEST, preferred_element_type=F32)
    fwd_lane = (lane >= half) & (lane < half + GDN_H)
    bwd_lane = (lane >= half + GDN_H) & (lane < N_GATE)
    col_ref[...] = jnp.where(fwd_lane, cum_f, jnp.where(bwd_lane, cum_b, gate))
    bat = bat_ref[...]
    g_row = -jnp.exp(ar_ref[...]) * _softplus(bat + dr_ref[...])
    row_f = jnp.dot(g_row, upper, precision=HIGHEST, preferred_element_type=F32)
    row_b = jnp.dot(g_row, lower, precision=HIGHEST, preferred_element_type=F32)
    srow = lax.broadcasted_iota(jnp.int32, bat.shape, 0)
    row_ref[...] = jnp.where(srow < half + GDN_H, row_f, row_b)

    assert 2 * CHUNK == LANES and GT % LANES == 0
    ii = lax.broadcasted_iota(jnp.int32, (CHUNK, LANES), 0)
    ll = lax.broadcasted_iota(jnp.int32, (CHUNK, LANES), 1)
    jj = ll % CHUNK
    fwd = ll < CHUNK
    incl = (fwd & (jj <= ii)) | (~fwd & (jj >= ii))
    strict = (fwd & (jj < ii)) | (~fwd & (jj > ii))
    fwd1 = lax.broadcasted_iota(jnp.int32, (1, LANES), 1) < CHUNK
    keep_f = jnp.where(fwd, 1.0, 0.0).astype(BF16)
    keep_b = jnp.where(fwd, 0.0, 1.0).astype(BF16)
    pairs = [(c, h) for c in range(GT // CHUNK) for h in range(GDN_H)]

    def both(c, lane_f, lane_b):
        rows = slice(c * CHUNK, (c + 1) * CHUNK)
        return jnp.where(fwd, col_ref[rows, lane_f:lane_f + 1], col_ref[rows, lane_b:lane_b + 1])

    def block_diag(x16):
        return jnp.concatenate([x16 * keep_f, x16 * keep_b], axis=0)

    for n, (c, h) in enumerate(pairs):
        rows = slice(c * CHUNK, (c + 1) * CHUNK)
        cols = slice(h * GDN_DK, (h + 1) * GDN_DK)
        beta = both(c, h, GDN_H + h)
        gc = both(c, half + h, half + GDN_H + h)
        lt = slice((c // 2) * LANES, (c // 2 + 1) * LANES)
        r_f = row_ref[half + h:half + h + 1, lt]
        r_b = row_ref[half + GDN_H + h:half + GDN_H + h + 1, lt]
        gr = (jnp.where(fwd1, r_f, pltpu.roll(r_b, CHUNK, 1)) if c % 2 == 0
              else jnp.where(fwd1, pltpu.roll(r_f, CHUNK, 1), r_b))
        decay = jnp.where(incl, jnp.exp(jnp.where(incl, gc - gr, 0.0)), 0.0)
        k16 = k_ref[rows, cols].astype(BF16)
        kk16 = jnp.concatenate([k16, k16], axis=0)
        m = jnp.where(strict, beta * _dot_nt(k16, kk16) * decay, 0.0)
        a_ref[rows, cols] = (_dot_nt(q_ref[rows, cols].astype(BF16), kk16) * decay).astype(BF16)
        m_ref[n] = m
        acc_ref[n] = jnp.where((ii // 2) == (jj // 2), -m, 0.0)

    b = 2
    while b < CHUNK:
        pair = ((ii // (2 * b)) == (jj // (2 * b))) & ((ii // b) != (jj // b))
        for n in range(len(pairs)):
            cpart = jnp.where(pair, m_ref[n], 0.0)
            acc = acc_ref[n]
            a16 = acc.astype(BF16)
            y = cpart + _dot(a16, block_diag(cpart.astype(BF16)))
            acc_ref[n] = acc - y - _dot(y.astype(BF16), block_diag(a16))
        b *= 2

    zeros = jnp.zeros((CHUNK, 2 * GDN_DK), BF16)
    for n, (c, h) in enumerate(pairs):
        rows = slice(c * CHUNK, (c + 1) * CHUNK)
        cols = slice(h * GDN_DK, (h + 1) * GDN_DK)
        kh, vh, qh = k_ref[rows, cols], v_ref[rows, cols], q_ref[rows, cols]
        rhs, egc, gcs = [], [], []
        for d in range(N_DIR):
            lane_ = d * GDN_H + h
            beta = col_ref[rows, lane_:lane_ + 1]
            gc = col_ref[rows, half + lane_:half + lane_ + 1]
            gcs.append(gc)
            egc.append(jnp.exp(gc))
            rhs.append(jnp.concatenate([vh * beta, kh * beta * egc[d]], axis=1))
        stacked = jnp.concatenate(
            [jnp.concatenate([rhs[0].astype(BF16), zeros], axis=1),
             jnp.concatenate([zeros, rhs[1].astype(BF16)], axis=1)], axis=0)
        prod = _dot(acc_ref[n].astype(BF16), stacked)
        for d in range(N_DIR):
            sol = rhs[d] + prod[:, d * 2 * GDN_DK:(d + 1) * 2 * GDN_DK]
            u_ref[d, rows, cols] = sol[:, :GDN_DK]
            wq_ref[d, c, 0:CHUNK, cols] = sol[:, GDN_DK:].astype(BF16)
            wq_ref[d, c, CHUNK:2 * CHUNK, cols] = (qh * egc[d]).astype(BF16)
            g_last = gcs[d][0:1] if d == 1 else gcs[d][CHUNK - 1:CHUNK]
            kd_ref[d, rows, cols] = (kh * jnp.exp(g_last - gcs[d])).astype(BF16)
            eg_ref[c, d * GDN_H + h:d * GDN_H + h + 1, :] = jnp.broadcast_to(jnp.exp(g_last), (1, GDN_DK))


def _gdn_pre(qkv, halo_prev, halo_next, conv_w, ba, bat, a_log, dt_bias):
    tokens = qkv.shape[0]
    n_tiles = tokens // GT
    cpt = GT // CHUNK
    half = N_GATE // 2
    al = a_log.reshape(-1)
    db = dt_bias.reshape(-1)
    al_l = jnp.zeros((1, GATE_PAD), F32).at[0, half:N_GATE].set(al)
    db_l = jnp.zeros((1, GATE_PAD), F32).at[0, half:N_GATE].set(db)
    al_r = jnp.zeros((N_GATE, 1), F32).at[half:, 0].set(al)
    db_r = jnp.zeros((N_GATE, 1), F32).at[half:, 0].set(db)
    const = lambda i: (0, 0)
    tile = lambda n: pl.BlockSpec((GT, n), lambda i: (i, 0))
    halo = pl.BlockSpec((1, 1, 3 * GDN_W), lambda i: (i, 0, 0))
    n_chunks = tokens // CHUNK
    n_pairs = cpt * GDN_H
    return pl.pallas_call(
        _gdn_pre_kernel,
        grid=(n_tiles,),
        in_specs=[tile(3 * GDN_W), halo, halo, pl.BlockSpec(conv_w.shape, const),
                  tile(GATE_PAD), pl.BlockSpec((N_GATE, GT), lambda i: (0, i)),
                  pl.BlockSpec((1, GATE_PAD), const), pl.BlockSpec((1, GATE_PAD), const),
                  pl.BlockSpec((N_GATE, 1), const), pl.BlockSpec((N_GATE, 1), const)],
        out_specs=[pl.BlockSpec((N_DIR, GT, GDN_W), lambda i: (0, i, 0)),
                   pl.BlockSpec((N_DIR, cpt, 2 * CHUNK, GDN_W), lambda i: (0, i, 0, 0)),
                   pl.BlockSpec((N_DIR, GT, GDN_W), lambda i: (0, i, 0)),
                   pl.BlockSpec((GT, GDN_W), lambda i: (i, 0)),
                   pl.BlockSpec((cpt, N_DIR * GDN_H, GDN_DK), lambda i: (i, 0, 0))],
        out_shape=[jax.ShapeDtypeStruct((N_DIR, tokens, GDN_W), F32),
                   jax.ShapeDtypeStruct((N_DIR, n_chunks, 2 * CHUNK, GDN_W), BF16),
                   jax.ShapeDtypeStruct((N_DIR, tokens, GDN_W), BF16),
                   jax.ShapeDtypeStruct((tokens, GDN_W), BF16),
                   jax.ShapeDtypeStruct((n_chunks, N_DIR * GDN_H, GDN_DK), F32)],
        scratch_shapes=[pltpu.VMEM((GT, GDN_W), F32)] * 3
                       + [pltpu.VMEM((GT, GATE_PAD), F32), pltpu.VMEM((N_GATE, GT), F32),
                          pltpu.VMEM((n_pairs, CHUNK, N_DIR * CHUNK), F32),
                          pltpu.VMEM((n_pairs, CHUNK, N_DIR * CHUNK), F32)],
        compiler_params=_params("parallel"),
        name="gdn_pre",
    )(qkv, halo_prev, halo_next, conv_w, ba, bat, al_l, db_l, al_r, db_r)


def _gdn_scan_kernel(*refs, has_s0, has_fin):
    per_dir = 5
    pos = NS * N_DIR * per_dir
    ins = [refs[k * per_dir:(k + 1) * per_dir] for k in range(NS * N_DIR)]
    s0_ref = None
    if has_s0:
        s0_ref = refs[pos]
        pos += 1
    of_ref, ob_ref = refs[pos], refs[pos + 1]
    pos += 2
    fin_ref = None
    if has_fin:
        fin_ref = refs[pos]
        pos += 1
    s_ref = refs[pos]
    j = pl.program_id(1)

    @pl.when(j == 0)
    def _():
        if has_s0:
            s_ref[...] = s0_ref[...]
        else:
            s_ref[...] = jnp.zeros(s_ref.shape, F32)

    cpb = SB // CHUNK
    for step in range(cpb):
        work = []
        for p in range(NS):
            for d in range(N_DIR):
                u_r, wq_r, kd_r, a_r, eg_r = ins[p * N_DIR + d]
                o_r = of_ref if d == 0 else ob_ref
                c = step if d == 0 else cpb - 1 - step
                rows = slice(c * CHUNK, (c + 1) * CHUNK)
                for h in range(GDN_H):
                    work.append((p, d, h, c, rows, slice(h * GDN_DK, (h + 1) * GDN_DK), u_r, wq_r, kd_r,
                                 a_r, eg_r, o_r))
        ps = [_dot(wq_r[0, c, :, cols], s_ref[p, d, h].astype(BF16))
              for (p, d, h, c, rows, cols, u_r, wq_r, kd_r, a_r, eg_r, o_r) in work]
        v16 = [(u_r[0, rows, cols] - q[:CHUNK]).astype(BF16)
               for q, (p, d, h, c, rows, cols, u_r, wq_r, kd_r, a_r, eg_r, o_r) in zip(ps, work)]
        zero = jnp.zeros((CHUNK, GDN_DK), BF16)
        for q, v, (p, d, h, c, rows, cols, u_r, wq_r, kd_r, a_r, eg_r, o_r) in zip(ps, v16, work):
            v2 = jnp.concatenate([v, zero] if d == 0 else [zero, v], axis=0)
            o_r[0, p, rows, cols] = (q[CHUNK:] + _dot(a_r[rows, cols], v2)).astype(o_r.dtype)
        for v, (p, d, h, c, rows, cols, u_r, wq_r, kd_r, a_r, eg_r, o_r) in zip(v16, work):
            eg = eg_r[c, d * GDN_H + h:d * GDN_H + h + 1, :]
            s_ref[p, d, h] = s_ref[p, d, h] * eg + _dot_tn(kd_r[0, rows, cols], v)

    if has_fin:
        @pl.when(j == pl.num_programs(1) - 1)
        def _():
            fin_ref[...] = s_ref[...]


def _gdn_scan(u, wq, kd, a, eg, s0, n_seq, length, base_tok, want_fin):
    assert n_seq % NS == 0
    n_b = length // SB
    cpb = SB // CHUNK
    base = base_tok // SB
    in_specs, args = [], []
    for p in range(NS):
        for d in range(N_DIR):
            if d == 0:
                blk = lambda s, j, p=p: base + (s * NS + p) * n_b + j
            else:
                blk = lambda s, j, p=p: base + (s * NS + p) * n_b + (n_b - 1 - j)
            in_specs += [
                pl.BlockSpec((1, SB, GDN_W), lambda s, j, d=d, blk=blk: (d, blk(s, j), 0)),
                pl.BlockSpec((1, cpb, 2 * CHUNK, GDN_W), lambda s, j, d=d, blk=blk: (d, blk(s, j), 0, 0)),
                pl.BlockSpec((1, SB, GDN_W), lambda s, j, d=d, blk=blk: (d, blk(s, j), 0)),
                pl.BlockSpec((SB, GDN_W), lambda s, j, blk=blk: (blk(s, j), 0)),
                pl.BlockSpec((cpb, N_DIR * GDN_H, GDN_DK), lambda s, j, blk=blk: (blk(s, j), 0, 0))]
            args += [u, wq, kd, a, eg]
    state_spec = pl.BlockSpec((NS, N_DIR, GDN_H, GDN_DK, GDN_DK), lambda s, j: (s, 0, 0, 0, 0))
    if s0 is not None:
        in_specs.append(state_spec)
        args.append(s0)
    out_specs = [pl.BlockSpec((1, NS, SB, GDN_W), lambda s, j: (s, 0, j, 0)),
                 pl.BlockSpec((1, NS, SB, GDN_W), lambda s, j: (s, 0, n_b - 1 - j, 0))]
    out_shape = [jax.ShapeDtypeStruct((n_seq // NS, NS, length, GDN_W), BF16)] * 2
    if want_fin:
        out_specs.append(state_spec)
        out_shape.append(jax.ShapeDtypeStruct((n_seq, N_DIR, GDN_H, GDN_DK, GDN_DK), F32))
    outs = pl.pallas_call(
        functools.partial(_gdn_scan_kernel, has_s0=s0 is not None, has_fin=want_fin),
        grid=(n_seq // NS, n_b),
        in_specs=in_specs,
        out_specs=out_specs,
        out_shape=out_shape,
        scratch_shapes=[pltpu.VMEM((NS, N_DIR, GDN_H, GDN_DK, GDN_DK), F32)],
        compiler_params=_params("parallel", "arbitrary"),
        name="gdn_scan",
    )(*args)
    return [o.reshape(n_seq * length, GDN_W) for o in outs[:2]] + list(outs[2:])


def _out_mlp_kernel(x_ref, m_ref, yg_ref, u_ref, z_ref, dsk_ref, ofc_ref, obc_ref, ofl_ref, obl_ref,
                    zg_ref, gn_ref, w_ref, g2_ref, w1_ref, w2_ref, gf_ref, o_ref, *y5_refs,
                    n_ctx_tiles, first_tile, final):
    m = m_ref[0]
    _regroup_out(yg_ref, y5_refs)
    in_ctx = pl.program_id(0) + first_tile < n_ctx_tiles
    gn = gn_ref[...]
    halves = [slice(k * TO // 2, (k + 1) * TO // 2) for k in range(2)]
    xs, hs = [], []
    for r in halves:
        up = lambda ref: ref[r, :].astype(F32)
        y5 = jnp.concatenate([ref[r, :] for ref in y5_refs], axis=1)
        s5 = _gelu_tanh(y5 + up(u_ref) * dsk_ref[...]) * _sigmoid(up(z_ref))
        acc = _dot(s5.astype(BF16), w_ref[0:S5_W, :])
        o = jnp.where(in_ctx, up(ofc_ref) + up(obc_ref), up(ofl_ref) + up(obl_ref))
        zg = up(zg_ref)
        for h in range(GDN_H):
            lo, hi = h * GDN_DK, (h + 1) * GDN_DK
            oh = o[:, lo:hi]
            nh = oh * lax.rsqrt(jnp.mean(oh * oh, axis=-1, keepdims=True) + EPS) * gn
            gh = (nh * _silu(zg[:, lo:hi])).astype(BF16)
            acc = acc + _dot(gh, w_ref[S5_W + lo:S5_W + hi, :])
        x = x_ref[r, :] + m[:, 2 * D_MODEL:3 * D_MODEL] * acc
        xs.append(x)
        hs.append(_norm_mod(x, g2_ref[...], m[:, 3 * D_MODEL:4 * D_MODEL],
                            m[:, 4 * D_MODEL:5 * D_MODEL]).astype(BF16))
    for r, x, h in zip(halves, xs, hs):
        ff = jnp.maximum(_dot(h, w1_ref[...]), 0.0)
        ff = (ff * ff).astype(BF16)
        out = x + m[:, 5 * D_MODEL:6 * D_MODEL] * _dot(ff, w2_ref[...])
        if final:
            out = out * lax.rsqrt(jnp.mean(out * out, axis=-1, keepdims=True) + EPS) * gf_ref[...]
        o_ref[r, :] = out


def _out_mlp(x, mods_l, y_g, u, z, d_skip, o_ctx, o_lat, zg, gdn_norm, w_out, g2, w1, w2, layer,
             ctx_tok, lat_len, g_final=None, first_tok=0, n_tok=None):
    tokens = x.shape[0]
    n_tok = tokens if n_tok is None else n_tok
    assert ctx_tok % TO == 0 and lat_len % TO == 0 and tokens > ctx_tok
    assert first_tok % TO == 0 and n_tok % TO == 0
    n_ctx_tiles, t0 = ctx_tok // TO, first_tok // TO
    mod_row = _mod_row_map(n_ctx_tiles, lat_len // TO)
    const = lambda i: (0, 0)
    tile = lambda n: pl.BlockSpec((TO, n), lambda i: (i + t0, 0))
    ctx_tile = pl.BlockSpec((TO, GDN_W), lambda i: (jnp.minimum(i + t0, n_ctx_tiles - 1), 0))
    lat_tile = pl.BlockSpec((TO, GDN_W), lambda i: (jnp.maximum(i + t0 - n_ctx_tiles, 0), 0))
    weight = lambda w: pl.BlockSpec((None,) + w.shape[1:], lambda i: (layer, 0, 0),
                                    pipeline_mode=pl.Buffered(1))
    final = g_final is not None
    return pl.pallas_call(
        functools.partial(_out_mlp_kernel, n_ctx_tiles=n_ctx_tiles, first_tile=t0, final=final),
        grid=(n_tok // TO,),
        in_specs=[tile(D_MODEL),
                  pl.BlockSpec((1, 1, N_MOD * D_MODEL), lambda i: mod_row(i + t0)),
                  pl.BlockSpec((S5_G, TO // S5_T, S5_TP), lambda i: (0, i + t0, 0)),
                  tile(S5_W), tile(S5_W), pl.BlockSpec((1, S5_W), const),
                  ctx_tile, ctx_tile, lat_tile, lat_tile,
                  tile(GDN_W), pl.BlockSpec((1, GDN_DK), const),
                  weight(w_out), pl.BlockSpec((1, D_MODEL), const), weight(w1), weight(w2),
                  pl.BlockSpec((1, D_MODEL), const)],
        out_specs=pl.BlockSpec((TO, D_MODEL), lambda i: (i, 0)),
        out_shape=jax.ShapeDtypeStruct((n_tok, D_MODEL), F32),
        scratch_shapes=[pltpu.VMEM((TO, LANES), F32)] * (S5_W // LANES),
        compiler_params=_params("parallel"),
        name="out_mlp",
    )(x, mods_l, y_g, u, z, d_skip, *o_ctx, *o_lat, zg, gdn_norm, w_out, g2, w1, w2,
      g_final if final else g2)


def kernel(x_prompt, x_sample, state_s5_re, state_s5_im, state_gdn, c, c_ctx, norm_mix, norm_mlp, w_ada, b_ada, w_in, conv_qkv, s5_lambda_re, s5_lambda_im, s5_log_dt, s5_b_re, s5_b_im, s5_c_re, s5_c_im, s5_d, gdn_a_log, gdn_dt_bias, gdn_norm, w_out, w_mlp_in, w_mlp_out, norm_final):
    n_ctx, ctx_len, _ = x_prompt.shape
    n_lat, lat_len, _ = x_sample.shape
    depth = w_in.shape[0]
    assert ctx_len % TM == 0 and lat_len % TM == 0 and lat_len % (8 * GRID_W) == 0
    assert n_lat + 1 <= 8
    ctx_tok, lat_tok = n_ctx * ctx_len, n_lat * lat_len
    tokens = ctx_tok + lat_tok
    n_ctx_tiles, tiles_per_lat = ctx_tok // TM, lat_len // TM
    mod_map = _mod_row_map(n_ctx_tiles, tiles_per_lat)

    cond = jnp.zeros((8, D_MODEL), F32).at[0].set(c_ctx).at[1:1 + n_lat].set(c)
    mods = _modulation(cond, w_ada, b_ada)
    x = jnp.concatenate([x_prompt.reshape(ctx_tok, D_MODEL), _embed(x_sample)], axis=0)

    starts = np.arange(tokens // GT) * GT
    in_ctx = starts < ctx_tok
    pos = np.where(in_ctx, starts % ctx_len, (starts - ctx_tok) % lat_len)
    has_prev = pos != 0
    has_next = pos + GT != np.where(in_ctx, ctx_len, lat_len)

    s5_ops = _s5_operators(s5_lambda_re, s5_lambda_im, s5_log_dt, s5_b_re, s5_b_im, s5_c_re, s5_c_im)
    w_in16, w_out16 = w_in.astype(BF16), w_out.astype(BF16)
    w_mlp_in16, w_mlp_out16 = w_mlp_in.astype(BF16), w_mlp_out.astype(BF16)
    gate_cols = w_in16[:, :, 2 * S5_W + 4 * GDN_W:]
    w_ba = jnp.pad(gate_cols, ((0, 0), (0, 0), (0, GATE_PAD - N_GATE)))
    w_bat = jnp.swapaxes(gate_cols, 1, 2)
    fin_re, fin_im, fin_gdn = [], [], []
    for l in range(depth):
        mods_l = mods[l].reshape(8, 1, N_MOD * D_MODEL)
        u, u_g, z, qkv, zg, ba, bat, edge = _in_proj(x, mods_l, norm_mix[l].reshape(1, D_MODEL), w_in16,
                                                     w_ba, w_bat, l, mod_map)

        h0 = jnp.concatenate([state_s5_re[:, l], state_s5_im[:, l]], axis=-1)
        h0 = h0.transpose(2, 0, 1, 3).reshape(S5_G, n_lat * N_DIR, 2 * S5_N)
        y_g, s5_fin = _s5_mixer(u_g, tuple(o[l] for o in s5_ops), h0, n_ctx, ctx_len, n_lat, lat_len)
        s5_fin = s5_fin.transpose(2, 1, 0, 3)
        fin_re.append(s5_fin[..., :S5_N])
        fin_im.append(s5_fin[..., S5_N:])

        zero_row = jnp.zeros((1, 3 * GDN_W), qkv.dtype)
        prev_rows = jnp.concatenate([zero_row, edge[:-1, 1]], axis=0)
        next_rows = jnp.concatenate([edge[1:, 0], zero_row], axis=0)
        halo_prev = jnp.where(has_prev[:, None], prev_rows, 0.0)[:, None, :]
        halo_next = jnp.where(has_next[:, None], next_rows, 0.0)[:, None, :]
        pre = _gdn_pre(qkv, halo_prev, halo_next, conv_qkv[l], ba, bat, gdn_a_log[l], gdn_dt_bias[l])
        of_c, ob_c, gdn_fin = _gdn_scan(*pre, None, n_ctx, ctx_len, 0, True)
        of_l, ob_l = _gdn_scan(*pre, state_gdn[:, l], n_lat, lat_len, ctx_tok, False)
        fin_gdn.append(gdn_fin)

        tail = functools.partial(
            _out_mlp, x, mods_l, y_g, u, z, s5_d[l].reshape(1, S5_W), (of_c, ob_c), (of_l, ob_l), zg,
            gdn_norm[l].reshape(1, GDN_DK), w_out16, norm_mlp[l].reshape(1, D_MODEL),
            w_mlp_in16, w_mlp_out16, l, ctx_tok, lat_len)
        if l + 1 < depth:
            x = tail()
    g_fin = norm_final.reshape(1, D_MODEL)
    y_prompt = tail(g_final=g_fin, first_tok=0, n_tok=ctx_tok).reshape(n_ctx, ctx_len, D_MODEL)
    y_sample = tail(g_final=g_fin, first_tok=ctx_tok, n_tok=lat_tok).reshape(n_lat, lat_len, D_MODEL)
    return (y_prompt, y_sample, jnp.stack(fin_re, axis=1), jnp.stack(fin_im, axis=1),
            jnp.stack(fin_gdn, axis=1))
```
